```python
import jax
import jax.numpy as jnp
from jax import lax
import numpy as np

D_MODEL = 1024
BATCH = 8
SEQ = 4096
DEPTH = 2
DEC_BATCH = 32
DEC_SEQ = 1
PAST_LEN = 16384
PAGE_SIZE = 128

N_BR = 4
BR_W = D_MODEL // 4
HEAD_DIM = 64
N_HEADS = BR_W // HEAD_DIM
LORA_W = 64
LORA_A = 64
A_SHIFT_W = 3 * BR_W + LORA_W + LORA_A
ROPE_THETA = 500000.0
ROPE_DIMS = HEAD_DIM // 4
RET_THETA = 10000.0
RET_CHUNK = 64
IDX_HEADS = 4
IDX_DIM = 64
IDX_ROPE_DIMS = IDX_DIM // 4
DSA_TOPK_MAX = 256
DSA_Q_CHUNK = 64
MOBA_BLOCK = 256
MOBA_TOPK = 3
MOBA_Q_CHUNK = 16
RMS_EPS = 1e-6
RWKV_GN_EPS = 64e-5
RET_GN_EPS = 1e-5
COL_SIZES = (
    ('a_r', BR_W), ('a_k', BR_W), ('a_v', BR_W), ('a_wl', LORA_W), ('a_al', LORA_A), ('a_g', BR_W),
    ('b_q', BR_W), ('b_k', BR_W), ('b_v', BR_W), ('b_g', BR_W),
    ('c_q', BR_W), ('c_k', BR_W), ('c_v', BR_W), ('c_qi', IDX_HEADS * IDX_DIM), ('c_ki', IDX_DIM),
    ('c_wi', IDX_HEADS), ('c_g', BR_W),
    ('d_q', BR_W), ('d_k', BR_W), ('d_v', BR_W), ('d_g', BR_W),
)
N_COLS = sum(n for _, n in COL_SIZES)

kernel_name = 'hybrid_rwkv7_retention_dsa_moba_step'


def _rmsnorm(x, g):
    xf = x.astype(jnp.float32)
    y = xf * lax.rsqrt(jnp.mean(xf * xf, axis=-1, keepdims=True) + RMS_EPS)
    return (y * g.astype(jnp.float32)).astype(x.dtype)


def _heads(x, n=N_HEADS):
    return x.reshape(x.shape[:-1] + (n, x.shape[-1] // n))


def _head_norm(y, w, b, eps):
    mu = jnp.mean(y, axis=-1, keepdims=True)
    var = jnp.mean(jnp.square(y - mu), axis=-1, keepdims=True)
    yn = ((y - mu) * lax.rsqrt(var + eps)).reshape(y.shape[:-2] + (-1,))
    return yn * w.astype(jnp.float32) + b.astype(jnp.float32)


def _l2norm(x):
    return x * lax.rsqrt(jnp.sum(x * x, axis=-1, keepdims=True) + 1e-12)


def _rope(x, pos, rot_dims, theta):
    half = rot_dims // 2
    inv = jnp.power(jnp.float32(theta), -jnp.arange(half, dtype=jnp.float32) / half)
    ang = pos.astype(jnp.float32)[:, None] * inv[None, :]
    cos = jnp.cos(ang)[None, :, None, :]
    sin = jnp.sin(ang)[None, :, None, :]
    xf = x.astype(jnp.float32)
    x1 = xf[..., :half]
    x2 = xf[..., half:rot_dims]
    out = jnp.concatenate([x1 * cos - x2 * sin, x1 * sin + x2 * cos, xf[..., rot_dims:]], axis=-1)
    return out.astype(x.dtype)


def _to_chunks(x, c):
    b, t = x.shape[:2]
    return jnp.moveaxis(x.reshape((b, t // c, c) + x.shape[2:]), 1, 0)


def _from_chunks(y):
    y = jnp.moveaxis(y, 0, 1)
    return y.reshape((y.shape[0], y.shape[1] * y.shape[2]) + y.shape[3:])


def _split_cols(u):
    parts = {}
    off = 0
    for name, n in COL_SIZES:
        parts[name] = u[..., off:off + n]
        off += n
    return parts


def _gather_pages(pool, page_table):
    rows = pool[page_table]
    return rows.reshape((page_table.shape[0], page_table.shape[1] * PAGE_SIZE) + pool.shape[2:])


def _paged_rows(pool, page_table, new_rows, pos, head=None):
    past_len = page_table.shape[1] * PAGE_SIZE
    bi = jnp.arange(pos.shape[0]).reshape((-1,) + (1,) * (pos.ndim - 1))
    pc = jnp.minimum(pos, past_len - 1)
    phys = page_table[bi, pc // PAGE_SIZE]
    off = pc % PAGE_SIZE
    pn = jnp.clip(pos - past_len, 0, new_rows.shape[1] - 1)
    if head is None:
        rp = pool[phys, off]
        rn = new_rows[bi, pn]
    else:
        rp = pool[phys, off, head]
        rn = new_rows[bi, pn, head]
    in_past = (pos < past_len).reshape(pos.shape + (1,) * (rp.ndim - pos.ndim))
    return jnp.where(in_past, rp, rn.astype(rp.dtype))


def _rwkv7_branch(ua, ua_prev, wkv0, p):
    prev = jnp.concatenate([ua_prev[:, None, :].astype(ua.dtype), ua[:, :-1]], axis=1)
    xs = (ua + (prev - ua) * p['a_mu']).astype(jnp.float32)
    r = xs[..., :BR_W]
    k = xs[..., BR_W:2 * BR_W]
    v = xs[..., 2 * BR_W:3 * BR_W]
    wl = xs[..., 3 * BR_W:3 * BR_W + LORA_W]
    al = xs[..., 3 * BR_W + LORA_W:]
    w_log = -jax.nn.softplus(-(p['a_w0'] + jnp.tanh(wl) @ p['a_w2'])) - 0.5
    decay = jnp.exp(-jnp.exp(w_log))
    a = jax.nn.sigmoid(p['a_a0'] + al @ p['a_a2'])
    kk = _l2norm(_heads(k * p['a_kk']))
    k = k * (1.0 + (a - 1.0) * p['a_ka'])
    rh, kh, vh, wh, ah = _heads(r), _heads(k), _heads(v), _heads(decay), _heads(a)

    def step(S, inp):
        r_t, w_t, k_t, v_t, kk_t, a_t = inp
        sk = jnp.einsum('bhvk,bhk->bhv', S, kk_t)
        S = (S * w_t[:, :, None, :] - sk[..., None] * (kk_t * a_t)[:, :, None, :]
             + v_t[..., None] * k_t[:, :, None, :])
        return S, jnp.einsum('bhvk,bhk->bhv', S, r_t)

    seq = tuple(jnp.moveaxis(t, 1, 0) for t in (rh, wh, kh, vh, kk, ah))
    S, y = lax.scan(step, wkv0.astype(jnp.float32), seq)
    y = _head_norm(jnp.moveaxis(y, 0, 1), p['a_ln_w'], p['a_ln_b'], RWKV_GN_EPS)
    bonus = (jnp.sum(rh * kh * _heads(p['a_rk'].astype(jnp.float32)), axis=-1, keepdims=True) * vh)
    return y + bonus.reshape(y.shape), S


def _retention_scan(q, k, v, S0, chunk):
    h = q.shape[2]
    log_g = jnp.log(1.0 - jnp.power(2.0, -5.0 - jnp.arange(h, dtype=jnp.float32)))
    i = jnp.arange(chunk, dtype=jnp.float32)
    diff = i[:, None] - i[None, :]
    dmat = jnp.where(diff[None] >= 0, jnp.exp(jnp.maximum(diff, 0.0)[None] * log_g[:, None, None]), 0.0)
    dq = jnp.exp((i[:, None] + 1.0) * log_g[None, :])
    dk = jnp.exp((chunk - 1.0 - i)[:, None] * log_g[None, :])
    ds = jnp.exp(chunk * log_g)

    def step(S, inp):
        qc, kc, vc = inp
        att = jnp.einsum('bihd,bjhd->bhij', qc, kc) * dmat[None]
        y = (jnp.einsum('bhij,bjhe->bihe', att, vc)
             + jnp.einsum('bihd,bhde->bihe', qc, S) * dq[None, :, :, None])
        S = S * ds[None, :, None, None] + jnp.einsum('bjhd,bjhe->bhde', kc * dk[None, :, :, None], vc)
        return S, y

    S, y = lax.scan(step, S0, tuple(_to_chunks(t, chunk) for t in (q, k, v)))
    return _from_chunks(y), S


def _retention_branch(q, k, v, S0, chunk, p):
    y, S = _retention_scan(q.astype(jnp.float32), k.astype(jnp.float32) * HEAD_DIM ** -0.5,
                           v.astype(jnp.float32), S0.astype(jnp.float32), chunk)
    return _head_norm(y, p['b_gn_w'], p['b_gn_b'], RET_GN_EPS), S


def _dsa_attend(q, qpos, qi, wi, kidx_all, fetch, topk):
    L = kidx_all.shape[1]
    sc = jax.nn.relu(jnp.einsum('bqhd,bsd->bqhs', qi.astype(jnp.float32), kidx_all.astype(jnp.float32))
                     * IDX_DIM ** -0.5)
    sc = jnp.einsum('bqhs,bqh->bqs', sc, wi.astype(jnp.float32) * IDX_HEADS ** -0.5)
    causal = jnp.arange(L)[None, None, :] <= qpos[None, :, None]
    sc = jnp.where(causal, sc, -jnp.inf)
    _, sel = lax.top_k(sc, topk)
    valid = sel <= qpos[None, :, None]
    ks, vs = fetch(sel)
    att = jnp.einsum('bqhd,bqkhd->bqhk', q.astype(jnp.float32), ks.astype(jnp.float32)) * HEAD_DIM ** -0.5
    att = jnp.where(valid[:, :, None, :], att, -jnp.inf)
    pr = jax.nn.softmax(att, axis=-1)
    return jnp.einsum('bqhk,bqkhd->bqhd', pr, vs.astype(jnp.float32))


def _moba_attend(q, qpos, kbar, fetch):
    b, tq, h = q.shape[:3]
    nb = kbar.shape[1]
    nsel = min(MOBA_TOPK, nb - 1)
    qf = q.astype(jnp.float32)
    j = qpos // MOBA_BLOCK
    own = jnp.broadcast_to(j[None, :, None, None], (b, tq, h, 1))
    if nsel > 0:
        gate = jnp.einsum('bqhd,bnhd->bqhn', qf, kbar.astype(jnp.float32))
        past = jnp.arange(nb)[None, None, None, :] < j[None, :, None, None]
        gate = jnp.where(past, gate, -jnp.inf)
        _, sel = lax.top_k(gate, nsel)
        blocks = jnp.concatenate([sel, own], axis=-1)
        bvalid = jnp.concatenate([sel < j[None, :, None, None], jnp.ones(own.shape, bool)], axis=-1)
    else:
        blocks = own
        bvalid = jnp.ones(own.shape, bool)
    kpos = (blocks[..., None] * MOBA_BLOCK + jnp.arange(MOBA_BLOCK)).reshape(b, tq, h, -1)
    kvalid = jnp.repeat(bvalid, MOBA_BLOCK, axis=-1) & (kpos <= qpos[None, :, None, None])
    ks, vs = fetch(kpos)
    att = jnp.einsum('bqhd,bqhpd->bqhp', qf, ks.astype(jnp.float32)) * HEAD_DIM ** -0.5
    att = jnp.where(kvalid, att, -jnp.inf)
    pr = jax.nn.softmax(att, axis=-1)
    return jnp.einsum('bqhp,bqhpd->bqhd', pr, vs.astype(jnp.float32))


def _dsa_prompt(q, qi, wi, k, v, kidx, pos):
    b, t = q.shape[:2]
    topk = min(DSA_TOPK_MAX, t // 4)
    bi = jnp.arange(b)[:, None, None]

    def fetch(sel):
        return k[bi, sel], v[bi, sel]

    def one(args):
        qb, qib, wib, pb = args
        return _dsa_attend(qb, pb, qib, wib, kidx, fetch, topk)

    out = lax.map(one, (_to_chunks(q, DSA_Q_CHUNK), _to_chunks(qi, DSA_Q_CHUNK),
                        _to_chunks(wi, DSA_Q_CHUNK), pos.reshape(-1, DSA_Q_CHUNK)))
    return _from_chunks(out)


def _moba_prompt(q, k, v, pos):
    b, t, h, d = q.shape
    nb = -(-t // MOBA_BLOCK)
    pad = nb * MOBA_BLOCK - t
    kp = jnp.pad(k, ((0, 0), (0, pad), (0, 0), (0, 0)))
    vp = jnp.pad(v, ((0, 0), (0, pad), (0, 0), (0, 0)))
    kbar = kp.astype(jnp.float32).reshape(b, nb, MOBA_BLOCK, h, d).mean(axis=2)
    kT = jnp.swapaxes(kp, 1, 2)
    vT = jnp.swapaxes(vp, 1, 2)
    bi = jnp.arange(b)[:, None, None, None]
    hi = jnp.arange(h)[None, None, :, None]

    def fetch(kpos):
        return kT[bi, hi, kpos], vT[bi, hi, kpos]

    def one(args):
        qb, pb = args
        return _moba_attend(qb, pb, kbar, fetch)

    out = lax.map(one, (_to_chunks(q, MOBA_Q_CHUNK), pos.reshape(-1, MOBA_Q_CHUNK)))
    return _from_chunks(out)


def _layer_in(x, c, p):
    mod = c @ p['w_ada'] + p['b_ada']
    shift = mod[:, :D_MODEL]
    scale = mod[:, D_MODEL:2 * D_MODEL]
    gate = mod[:, 2 * D_MODEL:]
    h = _rmsnorm(x, p['g_pre']) * (1.0 + scale[:, None, :]) + shift[:, None, :]
    return h, h @ p['w_in'], gate


def _mixer_inputs(u, pos):
    cols = _split_cols(u)
    qb = _rope(_heads(cols['b_q']), pos, HEAD_DIM, RET_THETA)
    kb = _rope(_heads(cols['b_k']), pos, HEAD_DIM, RET_THETA)
    vb = _heads(cols['b_v'])
    qc = _rope(_heads(cols['c_q']), pos, ROPE_DIMS, ROPE_THETA)
    kc = _rope(_heads(cols['c_k']), pos, ROPE_DIMS, ROPE_THETA)
    vc = _heads(cols['c_v'])
    qic = _rope(_heads(cols['c_qi'], IDX_HEADS), pos, IDX_ROPE_DIMS, ROPE_THETA)
    kic = _rope(cols['c_ki'][..., None, :], pos, IDX_ROPE_DIMS, ROPE_THETA)[..., 0, :]
    wic = cols['c_wi']
    qd = _rope(_heads(cols['d_q']), pos, ROPE_DIMS, ROPE_THETA)
    kd = _rope(_heads(cols['d_k']), pos, ROPE_DIMS, ROPE_THETA)
    vd = _heads(cols['d_v'])
    gates = (cols['a_g'], cols['b_g'], cols['c_g'], cols['d_g'])
    return u[..., :A_SHIFT_W], (qb, kb, vb), (qc, kc, vc, qic, kic, wic), (qd, kd, vd), gates


def _layer_out(x, h, outs, gate_paths, gate, p):
    merged = None
    for n in range(N_BR):
        o = outs[n] * jax.nn.silu(gate_paths[n])
        term = jax.nn.sigmoid(h @ p['w_merge'][n]) * (o @ p['w_branch'][n])
        merged = term if merged is None else merged + term
    y = merged @ p['w_out']
    return x + gate[:, None, :] * _rmsnorm(y, p['g_post'])


def _prompt_layer(x, c, p):
    b, t = x.shape[:2]
    pos = jnp.arange(t)
    h, u, gate = _layer_in(x, c, p)
    ua, (qb, kb, vb), (qc, kc, vc, qic, kic, wic), (qd, kd, vd), gates = _mixer_inputs(u, pos)
    zero_state = jnp.zeros((b, N_HEADS, HEAD_DIM, HEAD_DIM), jnp.float32)
    ya, wkv = _rwkv7_branch(ua, jnp.zeros((b, A_SHIFT_W), ua.dtype), zero_state, p)
    chunk = RET_CHUNK if t % RET_CHUNK == 0 else t
    yb, ret = _retention_branch(qb, kb, vb, zero_state, chunk, p)
    yc = _dsa_prompt(qc, qic, wic, kc, vc, kic, pos).reshape(b, t, BR_W)
    yd = _moba_prompt(qd, kd, vd, pos).reshape(b, t, BR_W)
    outs = tuple(y.astype(x.dtype) for y in (ya, yb, yc, yd))
    x_new = _layer_out(x, h, outs, gates, gate, p)
    new_state = (ua[:, -1], wkv.astype(x.dtype), ret.astype(x.dtype), kc, vc, kic, kd, vd)
    return x_new, new_state


def _sample_layer(x, c, p, a_shift, a_wkv, b_ret, ck_pool, cv_pool, cki_pool, dk_pool, dv_pool, page_table):
    b, t = x.shape[:2]
    past_len = page_table.shape[1] * PAGE_SIZE
    L = past_len + t
    pos = past_len + jnp.arange(t)
    h, u, gate = _layer_in(x, c, p)
    ua, (qb, kb, vb), (qc, kc, vc, qic, kic, wic), (qd, kd, vd), gates = _mixer_inputs(u, pos)
    ya, wkv = _rwkv7_branch(ua, a_shift, a_wkv, p)
    yb, ret = _retention_branch(qb, kb, vb, b_ret, t, p)
    kidx_all = jnp.concatenate([_gather_pages(cki_pool, page_table), kic.astype(cki_pool.dtype)], axis=1)

    def fetch_c(sel):
        return _paged_rows(ck_pool, page_table, kc, sel), _paged_rows(cv_pool, page_table, vc, sel)

    yc = _dsa_attend(qc, pos, qic, wic, kidx_all, fetch_c, min(DSA_TOPK_MAX, L // 4)).reshape(b, t, BR_W)
    nb = -(-L // MOBA_BLOCK)
    k_all = jnp.concatenate([_gather_pages(dk_pool, page_table), kd.astype(dk_pool.dtype)], axis=1)
    k_all = jnp.pad(k_all, ((0, 0), (0, nb * MOBA_BLOCK - L), (0, 0), (0, 0)))
    kbar = k_all.astype(jnp.float32).reshape(b, nb, MOBA_BLOCK, N_HEADS, HEAD_DIM).mean(axis=2)
    hi = jnp.arange(N_HEADS)[None, None, :, None]

    def fetch_d(kpos):
        return (_paged_rows(dk_pool, page_table, kd, kpos, hi),
                _paged_rows(dv_pool, page_table, vd, kpos, hi))

    yd = _moba_attend(qd, pos, kbar, fetch_d).reshape(b, t, BR_W)
    outs = tuple(y.astype(x.dtype) for y in (ya, yb, yc, yd))
    x_new = _layer_out(x, h, outs, gates, gate, p)
    new_state = (ua[:, -1], wkv.astype(x.dtype), ret.astype(x.dtype), kc, vc, kic, kd, vd)
    return x_new, new_state


def setup_inputs(seed: int = 0) -> dict:
    key = jax.random.key(seed)
    ks = jax.random.split(key, 36)
    n_pages = PAST_LEN // PAGE_SIZE
    n_used = DEC_BATCH * n_pages
    n_pool = n_used + max(1, n_used // 4)

    def nrm(k, shape, s=1.0):
        return s * jax.random.normal(k, shape, jnp.float32)

    kv_shape = (DEPTH, n_pool, PAGE_SIZE, N_HEADS, HEAD_DIM)
    st_shape = (DEPTH, DEC_BATCH, N_HEADS, HEAD_DIM, HEAD_DIM)
    page_table = jax.random.permutation(ks[0], n_pool)[:n_used].reshape(DEC_BATCH, n_pages).astype(jnp.int32)
    return {
        'x_prompt': nrm(ks[1], (BATCH, SEQ, D_MODEL)),
        'x_sample': nrm(ks[2], (DEC_BATCH, DEC_SEQ, D_MODEL)),
        'c_prompt': nrm(ks[3], (BATCH, D_MODEL)),
        'c_sample': nrm(ks[4], (DEC_BATCH, D_MODEL)),
        'state_a_shift': nrm(ks[5], (DEPTH, DEC_BATCH, A_SHIFT_W)),
        'state_a_wkv': nrm(ks[6], st_shape, 0.5),
        'state_b_ret': nrm(ks[7], st_shape, 2.0),
        'cache_c_k': nrm(ks[8], kv_shape),
        'cache_c_v': nrm(ks[9], kv_shape),
        'cache_c_kidx': nrm(ks[10], (DEPTH, n_pool, PAGE_SIZE, IDX_DIM)),
        'cache_d_k': nrm(ks[11], kv_shape),
        'cache_d_v': nrm(ks[12], kv_shape),
        'page_table': page_table,
        'w_ada': nrm(ks[13], (DEPTH, D_MODEL, 3 * D_MODEL), 0.5 * D_MODEL ** -0.5),
        'b_ada': nrm(ks[14], (DEPTH, 3 * D_MODEL), 0.01),
        'g_pre': 1.0 + nrm(ks[15], (DEPTH, D_MODEL), 0.1),
        'g_post': 1.0 + nrm(ks[16], (DEPTH, D_MODEL), 0.1),
        'w_in': nrm(ks[17], (DEPTH, D_MODEL, N_COLS), D_MODEL ** -0.5),
        'a_mu': jax.random.uniform(ks[18], (DEPTH, A_SHIFT_W), jnp.float32),
        'a_w0': jax.random.uniform(ks[19], (DEPTH, BR_W), jnp.float32, -3.0, 1.0),
        'a_w2': nrm(ks[20], (DEPTH, LORA_W, BR_W), 0.5 * LORA_W ** -0.5),
        'a_a0': nrm(ks[21], (DEPTH, BR_W), 0.1),
        'a_a2': nrm(ks[22], (DEPTH, LORA_A, BR_W), 0.5 * LORA_A ** -0.5),
        'a_kk': 0.85 + nrm(ks[23], (DEPTH, BR_W), 0.05),
        'a_ka': 1.0 + nrm(ks[24], (DEPTH, BR_W), 0.05),
        'a_rk': nrm(ks[25], (DEPTH, BR_W), 0.1),
        'a_ln_w': 1.0 + nrm(ks[26], (DEPTH, BR_W), 0.1),
        'a_ln_b': nrm(ks[27], (DEPTH, BR_W), 0.01),
        'b_gn_w': 1.0 + nrm(ks[28], (DEPTH, BR_W), 0.1),
        'b_gn_b': nrm(ks[29], (DEPTH, BR_W), 0.01),
        'w_branch': nrm(ks[30], (DEPTH, N_BR, BR_W, D_MODEL), BR_W ** -0.5),
        'w_merge': nrm(ks[31], (DEPTH, N_BR, D_MODEL, D_MODEL), D_MODEL ** -0.5),
        'w_out': nrm(ks[32], (DEPTH, D_MODEL, D_MODEL), D_MODEL ** -0.5),
    }


def reference(x_prompt, x_sample, c_prompt, c_sample, state_a_shift, state_a_wkv, state_b_ret,
              cache_c_k, cache_c_v, cache_c_kidx, cache_d_k, cache_d_v, page_table,
              w_ada, b_ada, g_pre, g_post, w_in, a_mu, a_w0, a_w2, a_a0, a_a2, a_kk, a_ka, a_rk,
              a_ln_w, a_ln_b, b_gn_w, b_gn_b, w_branch, w_merge, w_out):
    xp = x_prompt
    xs = x_sample
    p_new = []
    s_new = []
    for l in range(DEPTH):
        p = {'w_ada': w_ada[l], 'b_ada': b_ada[l], 'g_pre': g_pre[l], 'g_post': g_post[l], 'w_in': w_in[l],
             'a_mu': a_mu[l], 'a_w0': a_w0[l], 'a_w2': a_w2[l], 'a_a0': a_a0[l], 'a_a2': a_a2[l],
             'a_kk': a_kk[l], 'a_ka': a_ka[l], 'a_rk': a_rk[l], 'a_ln_w': a_ln_w[l], 'a_ln_b': a_ln_b[l],
             'b_gn_w': b_gn_w[l], 'b_gn_b': b_gn_b[l], 'w_branch': w_branch[l], 'w_merge': w_merge[l],
             'w_out': w_out[l]}
        xp, st_p = _prompt_layer(xp, c_prompt, p)
        xs, st_s = _sample_layer(xs, c_sample, p, state_a_shift[l], state_a_wkv[l], state_b_ret[l],
                                 cache_c_k[l], cache_c_v[l], cache_c_kidx[l], cache_d_k[l], cache_d_v[l],
                                 page_table)
        p_new.append(st_p)
        s_new.append(st_s)
    p_a_shift = jnp.stack([s[0] for s in p_new])
    p_a_wkv = jnp.stack([s[1] for s in p_new])
    p_b_ret = jnp.stack([s[2] for s in p_new])
    p_c_k = jnp.stack([s[3] for s in p_new])
    p_c_v = jnp.stack([s[4] for s in p_new])
    p_c_kidx = jnp.stack([s[5] for s in p_new])
    p_d_k = jnp.stack([s[6] for s in p_new])
    p_d_v = jnp.stack([s[7] for s in p_new])
    s_a_shift = jnp.stack([s[0] for s in s_new])
    s_a_wkv = jnp.stack([s[1] for s in s_new])
    s_b_ret = jnp.stack([s[2] for s in s_new])
    s_c_k = jnp.stack([s[3] for s in s_new])
    s_c_v = jnp.stack([s[4] for s in s_new])
    s_c_kidx = jnp.stack([s[5] for s in s_new])
    s_d_k = jnp.stack([s[6] for s in s_new])
    s_d_v = jnp.stack([s[7] for s in s_new])
    return (xp, xs, p_a_shift, p_a_wkv, p_b_ret, p_c_k, p_c_v, p_c_kidx, p_d_k, p_d_v,
            s_a_shift, s_a_wkv, s_b_ret, s_c_k, s_c_v, s_c_kidx, s_d_k, s_d_v)
```

```python
import functools

import jax
import jax.numpy as jnp
import numpy as np
from jax import lax
from jax.experimental import pallas as pl
from jax.experimental.pallas import tpu as pltpu

F32 = jnp.float32
BF16 = jnp.bfloat16
I32 = jnp.int32

D_MODEL = 1024
PAGE_SIZE = 128
BR_W = 256
HEAD_DIM = 64
N_HEADS = 4
LORA_W = 64
LORA_A = 64
A_SHIFT_W = 3 * BR_W + LORA_W + LORA_A
ROPE_THETA = 500000.0
ROPE_DIMS = HEAD_DIM // 4
RET_THETA = 10000.0
IDX_HEADS = 4
IDX_DIM = 64
DSA_TOPK_MAX = 256
MOBA_BLOCK = 256
MOBA_TOPK = 3
RMS_EPS = 1e-6
RWKV_GN_EPS = 64e-5
RET_GN_EPS = 1e-5
NEG = -1e30
VMEM_LIMIT_BYTES = 56 * 1024 * 1024

_C = {}
_off = 0
for _name, _n in (('a_r', 256), ('a_k', 256), ('a_v', 256), ('a_wl', 64), ('a_al', 64), ('a_g', 256),
                  ('b_q', 256), ('b_k', 256), ('b_v', 256), ('b_g', 256),
                  ('c_q', 256), ('c_k', 256), ('c_v', 256), ('c_qi', 256), ('c_ki', 64),
                  ('c_wi', 4), ('c_g', 256),
                  ('d_q', 256), ('d_k', 256), ('d_v', 256), ('d_g', 256)):
    _C[_name] = (_off, _off + _n)
    _off += _n

W_A, W_G, W_B, W_C, W_I, W_D = 896, 1024, 768, 768, 640, 768
OFF_A = 0
OFF_G = OFF_A + W_A
OFF_B = OFF_G + W_G
OFF_C = OFF_B + W_B
OFF_I = OFF_C + W_C
OFF_D = OFF_I + W_I
W_ALL = OFF_D + W_D


def _cparams(sem):
    return pltpu.CompilerParams(dimension_semantics=sem, vmem_limit_bytes=VMEM_LIMIT_BYTES)


def _split(x):
    hi = x.astype(BF16)
    lo = (x - hi.astype(F32)).astype(BF16)
    return hi, lo


def _dot(a, b):
    return jnp.dot(a, b, preferred_element_type=F32)


def _dot_nt(a, b):
    return lax.dot_general(a, b, (((1,), (1,)), ((), ())), preferred_element_type=F32)


def _mm1(a, b):
    return _dot(a.astype(BF16), b.astype(BF16))


def _mm3(a, b_hi, b_lo):
    a_hi, a_lo = _split(a)
    return _dot(a_hi, b_hi) + (_dot(a_lo, b_hi) + _dot(a_hi, b_lo))


def _mm3_nt(a, b):
    a_hi, a_lo = _split(a)
    b_hi, b_lo = _split(b)
    return _dot_nt(a_hi, b_hi) + (_dot_nt(a_lo, b_hi) + _dot_nt(a_hi, b_lo))


def _head_ones():
    r = lax.broadcasted_iota(I32, (BR_W, BR_W), 0) // HEAD_DIM
    c = lax.broadcasted_iota(I32, (BR_W, BR_W), 1) // HEAD_DIM
    return jnp.where(r == c, 1.0, 0.0).astype(BF16)


def _headsum(x, bd):
    hi, lo = _split(x)
    return _dot(hi, bd) + _dot(lo, bd)


def _head_mask(h, shape):
    c = lax.broadcasted_iota(I32, shape, len(shape) - 1) // HEAD_DIM
    return c == h


def _ada_kernel(c_ref, w_ref, b_ref, o_ref):
    w = w_ref[...]
    w_hi, w_lo = _split(w)
    o_ref[...] = _mm3(c_ref[...], w_hi, w_lo) + b_ref[...]


def _ada(c, w_ada, b_ada):
    bc = c.shape[0]
    n = w_ada.shape[1]
    tn = 1024
    return pl.pallas_call(
        _ada_kernel,
        grid=(n // tn,),
        in_specs=[pl.BlockSpec((bc, D_MODEL), lambda j: (0, 0)),
                  pl.BlockSpec((D_MODEL, tn), lambda j: (0, j)),
                  pl.BlockSpec((1, tn), lambda j: (0, j))],
        out_specs=pl.BlockSpec((bc, tn), lambda j: (0, j)),
        out_shape=jax.ShapeDtypeStruct((bc, n), F32),
        compiler_params=_cparams(("arbitrary",)),
        name="ada",
    )(c, w_ada, b_ada.reshape(1, n))


def _rope_tables(pos, rot_dims, theta, width):
    half = rot_dims // 2
    inv = jnp.power(jnp.float32(theta), -jnp.arange(half, dtype=jnp.float32) / half)
    ang = pos.astype(jnp.float32)[:, None] * inv[None, :]
    cos = jnp.cos(ang)
    sin = jnp.sin(ang)
    t = pos.shape[0]
    one = jnp.ones((t, HEAD_DIM - rot_dims), F32)
    zero = jnp.zeros((t, HEAD_DIM - rot_dims), F32)
    zh = jnp.zeros((t, half), F32)
    cos_h = jnp.concatenate([cos, cos, one], axis=1)
    up_h = jnp.concatenate([-sin, zh, zero], axis=1)
    dn_h = jnp.concatenate([zh, sin, zero], axis=1)
    reps = width // HEAD_DIM
    return jnp.stack([jnp.tile(cos_h, (1, reps)), jnp.tile(up_h, (1, reps)), jnp.tile(dn_h, (1, reps))])


def _rope_apply(x, tab_ref, lo, hi, half):
    n = hi - lo
    cos = tab_ref[0, :, lo:hi]
    up = tab_ref[1, :, lo:hi]
    dn = tab_ref[2, :, lo:hi]
    return x * cos + pltpu.roll(x, n - half, 1) * up + pltpu.roll(x, half, 1) * dn


def _layer_in_kernel(x_ref, mod_ref, g_ref, wh_ref, wl_ref, tr_ref, ts_ref,
                     ua_ref, gt_ref, qb_ref, kb_ref, vb_ref, qc_ref, kc_ref, vc_ref,
                     qi_ref, ki_ref, wi_ref, qd_ref, kd_ref, vd_ref):
    x = x_ref[0]
    y = x * lax.rsqrt(jnp.mean(x * x, axis=-1, keepdims=True) + RMS_EPS) * g_ref[...]
    shift = mod_ref[0, :, 0:D_MODEL]
    scale = mod_ref[0, :, D_MODEL:2 * D_MODEL]
    h = y * (1.0 + scale) + shift
    h_hi, h_lo = _split(h)

    def proj(lo, hi):
        b_hi = wh_ref[:, lo:hi]
        b_lo = wl_ref[:, lo:hi]
        return _dot(h_hi, b_hi) + (_dot(h_lo, b_hi) + _dot(h_hi, b_lo))

    ua_ref[0] = proj(OFF_A, OFF_A + W_A)
    gt_ref[0] = proj(OFF_G, OFF_G + W_G)
    ub = proj(OFF_B, OFF_B + W_B)
    qb_ref[0] = _rope_apply(ub[:, 0:256], tr_ref, 0, 256, HEAD_DIM // 2)
    kb_ref[0] = _rope_apply(ub[:, 256:512], tr_ref, 0, 256, HEAD_DIM // 2)
    vb_ref[0] = ub[:, 512:768]
    uc = proj(OFF_C, OFF_C + W_C)
    qc_ref[0] = _rope_apply(uc[:, 0:256], ts_ref, 0, 256, ROPE_DIMS // 2)
    kc_ref[0] = _rope_apply(uc[:, 256:512], ts_ref, 0, 256, ROPE_DIMS // 2)
    vc_ref[0] = uc[:, 512:768]
    ui = proj(OFF_I, OFF_I + W_I)
    qi_ref[0] = _rope_apply(ui[:, 0:256], ts_ref, 0, 256, ROPE_DIMS // 2)
    ki_ref[0] = _rope_apply(ui[:, 256:512], ts_ref, 0, 256, ROPE_DIMS // 2)
    wi_ref[0] = ui[:, 512:640]
    ud = proj(OFF_D, OFF_D + W_D)
    qd_ref[0] = _rope_apply(ud[:, 0:256], ts_ref, 0, 256, ROPE_DIMS // 2)
    kd_ref[0] = _rope_apply(ud[:, 256:512], ts_ref, 0, 256, ROPE_DIMS // 2)
    vd_ref[0] = ud[:, 512:768]


def _regroup_w_in(w_in):
    def cols(name):
        lo, hi = _C[name]
        return w_in[:, lo:hi]
    ki4 = jnp.tile(cols('c_ki'), (1, 4))
    wi_pad = jnp.pad(cols('c_wi'), ((0, 0), (0, 124)))
    w = jnp.concatenate([
        cols('a_r'), cols('a_k'), cols('a_v'), cols('a_wl'), cols('a_al'),
        cols('a_g'), cols('b_g'), cols('c_g'), cols('d_g'),
        cols('b_q'), cols('b_k'), cols('b_v'),
        cols('c_q'), cols('c_k'), cols('c_v'),
        cols('c_qi'), ki4, wi_pad,
        cols('d_q'), cols('d_k'), cols('d_v')], axis=1)
    return _split(w)


def _layer_in(x, mod, g_pre, w_hi, w_lo, tab_ret, tab_std, tm):
    b, t, _ = x.shape
    r = mod.shape[1]
    if r == 1:
        mod_spec = pl.BlockSpec((1, 1, 3 * D_MODEL), lambda i, j: (i, 0, 0))
    else:
        mod_spec = pl.BlockSpec((1, tm, 3 * D_MODEL), lambda i, j: (i, j, 0))
    widths = (W_A, W_G, 256, 256, 256, 256, 256, 256, 256, 256, 128, 256, 256, 256)
    out_shape = tuple(jax.ShapeDtypeStruct((b, t, w), F32) for w in widths)
    out_specs = tuple(pl.BlockSpec((1, tm, w), lambda i, j: (i, j, 0)) for w in widths)
    return pl.pallas_call(
        _layer_in_kernel,
        grid=(b, t // tm),
        in_specs=[pl.BlockSpec((1, tm, D_MODEL), lambda i, j: (i, j, 0)),
                  mod_spec,
                  pl.BlockSpec((1, D_MODEL), lambda i, j: (0, 0)),
                  pl.BlockSpec((D_MODEL, W_ALL), lambda i, j: (0, 0), pipeline_mode=pl.Buffered(1)),
                  pl.BlockSpec((D_MODEL, W_ALL), lambda i, j: (0, 0), pipeline_mode=pl.Buffered(1)),
                  pl.BlockSpec((3, tm, 256), lambda i, j: (0, j, 0)),
                  pl.BlockSpec((3, tm, 256), lambda i, j: (0, j, 0))],
        out_specs=out_specs,
        out_shape=out_shape,
        compiler_params=_cparams(("arbitrary", "arbitrary")),
        name="layer_in",
    )(x, mod, g_pre.reshape(1, D_MODEL), w_hi, w_lo, tab_ret, tab_std)


def _rwkv_pre_math(ua, prev, mu, w0, w2h, w2l, a0, a2h, a2l, kkp, ka, rk, bd):
    xs = ua + (prev - ua) * mu
    r = xs[:, 0:256]
    k = xs[:, 256:512]
    v = xs[:, 512:768]
    wl = xs[:, 768:832]
    al = xs[:, 832:896]
    zw = w0 + _mm3(jnp.tanh(wl), w2h, w2l)
    w_log = -jax.nn.softplus(-zw) - 0.5
    decay = jnp.exp(-jnp.exp(w_log))
    a = jax.nn.sigmoid(a0 + _mm3(al, a2h, a2l))
    kq = k * kkp
    kk = kq * lax.rsqrt(_headsum(kq * kq, bd) + 1e-12)
    k2 = k * (1.0 + (a - 1.0) * ka)
    bonus = _headsum(r * k2 * rk, bd) * v
    return r, decay, k2, v, kk, kk * a, bonus


def _rwkv_pre_shift_kernel(ua_ref, up_ref, p0_ref, mu_ref, w0_ref, w2h_ref, w2l_ref, a0_ref, a2h_ref, a2l_ref,
                           kkp_ref, ka_ref, rk_ref, r_ref, w_ref, k_ref, v_ref, kk_ref, b_ref, bo_ref):
    j = pl.program_id(1)
    ua = ua_ref[0]
    tm = ua.shape[0]
    first = jnp.where(j == 0, p0_ref[0], up_ref[0, 7:8, :])
    row = lax.broadcasted_iota(I32, ua.shape, 0)
    prev = jnp.where(row == 0, first, pltpu.roll(ua, 1, 0))
    outs = _rwkv_pre_math(ua, prev, mu_ref[...], w0_ref[...], w2h_ref[...], w2l_ref[...], a0_ref[...],
                          a2h_ref[...], a2l_ref[...], kkp_ref[...], ka_ref[...], rk_ref[...], _head_ones())
    for o_ref, o in zip((r_ref, w_ref, k_ref, v_ref, kk_ref, b_ref, bo_ref), outs):
        o_ref[0] = o


def _rwkv_pre_rows_kernel(ua_ref, pv_ref, mu_ref, w0_ref, w2h_ref, w2l_ref, a0_ref, a2h_ref, a2l_ref,
                          kkp_ref, ka_ref, rk_ref, r_ref, w_ref, k_ref, v_ref, kk_ref, b_ref, bo_ref):
    outs = _rwkv_pre_math(ua_ref[...], pv_ref[...], mu_ref[...], w0_ref[...], w2h_ref[...], w2l_ref[...],
                          a0_ref[...], a2h_ref[...], a2l_ref[...], kkp_ref[...], ka_ref[...], rk_ref[...],
                          _head_ones())
    for o_ref, o in zip((r_ref, w_ref, k_ref, v_ref, kk_ref, b_ref, bo_ref), outs):
        o_ref[...] = o


def _rwkv_params(p):
    w2h, w2l = _split(p['a_w2'])
    a2h, a2l = _split(p['a_a2'])
    row = lambda v: v.reshape(1, -1)
    return (row(p['a_mu']), row(p['a_w0']), w2h, w2l, row(p['a_a0']), a2h, a2l,
            row(p['a_kk']), row(p['a_ka']), row(p['a_rk']))


def _rwkv_pre_prompt(ua, prev0, p, tm):
    b, t, _ = ua.shape
    prm = _rwkv_params(p)
    full = lambda a: pl.BlockSpec(a.shape, lambda i, j: (0,) * a.ndim)
    blk8 = tm // 8
    return pl.pallas_call(
        _rwkv_pre_shift_kernel,
        grid=(b, t // tm),
        in_specs=[pl.BlockSpec((1, tm, A_SHIFT_W), lambda i, j: (i, j, 0)),
                  pl.BlockSpec((1, 8, A_SHIFT_W), lambda i, j: (i, jnp.maximum(j * blk8 - 1, 0), 0)),
                  pl.BlockSpec((1, 1, A_SHIFT_W), lambda i, j: (i, 0, 0))] + [full(a) for a in prm],
        out_specs=tuple(pl.BlockSpec((1, tm, BR_W), lambda i, j: (i, j, 0)) for _ in range(7)),
        out_shape=tuple(jax.ShapeDtypeStruct((b, t, BR_W), F32) for _ in range(7)),
        compiler_params=_cparams(("arbitrary", "arbitrary")),
        name="rwkv_pre",
    )(ua, ua, prev0.reshape(b, 1, A_SHIFT_W), *prm)


def _rwkv_pre_rows(ua, prev, p):
    n = ua.shape[0]
    prm = _rwkv_params(p)
    full = lambda a: pl.BlockSpec(a.shape, lambda: (0,) * a.ndim)
    return pl.pallas_call(
        _rwkv_pre_rows_kernel,
        in_specs=[full(ua), full(prev)] + [full(a) for a in prm],
        out_specs=tuple(pl.BlockSpec((n, BR_W), lambda: (0, 0)) for _ in range(7)),
        out_shape=tuple(jax.ShapeDtypeStruct((n, BR_W), F32) for _ in range(7)),
        name="rwkv_pre_rows",
    )(ua, prev, *prm)


def _rwkv_scan_kernel(r_ref, w_ref, k_ref, v_ref, kk_ref, b_ref, s0_ref, y_ref, sout_ref, s_scr, *, tc):
    c = pl.program_id(0)

    @pl.when(c == 0)
    def _():
        s_scr[...] = s0_ref[...]

    nb = s_scr.shape[0]
    bd = _head_ones()
    vi = lax.broadcasted_iota(I32, (HEAD_DIM, BR_W), 0)
    ci = lax.broadcasted_iota(I32, (HEAD_DIM, BR_W), 1)
    diag = ((ci % HEAD_DIM) == vi)[None]

    def hsum(x):
        hi, lo = _split(x.reshape(nb * HEAD_DIM, BR_W))
        return (_dot(hi, bd) + _dot(lo, bd)).reshape(nb, HEAD_DIM, BR_W)

    def step(t, carry):
        row = lambda ref: ref[:, pl.ds(t, 1), :]
        s = s_scr[...]
        sk = hsum(s * row(kk_ref))
        vcol = hsum(jnp.where(diag, row(v_ref), 0.0))
        s = s * row(w_ref) - sk * row(b_ref) + vcol * row(k_ref)
        s_scr[...] = s
        yb = hsum(s * row(r_ref))
        y_ref[:, pl.ds(t, 1), :] = jnp.sum(jnp.where(diag, yb, 0.0), axis=1, keepdims=True)
        return carry

    lax.fori_loop(0, tc, step, 0)

    @pl.when(c == pl.num_programs(0) - 1)
    def _():
        sout_ref[...] = s_scr[...]


def _rwkv_scan(r, w, k, v, kk, bb, s0, tc):
    b, t, _ = r.shape
    seq = pl.BlockSpec((b, tc, BR_W), lambda c: (0, c, 0))
    st = pl.BlockSpec((b, HEAD_DIM, BR_W), lambda c: (0, 0, 0))
    return pl.pallas_call(
        functools.partial(_rwkv_scan_kernel, tc=tc),
        grid=(t // tc,),
        in_specs=[seq] * 6 + [st],
        out_specs=(seq, st),
        out_shape=(jax.ShapeDtypeStruct((b, t, BR_W), F32), jax.ShapeDtypeStruct((b, HEAD_DIM, BR_W), F32)),
        scratch_shapes=[pltpu.VMEM((b, HEAD_DIM, BR_W), F32)],
        compiler_params=_cparams(("arbitrary",)),
        name="rwkv_scan",
    )(r, w, k, v, kk, bb, s0)


def _wkv_to_scan_layout(s):
    b = s.shape[0]
    return jnp.transpose(s, (0, 2, 1, 3)).reshape(b, HEAD_DIM, BR_W)


def _wkv_from_scan_layout(s):
    b = s.shape[0]
    return jnp.transpose(s.reshape(b, HEAD_DIM, N_HEADS, HEAD_DIM), (0, 2, 1, 3))


def _rwkv_step_kernel(r_ref, w_ref, k_ref, vc_ref, kk_ref, b_ref, s_ref, y_ref, so_ref):
    s = s_ref[...]
    sk = jnp.sum(s * kk_ref[...], axis=-1, keepdims=True)
    s = s * w_ref[...] - sk * b_ref[...] + vc_ref[...] * k_ref[...]
    so_ref[...] = s
    y_ref[...] = jnp.sum(s * r_ref[...], axis=-1, keepdims=True)


def _rwkv_step(r, w, k, v, kk, bb, s0):
    b = r.shape[0]
    n = b * N_HEADS
    rowf = lambda a: a.reshape(n, 1, HEAD_DIM)
    full = lambda shape: pl.BlockSpec(shape, lambda: (0,) * len(shape))
    y, s = pl.pallas_call(
        _rwkv_step_kernel,
        in_specs=[full((n, 1, HEAD_DIM))] * 3 + [full((n, HEAD_DIM, 1))] + [full((n, 1, HEAD_DIM))] * 2
                 + [full((n, HEAD_DIM, HEAD_DIM))],
        out_specs=(full((n, HEAD_DIM, 1)), full((n, HEAD_DIM, HEAD_DIM))),
        out_shape=(jax.ShapeDtypeStruct((n, HEAD_DIM, 1), F32), jax.ShapeDtypeStruct((n, HEAD_DIM, HEAD_DIM), F32)),
        compiler_params=pltpu.CompilerParams(vmem_limit_bytes=VMEM_LIMIT_BYTES),
        name="rwkv_step",
    )(rowf(r), rowf(w), rowf(k), v.reshape(n, HEAD_DIM, 1), rowf(kk), rowf(bb), s0.reshape(n, HEAD_DIM, HEAD_DIM))
    return y.reshape(b, BR_W), s.reshape(b, N_HEADS, HEAD_DIM, HEAD_DIM)


def _ret_tables(c):
    log_g = jnp.log(1.0 - jnp.power(2.0, -5.0 - jnp.arange(N_HEADS, dtype=jnp.float32)))
    i = jnp.arange(c, dtype=jnp.float32)
    diff = i[:, None] - i[None, :]
    dmat = jnp.where(diff[None] >= 0, jnp.exp(jnp.maximum(diff, 0.0)[None] * log_g[:, None, None]), 0.0)
    dq = jnp.exp((i[:, None] + 1.0) * log_g[None, :])
    dk = jnp.exp((c - 1.0 - i)[:, None] * log_g[None, :])
    ds = jnp.exp(c * log_g)
    lanes = lambda a: jnp.repeat(a, HEAD_DIM, axis=-1)
    return dmat, lanes(dq), lanes(dk), lanes(ds[None, :])


def _ret_kernel(q_ref, k_ref, v_ref, s0_ref, dm_ref, dq_ref, dk_ref, ds_ref, gw_ref, gb_ref,
                y_ref, so_ref, s_scr):
    j = pl.program_id(1)

    @pl.when(j == 0)
    def _():
        s_scr[...] = s0_ref[0]

    q = q_ref[0]
    k = k_ref[0] * (HEAD_DIM ** -0.5)
    v = v_ref[0]
    s = s_scr[...]
    bd = _head_ones()
    qb = q.astype(BF16)
    kb = k.astype(BF16)
    vb = v.astype(BF16)
    y = _dot(qb, s.astype(BF16)) * dq_ref[...]
    for h in range(N_HEADS):
        hm = _head_mask(h, (1, BR_W))
        att = _dot_nt(jnp.where(hm, q, 0.0).astype(BF16), kb) * dm_ref[h]
        y = y + jnp.where(hm, _dot(att.astype(BF16), vb), 0.0)
    kd = (k * dk_ref[...]).T.astype(BF16)
    s_scr[...] = s * ds_ref[...] + bd.astype(F32) * _dot(kd, vb)
    mu = _headsum(y, bd) * (1.0 / HEAD_DIM)
    d = y - mu
    var = _headsum(d * d, bd) * (1.0 / HEAD_DIM)
    y_ref[0] = d * lax.rsqrt(var + RET_GN_EPS) * gw_ref[...] + gb_ref[...]

    @pl.when(j == pl.num_programs(1) - 1)
    def _():
        so_ref[0] = s_scr[...]


def _ret_state_embed(s):
    b = s.shape[0]
    eye = jnp.eye(N_HEADS, dtype=s.dtype)
    return jnp.einsum('bhde,hg->bhdge', s, eye).reshape(b, BR_W, BR_W)


def _ret_state_extract(s):
    b = s.shape[0]
    s4 = s.reshape(b, N_HEADS, HEAD_DIM, N_HEADS, HEAD_DIM)
    return jnp.stack([s4[:, h, :, h, :] for h in range(N_HEADS)], axis=1)


def _retention_prompt(q, k, v, s0, gn_w, gn_b, c):
    b, t, _ = q.shape
    dmat, dq, dk, ds = _ret_tables(c)
    seq = pl.BlockSpec((1, c, BR_W), lambda i, j: (i, j, 0))
    st = pl.BlockSpec((1, BR_W, BR_W), lambda i, j: (i, 0, 0))
    const = lambda a: pl.BlockSpec(a.shape, lambda i, j: (0,) * a.ndim)
    gw = gn_w.reshape(1, BR_W)
    gb = gn_b.reshape(1, BR_W)
    y, s = pl.pallas_call(
        _ret_kernel,
        grid=(b, t // c),
        in_specs=[seq, seq, seq, st, const(dmat), const(dq), const(dk), const(ds), const(gw), const(gb)],
        out_specs=(seq, st),
        out_shape=(jax.ShapeDtypeStruct((b, t, BR_W), F32), jax.ShapeDtypeStruct((b, BR_W, BR_W), F32)),
        scratch_shapes=[pltpu.VMEM((BR_W, BR_W), F32)],
        compiler_params=_cparams(("arbitrary", "arbitrary")),
        name="retention",
    )(q, k, v, _ret_state_embed(s0), dmat, dq, dk, ds, gw, gb)
    return y, _ret_state_extract(s)


def _ret_step_kernel(qc_ref, kc_ref, qr_ref, kr_ref, v_ref, s_ref, g_ref, gw_ref, gb_ref, y_ref, so_ref):
    s = s_ref[...]
    g = g_ref[...]
    v = v_ref[...]
    qk = jnp.sum(qr_ref[...] * kr_ref[...], axis=-1, keepdims=True)
    y = qk * v + jnp.sum(qc_ref[...] * s, axis=1, keepdims=True) * g
    so_ref[...] = s * g + kc_ref[...] * v
    mu = jnp.mean(y, axis=-1, keepdims=True)
    d = y - mu
    var = jnp.mean(d * d, axis=-1, keepdims=True)
    y_ref[...] = d * lax.rsqrt(var + RET_GN_EPS) * gw_ref[...] + gb_ref[...]


def _retention_step(q, k, v, s0, gn_w, gn_b):
    b = q.shape[0]
    n = b * N_HEADS
    ks = k * (HEAD_DIM ** -0.5)
    g = 1.0 - jnp.power(2.0, -5.0 - jnp.arange(N_HEADS, dtype=jnp.float32))
    g = jnp.tile(g, (b,)).reshape(n, 1, 1)
    gw = jnp.tile(gn_w.reshape(N_HEADS, HEAD_DIM), (b, 1)).reshape(n, 1, HEAD_DIM)
    gb = jnp.tile(gn_b.reshape(N_HEADS, HEAD_DIM), (b, 1)).reshape(n, 1, HEAD_DIM)
    col = lambda a: a.reshape(n, HEAD_DIM, 1)
    row = lambda a: a.reshape(n, 1, HEAD_DIM)
    args = (col(q), col(ks), row(q), row(ks), row(v), s0.reshape(n, HEAD_DIM, HEAD_DIM), g, gw, gb)
    full = lambda a: pl.BlockSpec(a.shape, lambda: (0,) * a.ndim)
    y, s = pl.pallas_call(
        _ret_step_kernel,
        in_specs=[full(a) for a in args],
        out_specs=(pl.BlockSpec((n, 1, HEAD_DIM), lambda: (0, 0, 0)),
                   pl.BlockSpec((n, HEAD_DIM, HEAD_DIM), lambda: (0, 0, 0))),
        out_shape=(jax.ShapeDtypeStruct((n, 1, HEAD_DIM), F32), jax.ShapeDtypeStruct((n, HEAD_DIM, HEAD_DIM), F32)),
        compiler_params=pltpu.CompilerParams(vmem_limit_bytes=VMEM_LIMIT_BYTES),
        name="retention_step",
    )(*args)
    return y.reshape(b, BR_W), s.reshape(b, N_HEADS, HEAD_DIM, HEAD_DIM)


INT_MIN = -2 ** 31


def _sort_key(s):
    s = jnp.where(s == 0.0, 0.0, s)
    bits = lax.bitcast_convert_type(s, I32)
    return jnp.where(bits < 0, bits ^ jnp.int32(0x7FFFFFFF), bits)


def _kth_largest_key(count_ge, rows, k):
    def bit_step(i, tb):
        cand = tb + jnp.left_shift(jnp.int32(1), 31 - i)
        return jnp.where(count_ge(cand) >= k, cand, tb)

    return lax.fori_loop(0, 32, bit_step, jnp.full((rows, 128), INT_MIN, I32))


def _flash_update(qf, kb, vb, msk, stats, acc):
    new_stats = []
    alpha_l = jnp.zeros_like(acc)
    pv = jnp.zeros_like(acc)
    for h in range(N_HEADS):
        hm = _head_mask(h, (1, BR_W))
        m_old, l_old = stats[h]
        mk = msk[h] if isinstance(msk, (list, tuple)) else msk
        s = _dot_nt(jnp.where(hm, qf, 0.0).astype(BF16), kb) * (HEAD_DIM ** -0.5)
        s = jnp.where(mk, s, NEG)
        m_new = jnp.maximum(m_old, jnp.max(s, axis=-1, keepdims=True))
        alpha = jnp.exp(m_old - m_new)
        p = jnp.where(mk, jnp.exp(s - m_new), 0.0)
        l_new = alpha * l_old + jnp.sum(p, axis=-1, keepdims=True)
        alpha_l = jnp.where(hm, alpha, alpha_l)
        pv = jnp.where(hm, _dot(p.astype(BF16), vb), pv)
        new_stats.append((m_new, l_new))
    return tuple(new_stats), acc * alpha_l + pv


def _flash_init(tq):
    stats = tuple((jnp.full((tq, 1), NEG, F32), jnp.zeros((tq, 1), F32)) for _ in range(N_HEADS))
    return stats, jnp.zeros((tq, BR_W), F32)


def _flash_finish(stats, acc):
    l_l = jnp.zeros_like(acc)
    for h in range(N_HEADS):
        l_l = jnp.where(_head_mask(h, (1, BR_W)), stats[h][1], l_l)
    return acc / l_l


def _dsa_prompt_kernel(q_ref, qi_ref, wi_ref, k_ref, v_ref, ki_ref, o_ref, key_scr, *, tq, kb, topk):
    qt = pl.program_id(1)
    nkc = (qt + 1) * (tq // kb)
    rowpos = qt * tq + lax.broadcasted_iota(I32, (tq, kb), 0)
    coli = lax.broadcasted_iota(I32, (tq, kb), 1)
    qi = qi_ref[0]
    wi = wi_ref[0] * (IDX_HEADS ** -0.5)

    def score_chunk(c, carry):
        kic = ki_ref[0, pl.ds(pl.multiple_of(c * kb, kb), kb), :]
        k_hi, k_lo = _split(kic)
        s = jnp.zeros((tq, kb), F32)
        for h in range(IDX_HEADS):
            q_hi, q_lo = _split(jnp.where(_head_mask(h, (1, BR_W)), qi, 0.0))
            d = _dot_nt(q_hi, k_hi) + (_dot_nt(q_lo, k_hi) + _dot_nt(q_hi, k_lo))
            s = s + jnp.maximum(d * (IDX_DIM ** -0.5), 0.0) * wi[:, h:h + 1]
        s = jnp.where(c * kb + coli <= rowpos, s, -jnp.inf)
        key_scr[c] = _sort_key(s)
        return carry

    lax.fori_loop(0, nkc, score_chunk, 0)

    def counter(cmp):
        def count(cand):
            def body(c, acc):
                kc = key_scr[c]
                for j in range(kb // 128):
                    acc = acc + jnp.where(cmp(kc[:, j * 128:(j + 1) * 128], cand), 1.0, 0.0)
                return acc
            acc = lax.fori_loop(0, nkc, body, jnp.zeros((tq, 128), F32))
            return jnp.sum(acc, axis=-1, keepdims=True)
        return count

    tb = _kth_largest_key(counter(lambda a, b: a >= b), tq, float(topk))
    rem = float(topk) - counter(lambda a, b: a > b)(tb)
    thr = tb[:, 0:1]
    upper = (lax.broadcasted_iota(I32, (kb, kb), 0) <= lax.broadcasted_iota(I32, (kb, kb), 1)).astype(BF16)
    qf = q_ref[0]

    def attn_chunk(c, carry):
        stats, acc, run = carry
        kc = key_scr[c]
        eq = kc == thr
        eqf = jnp.where(eq, 1.0, 0.0)
        pre = _dot(eqf.astype(BF16), upper) + run
        sel = (kc > thr) | (eq & (pre <= rem))
        msk = sel & (c * kb + coli <= rowpos)
        off = pl.multiple_of(c * kb, kb)
        kblk = k_ref[0, pl.ds(off, kb), :].astype(BF16)
        vblk = v_ref[0, pl.ds(off, kb), :].astype(BF16)
        stats, acc = _flash_update(qf, kblk, vblk, msk, stats, acc)
        return stats, acc, run + jnp.sum(eqf, axis=-1, keepdims=True)

    stats, acc = _flash_init(tq)
    stats, acc, _ = lax.fori_loop(0, nkc, attn_chunk, (stats, acc, jnp.zeros((tq, 1), F32)))
    o_ref[0] = _flash_finish(stats, acc)


def _dsa_prompt(q, qi, wi, k, v, ki4, tq=256, kb=256):
    b, t, _ = q.shape
    topk = min(DSA_TOPK_MAX, t // 4)
    tile = lambda w: pl.BlockSpec((1, tq, w), lambda i, j: (i, j, 0))
    full = pl.BlockSpec((1, t, BR_W), lambda i, j: (i, 0, 0))
    return pl.pallas_call(
        functools.partial(_dsa_prompt_kernel, tq=tq, kb=kb, topk=topk),
        grid=(b, t // tq),
        in_specs=[tile(BR_W), tile(BR_W), tile(128), full, full, full],
        out_specs=tile(BR_W),
        out_shape=jax.ShapeDtypeStruct((b, t, BR_W), F32),
        scratch_shapes=[pltpu.VMEM((t // kb, tq, kb), I32)],
        compiler_params=_cparams(("arbitrary", "arbitrary")),
        name="dsa_prompt",
    )(q, qi, wi, k, v, ki4)


def _top_blocks(g, col, limit, nsel):
    g = jnp.where(col < limit, g, -jnp.inf)
    selm = jnp.zeros(g.shape, jnp.bool_)
    for _ in range(nsel):
        mx = jnp.max(g, axis=-1, keepdims=True)
        idx = jnp.min(jnp.where(g == mx, col, jnp.int32(1 << 20)), axis=-1, keepdims=True)
        pick = col == idx
        selm = selm | (pick & (col < limit))
        g = jnp.where(pick, -jnp.inf, g)
    return selm


def _moba_prompt_kernel(q_ref, k_ref, v_ref, o_ref, kbar_scr, *, nb, nsel):
    qt = pl.program_id(1)
    tq = MOBA_BLOCK

    @pl.when(qt == 0)
    def _():
        kbar_scr[...] = jnp.zeros_like(kbar_scr)

        def blk(n, carry):
            kblk = k_ref[0, pl.ds(pl.multiple_of(n * tq, tq), tq), :]
            kbar_scr[pl.ds(n, 1), :] = jnp.sum(kblk, axis=0, keepdims=True) * (1.0 / MOBA_BLOCK)
            return carry

        lax.fori_loop(0, nb, blk, 0)

    qf = q_ref[0]
    kbar = kbar_scr[...]
    col = lax.broadcasted_iota(I32, (tq, 128), 1)
    sel = []
    for h in range(N_HEADS):
        g = _mm3_nt(jnp.where(_head_mask(h, (1, BR_W)), qf, 0.0), kbar)
        sel.append(jnp.where(_top_blocks(g, col, qt, nsel), 1.0, 0.0))

    tri = lax.broadcasted_iota(I32, (tq, tq), 1) <= lax.broadcasted_iota(I32, (tq, tq), 0)
    off = pl.multiple_of(qt * tq, tq)
    stats, acc = _flash_init(tq)
    stats, acc = _flash_update(qf, k_ref[0, pl.ds(off, tq), :].astype(BF16),
                               v_ref[0, pl.ds(off, tq), :].astype(BF16), tri, stats, acc)

    def past(n, carry):
        stats, acc = carry
        o = pl.multiple_of(n * tq, tq)
        allow = [jnp.sum(jnp.where(col == n, sel[h], 0.0), axis=-1, keepdims=True) > 0.0
                 for h in range(N_HEADS)]
        return _flash_update(qf, k_ref[0, pl.ds(o, tq), :].astype(BF16),
                             v_ref[0, pl.ds(o, tq), :].astype(BF16), allow, stats, acc)

    stats, acc = lax.fori_loop(0, qt, past, (stats, acc))
    o_ref[0] = _flash_finish(stats, acc)


def _moba_prompt(q, k, v):
    b, t, _ = q.shape
    nb = t // MOBA_BLOCK
    nsel = min(MOBA_TOPK, nb - 1)
    tile = pl.BlockSpec((1, MOBA_BLOCK, BR_W), lambda i, j: (i, j, 0))
    full = pl.BlockSpec((1, t, BR_W), lambda i, j: (i, 0, 0))
    return pl.pallas_call(
        functools.partial(_moba_prompt_kernel, nb=nb, nsel=nsel),
        grid=(b, nb),
        in_specs=[tile, full, full],
        out_specs=tile,
        out_shape=jax.ShapeDtypeStruct((b, t, BR_W), F32),
        scratch_shapes=[pltpu.VMEM((128, BR_W), F32)],
        compiler_params=_cparams(("arbitrary", "arbitrary")),
        name="moba_prompt",
    )(q, k, v)


def _layer_out_kernel(x_ref, mod_ref, gpre_ref, gpost_ref, ya_ref, bo_ref, yb_ref, yc_ref, yd_ref, gt_ref,
                      lnw_ref, lnb_ref, wb_ref, wm_ref, wo_ref, o_ref):
    x = x_ref[0]
    y = x * lax.rsqrt(jnp.mean(x * x, axis=-1, keepdims=True) + RMS_EPS) * gpre_ref[...]
    shift = mod_ref[0, :, 0:D_MODEL]
    scale = mod_ref[0, :, D_MODEL:2 * D_MODEL]
    gate = mod_ref[0, :, 2 * D_MODEL:3 * D_MODEL]
    hb = (y * (1.0 + scale) + shift).astype(BF16)
    bd = _head_ones()
    ya = ya_ref[0]
    mu = _headsum(ya, bd) * (1.0 / HEAD_DIM)
    d = ya - mu
    var = _headsum(d * d, bd) * (1.0 / HEAD_DIM)
    ya = d * lax.rsqrt(var + RWKV_GN_EPS) * lnw_ref[...] + lnb_ref[...] + bo_ref[0]
    outs = (ya, yb_ref[0], yc_ref[0], yd_ref[0])
    merged = jnp.zeros(x.shape, F32)
    for n in range(4):
        o = outs[n] * jax.nn.silu(gt_ref[0, :, n * BR_W:(n + 1) * BR_W])
        merged = merged + jax.nn.sigmoid(_dot(hb, wm_ref[n])) * _dot(o.astype(BF16), wb_ref[n])
    z = _dot(merged.astype(BF16), wo_ref[...])
    z = z * lax.rsqrt(jnp.mean(z * z, axis=-1, keepdims=True) + RMS_EPS) * gpost_ref[...]
    o_ref[0] = x + gate * z


def _layer_out(x, mod, p, ya, bonus, yb, yc, yd, gates, tm):
    b, t, _ = x.shape
    r = mod.shape[1]
    if r == 1:
        mod_spec = pl.BlockSpec((1, 1, 3 * D_MODEL), lambda i, j: (i, 0, 0))
    else:
        mod_spec = pl.BlockSpec((1, tm, 3 * D_MODEL), lambda i, j: (i, j, 0))
    tile = lambda w: pl.BlockSpec((1, tm, w), lambda i, j: (i, j, 0))
    row = lambda v: v.reshape(1, -1)
    const = lambda shape: pl.BlockSpec(shape, lambda i, j: (0,) * len(shape), pipeline_mode=pl.Buffered(1))
    vec = lambda n: pl.BlockSpec((1, n), lambda i, j: (0, 0))
    return pl.pallas_call(
        _layer_out_kernel,
        grid=(b, t // tm),
        in_specs=[tile(D_MODEL), mod_spec, vec(D_MODEL), vec(D_MODEL),
                  tile(BR_W), tile(BR_W), tile(BR_W), tile(BR_W), tile(BR_W), tile(D_MODEL),
                  vec(BR_W), vec(BR_W),
                  const((4, BR_W, D_MODEL)), const((4, D_MODEL, D_MODEL)), const((D_MODEL, D_MODEL))],
        out_specs=tile(D_MODEL),
        out_shape=jax.ShapeDtypeStruct((b, t, D_MODEL), F32),
        compiler_params=_cparams(("arbitrary", "arbitrary")),
        name="layer_out",
    )(x, mod, row(p['g_pre']), row(p['g_post']), ya, bonus, yb, yc, yd, gates,
      row(p['a_ln_w']), row(p['a_ln_b']),
      p['w_branch'].astype(BF16), p['w_merge'].astype(BF16), p['w_out'].astype(BF16))


PAGES_PER_STEP = 8


def _page_specs(block, g_count, width):
    def spec(g):
        return pl.BlockSpec(block, lambda i, j, pt: (pt[i, j * g_count + g], 0, 0))
    del width
    return [spec(g) for g in range(g_count)]


def _head_rows(x):
    hr = lax.broadcasted_iota(I32, (N_HEADS, BR_W), 0)
    hl = lax.broadcasted_iota(I32, (N_HEADS, BR_W), 1) // HEAD_DIM
    return jnp.where(hr == hl, x, 0.0), hr == hl


def _dsa_scores_kernel(pt_ref, qm_ref, wi_ref, kn_ref, *refs, g_count):
    page_refs = refs[:g_count]
    o_ref, on_ref = refs[g_count], refs[g_count + 1]
    j = pl.program_id(1)
    qm = qm_ref[0]
    q_hi, q_lo = _split(qm)
    w = wi_ref[0] * (IDX_HEADS ** -0.5)

    def combine(d):
        return jnp.sum(jnp.maximum(d * (IDX_DIM ** -0.5), 0.0) * w, axis=0, keepdims=True)

    for g in range(g_count):
        k_hi, k_lo = _split(page_refs[g][0])
        d = _dot_nt(q_hi, k_hi) + (_dot_nt(q_lo, k_hi) + _dot_nt(q_hi, k_lo))
        o_ref[0, g] = combine(d)

    @pl.when(j == 0)
    def _():
        dn = jnp.sum(qm * kn_ref[0], axis=-1, keepdims=True)
        on_ref[0] = jnp.broadcast_to(combine(dn), (1, 128))


def _dsa_scores(qi, wi, ki_new, pool, page_table):
    b, n_pages = page_table.shape
    g_count = min(PAGES_PER_STEP, n_pages)
    qm = qi.reshape(b, IDX_HEADS, IDX_DIM)
    wcol = wi.reshape(b, IDX_HEADS, 1)
    kn = ki_new.reshape(b, 1, IDX_DIM)
    grid_spec = pltpu.PrefetchScalarGridSpec(
        num_scalar_prefetch=1,
        grid=(b, n_pages // g_count),
        in_specs=[pl.BlockSpec((1, IDX_HEADS, IDX_DIM), lambda i, j, pt: (i, 0, 0)),
                  pl.BlockSpec((1, IDX_HEADS, 1), lambda i, j, pt: (i, 0, 0)),
                  pl.BlockSpec((1, 1, IDX_DIM), lambda i, j, pt: (i, 0, 0))]
                 + _page_specs((1, PAGE_SIZE, IDX_DIM), g_count, IDX_DIM),
        out_specs=(pl.BlockSpec((1, g_count, 1, PAGE_SIZE), lambda i, j, pt: (i, j, 0, 0)),
                   pl.BlockSpec((1, 1, 128), lambda i, j, pt: (i, 0, 0))))
    sc, sc_new = pl.pallas_call(
        functools.partial(_dsa_scores_kernel, g_count=g_count),
        grid_spec=grid_spec,
        out_shape=(jax.ShapeDtypeStruct((b, n_pages, 1, PAGE_SIZE), F32),
                   jax.ShapeDtypeStruct((b, 1, 128), F32)),
        compiler_params=_cparams(("arbitrary", "arbitrary")),
        name="dsa_scores",
    )(page_table, qm, wcol, kn, *([pool] * g_count))
    return sc.reshape(b, n_pages * PAGE_SIZE), sc_new[:, 0, 0:1]


def _topk_rows_kernel(s_ref, o_ref, key_scr, *, n_valid, topk):
    nblk, rows, _ = s_ref.shape
    col = lax.broadcasted_iota(I32, (rows, 128), 1)

    def to_key(j, carry):
        s = jnp.where(j * 128 + col < n_valid, s_ref[j], -jnp.inf)
        key_scr[j] = _sort_key(s)
        return carry

    lax.fori_loop(0, nblk, to_key, 0)

    def counter(cmp):
        def count(cand):
            body = lambda j, acc: acc + jnp.where(cmp(key_scr[j], cand), 1.0, 0.0)
            acc = lax.fori_loop(0, nblk, body, jnp.zeros((rows, 128), F32))
            return jnp.sum(acc, axis=-1, keepdims=True)
        return count

    tb = _kth_largest_key(counter(lambda a, b: a >= b), rows, float(topk))
    rem = float(topk) - counter(lambda a, b: a > b)(tb)
    upper = (lax.broadcasted_iota(I32, (128, 128), 0) <= lax.broadcasted_iota(I32, (128, 128), 1)).astype(BF16)

    def select(j, run):
        kc = key_scr[j]
        eq = kc == tb
        eqf = jnp.where(eq, 1.0, 0.0)
        pre = _dot(eqf.astype(BF16), upper) + run
        sel = ((kc > tb) | (eq & (pre <= rem))) & (j * 128 + col < n_valid)
        o_ref[j] = jnp.where(sel, 1.0, 0.0)
        return run + jnp.sum(eqf, axis=-1, keepdims=True)

    lax.fori_loop(0, nblk, select, jnp.zeros((rows, 1), F32))


def _topk_rows(scores, topk):
    rows, n = scores.shape
    nblk = -(-n // 128)
    s = jnp.pad(scores, ((0, 0), (0, nblk * 128 - n)))
    s = jnp.transpose(s.reshape(rows, nblk, 128), (1, 0, 2))
    m = pl.pallas_call(
        functools.partial(_topk_rows_kernel, n_valid=n, topk=topk),
        in_specs=[pl.BlockSpec((nblk, rows, 128), lambda: (0, 0, 0))],
        out_specs=pl.BlockSpec((nblk, rows, 128), lambda: (0, 0, 0)),
        out_shape=jax.ShapeDtypeStruct((nblk, rows, 128), F32),
        scratch_shapes=[pltpu.VMEM((nblk, rows, 128), I32)],
        compiler_params=pltpu.CompilerParams(vmem_limit_bytes=VMEM_LIMIT_BYTES),
        name="topk_rows",
    )(s)
    return jnp.transpose(m, (1, 0, 2)).reshape(rows, nblk * 128)[:, :n]


def _dsa_attn_kernel(pt_ref, q_ref, kn_ref, vn_ref, mn_ref, msk_ref, *refs, g_count):
    k_refs = refs[:g_count]
    v_refs = refs[g_count:2 * g_count]
    o_ref = refs[2 * g_count]
    m_scr, l_scr, acc_scr = refs[2 * g_count + 1:]
    j = pl.program_id(1)
    qbd, hsel = _head_rows(q_ref[0])

    @pl.when(j == 0)
    def _():
        sn = jnp.sum(qbd * kn_ref[0], axis=-1, keepdims=True) * (HEAD_DIM ** -0.5)
        ok = mn_ref[0][:, 0:1] > 0.0
        m_scr[...] = jnp.broadcast_to(jnp.where(ok, sn, NEG), (N_HEADS, 128))
        l_scr[...] = jnp.broadcast_to(jnp.where(ok, 1.0, 0.0), (N_HEADS, 128))
        acc_scr[...] = jnp.where(ok, jnp.broadcast_to(vn_ref[0], (N_HEADS, BR_W)), 0.0)

    kcat = jnp.concatenate([r[0] for r in k_refs], axis=0).astype(BF16)
    vcat = jnp.concatenate([r[0] for r in v_refs], axis=0).astype(BF16)
    mk = msk_ref[0] > 0.0
    s = jnp.where(mk, _dot_nt(qbd.astype(BF16), kcat) * (HEAD_DIM ** -0.5), NEG)
    m_old = m_scr[:, 0:1]
    m_new = jnp.maximum(m_old, jnp.max(s, axis=-1, keepdims=True))
    alpha = jnp.exp(m_old - m_new)
    p = jnp.where(mk, jnp.exp(s - m_new), 0.0)
    l_new = alpha * l_scr[:, 0:1] + jnp.sum(p, axis=-1, keepdims=True)
    acc = alpha * acc_scr[...] + _dot(p.astype(BF16), vcat)
    m_scr[...] = jnp.broadcast_to(m_new, (N_HEADS, 128))
    l_scr[...] = jnp.broadcast_to(l_new, (N_HEADS, 128))
    acc_scr[...] = acc

    @pl.when(j == pl.num_programs(1) - 1)
    def _():
        o_ref[0] = jnp.sum(jnp.where(hsel, acc / l_new, 0.0), axis=0, keepdims=True)


def _dsa_attn(q, k_new, v_new, mask, k_pool, v_pool, page_table):
    b, n_pages = page_table.shape
    g_count = min(PAGES_PER_STEP, n_pages)
    past = n_pages * PAGE_SIZE
    n_pool = k_pool.shape[0]
    kp = k_pool.reshape(n_pool, PAGE_SIZE, BR_W)
    vp = v_pool.reshape(n_pool, PAGE_SIZE, BR_W)
    row = lambda a: a.reshape(b, 1, BR_W)
    m_new = jnp.broadcast_to(mask[:, past:past + 1], (b, 128)).reshape(b, 1, 128)
    m_past = mask[:, :past].reshape(b, 1, past)
    rspec = pl.BlockSpec((1, 1, BR_W), lambda i, j, pt: (i, 0, 0))
    grid_spec = pltpu.PrefetchScalarGridSpec(
        num_scalar_prefetch=1,
        grid=(b, n_pages // g_count),
        in_specs=[rspec, rspec, rspec,
                  pl.BlockSpec((1, 1, 128), lambda i, j, pt: (i, 0, 0)),
                  pl.BlockSpec((1, 1, g_count * PAGE_SIZE), lambda i, j, pt: (i, 0, j))]
                 + _page_specs((1, PAGE_SIZE, BR_W), g_count, BR_W) * 2,
        out_specs=rspec,
        scratch_shapes=[pltpu.VMEM((N_HEADS, 128), F32), pltpu.VMEM((N_HEADS, 128), F32),
                        pltpu.VMEM((N_HEADS, BR_W), F32)])
    out = pl.pallas_call(
        functools.partial(_dsa_attn_kernel, g_count=g_count),
        grid_spec=grid_spec,
        out_shape=jax.ShapeDtypeStruct((b, 1, BR_W), F32),
        compiler_params=_cparams(("arbitrary", "arbitrary")),
        name="dsa_attn",
    )(page_table, row(q), row(k_new), row(v_new), m_new, m_past, *([kp] * g_count), *([vp] * g_count))
    return out.reshape(b, BR_W)


def _kbar_kernel(pt_ref, *refs, g_count):
    page_refs = refs[:g_count]
    o_ref = refs[g_count]
    for g2 in range(g_count // 2):
        s = (jnp.sum(page_refs[2 * g2][0], axis=0, keepdims=True)
             + jnp.sum(page_refs[2 * g2 + 1][0], axis=0, keepdims=True))
        o_ref[0, g2] = s * (1.0 / MOBA_BLOCK)


def _moba_kbar(k_pool, page_table):
    b, n_pages = page_table.shape
    g_count = min(PAGES_PER_STEP, n_pages)
    n_pool = k_pool.shape[0]
    kp = k_pool.reshape(n_pool, PAGE_SIZE, BR_W)
    grid_spec = pltpu.PrefetchScalarGridSpec(
        num_scalar_prefetch=1,
        grid=(b, n_pages // g_count),
        in_specs=_page_specs((1, PAGE_SIZE, BR_W), g_count, BR_W),
        out_specs=pl.BlockSpec((1, g_count // 2, 1, BR_W), lambda i, j, pt: (i, j, 0, 0)))
    out = pl.pallas_call(
        functools.partial(_kbar_kernel, g_count=g_count),
        grid_spec=grid_spec,
        out_shape=jax.ShapeDtypeStruct((b, n_pages // 2, 1, BR_W), F32),
        compiler_params=_cparams(("arbitrary", "arbitrary")),
        name="moba_kbar",
    )(page_table, *([kp] * g_count))
    return out.reshape(b, n_pages // 2, BR_W)


def _moba_gate_kernel(q_ref, kbar_ref, o_ref, *, n_past, nsel):
    qbd, _ = _head_rows(q_ref[0])
    g = _mm3_nt(qbd, kbar_ref[0])
    col = lax.broadcasted_iota(I32, (N_HEADS, 128), 1)
    g = jnp.where(col < n_past, g, -jnp.inf)
    out = jnp.full((N_HEADS, 128), -1, I32)
    for i in range(nsel):
        mx = jnp.max(g, axis=-1, keepdims=True)
        idx = jnp.min(jnp.where(g == mx, col, jnp.int32(1 << 20)), axis=-1, keepdims=True)
        ok = idx < n_past
        out = jnp.where(col == i, jnp.where(ok, idx, -1), out)
        g = jnp.where(col == idx, -jnp.inf, g)
    o_ref[0] = out


def _moba_gate(q, kbar, nsel):
    b, n_past, _ = kbar.shape
    kb = jnp.pad(kbar, ((0, 0), (0, 128 - n_past), (0, 0)))
    out = pl.pallas_call(
        functools.partial(_moba_gate_kernel, n_past=n_past, nsel=nsel),
        grid=(b,),
        in_specs=[pl.BlockSpec((1, 1, BR_W), lambda i: (i, 0, 0)),
                  pl.BlockSpec((1, 128, BR_W), lambda i: (i, 0, 0))],
        out_specs=pl.BlockSpec((1, N_HEADS, 128), lambda i: (i, 0, 0)),
        out_shape=jax.ShapeDtypeStruct((b, N_HEADS, 128), I32),
        compiler_params=_cparams(("arbitrary",)),
        name="moba_gate",
    )(q.reshape(b, 1, BR_W), kb)
    return out[:, :, :nsel]


def _moba_attn_kernel(sel_ref, pt_ref, q_ref, kn_ref, vn_ref, *refs, nsel):
    n_pg = 2 * nsel
    k_refs = refs[:n_pg]
    v_refs = refs[n_pg:2 * n_pg]
    o_ref = refs[2 * n_pg]
    bi = pl.program_id(0)
    h = pl.program_id(1)
    hm = lax.broadcasted_iota(I32, (1, BR_W), 1) // HEAD_DIM == h
    qh = jnp.where(hm, q_ref[0], 0.0)
    sn = jnp.sum(qh * kn_ref[0], axis=-1, keepdims=True) * (HEAD_DIM ** -0.5)
    kcat = jnp.concatenate([r[0] for r in k_refs], axis=0).astype(BF16)
    vcat = jnp.concatenate([r[0] for r in v_refs], axis=0).astype(BF16)
    s = _dot_nt(qh.astype(BF16), kcat) * (HEAD_DIM ** -0.5)
    blk = lax.broadcasted_iota(I32, s.shape, 1) // MOBA_BLOCK
    mk = jnp.zeros(s.shape, jnp.bool_)
    for i in range(nsel):
        mk = mk | (blk == jnp.where(sel_ref[(bi * N_HEADS + h) * nsel + i] >= 0, i, -1))
    s = jnp.where(mk, s, NEG)
    m = jnp.maximum(jnp.max(s, axis=-1, keepdims=True), sn)
    p = jnp.where(mk, jnp.exp(s - m), 0.0)
    pn = jnp.exp(sn - m)
    out = (_dot(p.astype(BF16), vcat) + pn * vn_ref[0]) / (jnp.sum(p, axis=-1, keepdims=True) + pn)

    @pl.when(h == 0)
    def _():
        o_ref[0] = jnp.zeros((1, BR_W), F32)

    o_ref[0] = o_ref[0] + jnp.where(hm, out, 0.0)


def _moba_attn(q, k_new, v_new, sel, k_pool, v_pool, page_table):
    b, n_pages = page_table.shape
    nsel = sel.shape[-1]
    n_pool = k_pool.shape[0]
    kp = k_pool.reshape(n_pool, PAGE_SIZE, BR_W)
    vp = v_pool.reshape(n_pool, PAGE_SIZE, BR_W)
    row = lambda a: a.reshape(b, 1, BR_W)

    def pspec(i, half):
        def imap(bi, h, sel_ref, pt):
            blk = jnp.maximum(sel_ref[(bi * N_HEADS + h) * nsel + i], 0)
            return (pt[bi, 2 * blk + half], 0, 0)
        return pl.BlockSpec((1, PAGE_SIZE, BR_W), imap)

    pages = [pspec(i, half) for i in range(nsel) for half in range(2)]
    rspec = pl.BlockSpec((1, 1, BR_W), lambda bi, h, s, pt: (bi, 0, 0))
    grid_spec = pltpu.PrefetchScalarGridSpec(
        num_scalar_prefetch=2,
        grid=(b, N_HEADS),
        in_specs=[rspec, rspec, rspec] + pages + pages,
        out_specs=rspec)
    out = pl.pallas_call(
        functools.partial(_moba_attn_kernel, nsel=nsel),
        grid_spec=grid_spec,
        out_shape=jax.ShapeDtypeStruct((b, 1, BR_W), F32),
        compiler_params=_cparams(("arbitrary", "arbitrary")),
        name="moba_attn",
    )(sel.reshape(-1), page_table, row(q), row(k_new), row(v_new),
      *([kp] * (2 * nsel)), *([vp] * (2 * nsel)))
    return out.reshape(b, BR_W)


def _sample_layer(x, c, p, a_shift, a_wkv, b_ret, ck_pool, cv_pool, cki_pool, dk_pool, dv_pool, page_table):
    b = x.shape[0]
    n_pages = page_table.shape[1]
    past = n_pages * PAGE_SIZE
    assert x.shape[1] == 1 and past % MOBA_BLOCK == 0
    pos = jnp.full((b,), past, I32)
    tab_ret = _rope_tables(pos, HEAD_DIM, RET_THETA, BR_W)
    tab_std = _rope_tables(pos, ROPE_DIMS, ROPE_THETA, BR_W)
    w_hi, w_lo = _regroup_w_in(p['w_in'])
    mod = _ada(c, p['w_ada'], p['b_ada']).reshape(1, b, 3 * D_MODEL)
    xr = x.reshape(1, b, D_MODEL)
    outs = _layer_in(xr, mod, p['g_pre'], w_hi, w_lo, tab_ret, tab_std, b)
    (ua, gates, qb, kb, vb, qc, kc, vc, qi, ki4, wi, qd, kd, vd) = [o[0] for o in outs]
    r, w, k2, v, kk, bb, bonus = _rwkv_pre_rows(ua, a_shift, p)
    ya, wkv = _rwkv_step(r, w, k2, v, kk, bb, a_wkv)
    yb, ret = _retention_step(qb, kb, vb, b_ret, p['b_gn_w'], p['b_gn_b'])
    ki = ki4[:, :IDX_DIM]
    sc_past, sc_new = _dsa_scores(qi, wi[:, :IDX_HEADS], ki, cki_pool, page_table)
    total = past + 1
    mask = _topk_rows(jnp.concatenate([sc_past, sc_new], axis=1), min(DSA_TOPK_MAX, total // 4))
    yc = _dsa_attn(qc, kc, vc, mask, ck_pool, cv_pool, page_table)
    n_past_blocks = past // MOBA_BLOCK
    nsel = min(MOBA_TOPK, n_past_blocks)
    if nsel > 0:
        sel = _moba_gate(qd, _moba_kbar(dk_pool, page_table), nsel)
        yd = _moba_attn(qd, kd, vd, sel, dk_pool, dv_pool, page_table)
    else:
        yd = vd
    row = lambda a: a.reshape(1, b, -1)
    x_new = _layer_out(xr, mod, p, row(ya), row(bonus), row(yb), row(yc), row(yd), gates.reshape(1, b, -1), b)
    heads = lambda a: a.reshape(b, 1, N_HEADS, HEAD_DIM)
    new_state = (ua, wkv, ret, heads(kc), heads(vc), ki.reshape(b, 1, IDX_DIM), heads(kd), heads(vd))
    return x_new.reshape(b, 1, D_MODEL), new_state


def _prompt_layer(x, c, p):
    b, t, _ = x.shape
    pos = jnp.arange(t)
    tab_ret = _rope_tables(pos, HEAD_DIM, RET_THETA, BR_W)
    tab_std = _rope_tables(pos, ROPE_DIMS, ROPE_THETA, BR_W)
    w_hi, w_lo = _regroup_w_in(p['w_in'])
    mod = _ada(c, p['w_ada'], p['b_ada']).reshape(b, 1, 3 * D_MODEL)
    tm = min(256, t)
    (ua, gates, qb, kb, vb, qc, kc, vc, qi, ki4, wi, qd, kd, vd) = _layer_in(
        x, mod, p['g_pre'], w_hi, w_lo, tab_ret, tab_std, tm)
    r, w, k2, v, kk, bb, bonus = _rwkv_pre_prompt(ua, jnp.zeros((b, A_SHIFT_W), F32), p, tm)
    ya, wkv = _rwkv_scan(r, w, k2, v, kk, bb, jnp.zeros((b, HEAD_DIM, BR_W), F32), min(64, t))
    yb, ret = _retention_prompt(qb, kb, vb, jnp.zeros((b, N_HEADS, HEAD_DIM, HEAD_DIM), F32),
                                p['b_gn_w'], p['b_gn_b'], min(256, t))
    yc = _dsa_prompt(qc, qi, wi, kc, vc, ki4)
    yd = _moba_prompt(qd, kd, vd)
    x_new = _layer_out(x, mod, p, ya, bonus, yb, yc, yd, gates, tm)
    heads = lambda a: a.reshape(b, t, N_HEADS, HEAD_DIM)
    new_state = (ua[:, -1], _wkv_from_scan_layout(wkv), ret, heads(kc), heads(vc), ki4[..., :IDX_DIM],
                 heads(kd), heads(vd))
    return x_new, new_state


_PARAM_NAMES = ('w_ada', 'b_ada', 'g_pre', 'g_post', 'w_in', 'a_mu', 'a_w0', 'a_w2', 'a_a0', 'a_a2', 'a_kk', 'a_ka',
                'a_rk', 'a_ln_w', 'a_ln_b', 'b_gn_w', 'b_gn_b', 'w_branch', 'w_merge', 'w_out')


def kernel(x_prompt, x_sample, c_prompt, c_sample, state_a_shift, state_a_wkv, state_b_ret, cache_c_k, cache_c_v,
           cache_c_kidx, cache_d_k, cache_d_v, page_table, w_ada, b_ada, g_pre, g_post, w_in, a_mu, a_w0, a_w2,
           a_a0, a_a2, a_kk, a_ka, a_rk, a_ln_w, a_ln_b, b_gn_w, b_gn_b, w_branch, w_merge, w_out):
    stacked = dict(zip(_PARAM_NAMES, (w_ada, b_ada, g_pre, g_post, w_in, a_mu, a_w0, a_w2, a_a0, a_a2, a_kk, a_ka,
                                      a_rk, a_ln_w, a_ln_b, b_gn_w, b_gn_b, w_branch, w_merge, w_out)))
    depth = w_in.shape[0]
    n_pool = cache_c_k.shape[1]
    fold = lambda a: a.reshape((depth * n_pool,) + a.shape[2:])
    ck, cv, cki, dk, dv = (fold(a) for a in (cache_c_k, cache_c_v, cache_c_kidx, cache_d_k, cache_d_v))
    xp, xs = x_prompt, x_sample
    p_new, s_new = [], []
    for l in range(depth):
        p = {name: val[l] for name, val in stacked.items()}
        xp, st_p = _prompt_layer(xp, c_prompt, p)
        xs, st_s = _sample_layer(xs, c_sample, p, state_a_shift[l], state_a_wkv[l], state_b_ret[l],
                                 ck, cv, cki, dk, dv, page_table + l * n_pool)
        p_new.append(st_p)
        s_new.append(st_s)
    stack = lambda states, i: jnp.stack([s[i] for s in states])
    return ((xp, xs) + tuple(stack(p_new, i) for i in range(8)) + tuple(stack(s_new, i) for i in range(8)))
```

```python
import functools

import jax
import jax.numpy as jnp
import numpy as np
from jax import lax
from jax.experimental import pallas as pl
from jax.experimental.pallas import tpu as pltpu

F32 = jnp.float32
BF16 = jnp.bfloat16
I32 = jnp.int32

D_MODEL = 1024
PAGE_SIZE = 128
BR_W = 256
HEAD_DIM = 64
N_HEADS = 4
LORA_W = 64
LORA_A = 64
A_SHIFT_W = 3 * BR_W + LORA_W + LORA_A
ROPE_THETA = 500000.0
ROPE_DIMS = HEAD_DIM // 4
RET_THETA = 10000.0
IDX_HEADS = 4
IDX_DIM = 64
DSA_TOPK_MAX = 256
MOBA_BLOCK = 256
MOBA_TOPK = 3
RMS_EPS = 1e-6
RWKV_GN_EPS = 64e-5
RET_GN_EPS = 1e-5
NEG = -1e30
VMEM_LIMIT_BYTES = 56 * 1024 * 1024
RWKV_GROUP = 8

_C = {}
_off = 0
for _name, _n in (('a_r', 256), ('a_k', 256), ('a_v', 256), ('a_wl', 64), ('a_al', 64), ('a_g', 256),
                  ('b_q', 256), ('b_k', 256), ('b_v', 256), ('b_g', 256),
                  ('c_q', 256), ('c_k', 256), ('c_v', 256), ('c_qi', 256), ('c_ki', 64),
                  ('c_wi', 4), ('c_g', 256),
                  ('d_q', 256), ('d_k', 256), ('d_v', 256), ('d_g', 256)):
    _C[_name] = (_off, _off + _n)
    _off += _n

W_A, W_G, W_B, W_C, W_I, W_D = 896, 1024, 768, 768, 640, 768
OFF_A = 0
OFF_G = OFF_A + W_A
OFF_B = OFF_G + W_G
OFF_C = OFF_B + W_B
OFF_I = OFF_C + W_C
OFF_D = OFF_I + W_I
W_ALL = OFF_D + W_D


def _cparams(sem):
    return pltpu.CompilerParams(dimension_semantics=sem, vmem_limit_bytes=VMEM_LIMIT_BYTES)


def _split(x):
    hi = x.astype(BF16)
    lo = (x - hi.astype(F32)).astype(BF16)
    return hi, lo


def _dot(a, b):
    return jnp.dot(a, b, preferred_element_type=F32)


def _dot_nt(a, b):
    return lax.dot_general(a, b, (((1,), (1,)), ((), ())), preferred_element_type=F32)


def _mm1(a, b):
    return _dot(a.astype(BF16), b.astype(BF16))


def _mm3(a, b_hi, b_lo):
    a_hi, a_lo = _split(a)
    return _dot(a_hi, b_hi) + (_dot(a_lo, b_hi) + _dot(a_hi, b_lo))


def _mm3_nt(a, b):
    a_hi, a_lo = _split(a)
    b_hi, b_lo = _split(b)
    return _dot_nt(a_hi, b_hi) + (_dot_nt(a_lo, b_hi) + _dot_nt(a_hi, b_lo))


def _head_ones():
    r = lax.broadcasted_iota(I32, (BR_W, BR_W), 0) // HEAD_DIM
    c = lax.broadcasted_iota(I32, (BR_W, BR_W), 1) // HEAD_DIM
    return jnp.where(r == c, 1.0, 0.0).astype(BF16)


def _headsum(x, bd):
    hi, lo = _split(x)
    return _dot(hi, bd) + _dot(lo, bd)


def _head_mask(h, shape):
    c = lax.broadcasted_iota(I32, shape, len(shape) - 1) // HEAD_DIM
    return c == h


def _ada_kernel(c_ref, w_ref, b_ref, o_ref):
    w = w_ref[...]
    w_hi, w_lo = _split(w)
    o_ref[...] = _mm3(c_ref[...], w_hi, w_lo) + b_ref[...]


def _ada(c, w_ada, b_ada):
    bc = c.shape[0]
    n = w_ada.shape[1]
    tn = 1024
    return pl.pallas_call(
        _ada_kernel,
        grid=(n // tn,),
        in_specs=[pl.BlockSpec((bc, D_MODEL), lambda j: (0, 0)),
                  pl.BlockSpec((D_MODEL, tn), lambda j: (0, j)),
                  pl.BlockSpec((1, tn), lambda j: (0, j))],
        out_specs=pl.BlockSpec((bc, tn), lambda j: (0, j)),
        out_shape=jax.ShapeDtypeStruct((bc, n), F32),
        compiler_params=_cparams(("arbitrary",)),
        name="ada",
    )(c, w_ada, b_ada.reshape(1, n))


def _rope_tables(pos, rot_dims, theta, width):
    half = rot_dims // 2
    inv = jnp.power(jnp.float32(theta), -jnp.arange(half, dtype=jnp.float32) / half)
    ang = pos.astype(jnp.float32)[:, None] * inv[None, :]
    cos = jnp.cos(ang)
    sin = jnp.sin(ang)
    t = pos.shape[0]
    one = jnp.ones((t, HEAD_DIM - rot_dims), F32)
    zero = jnp.zeros((t, HEAD_DIM - rot_dims), F32)
    zh = jnp.zeros((t, half), F32)
    cos_h = jnp.concatenate([cos, cos, one], axis=1)
    up_h = jnp.concatenate([-sin, zh, zero], axis=1)
    dn_h = jnp.concatenate([zh, sin, zero], axis=1)
    reps = width // HEAD_DIM
    return jnp.stack([jnp.tile(cos_h, (1, reps)), jnp.tile(up_h, (1, reps)), jnp.tile(dn_h, (1, reps))])


def _rope_apply(x, tab_ref, lo, hi, half):
    n = hi - lo
    cos = tab_ref[0, :, lo:hi]
    up = tab_ref[1, :, lo:hi]
    dn = tab_ref[2, :, lo:hi]
    return x * cos + pltpu.roll(x, n - half, 1) * up + pltpu.roll(x, half, 1) * dn


def _indexer_key_operand(ki4):
    k_hi, k_lo = _split(ki4)
    lane = lax.broadcasted_iota(I32, ki4.shape, 1)
    return jnp.where(lane < 2 * IDX_DIM, k_hi, jnp.where(lane < 3 * IDX_DIM, k_lo, jnp.zeros_like(k_lo)))


def _layer_in_kernel(x_ref, mod_ref, g_ref, wh_ref, wl_ref, tr_ref, ts_ref,
                     ua_ref, gt_ref, qb_ref, kb_ref, vb_ref, qc_ref, kc_ref, vc_ref,
                     qi_ref, ki_ref, wi_ref, qd_ref, kd_ref, vd_ref,
                     kcb_ref, vcb_ref, kix_ref, kdb_ref, vdb_ref, kbar_ref):
    x = x_ref[0]
    y = x * lax.rsqrt(jnp.mean(x * x, axis=-1, keepdims=True) + RMS_EPS) * g_ref[...]
    shift = mod_ref[0, :, 0:D_MODEL]
    scale = mod_ref[0, :, D_MODEL:2 * D_MODEL]
    h = y * (1.0 + scale) + shift
    h_hi, h_lo = _split(h)

    def proj3(lo, hi):
        b_hi = wh_ref[:, lo:hi]
        b_lo = wl_ref[:, lo:hi]
        return _dot(h_hi, b_hi) + (_dot(h_lo, b_hi) + _dot(h_hi, b_lo))

    def proj1(lo, hi):
        return _dot(h_hi, wh_ref[:, lo:hi])

    ua_ref[0] = proj3(OFF_A, OFF_A + W_A)
    gt_ref[0] = proj1(OFF_G, OFF_G + W_G)
    ub = proj1(OFF_B, OFF_B + W_B)
    qb_ref[0] = _rope_apply(ub[:, 0:256], tr_ref, 0, 256, HEAD_DIM // 2)
    kb_ref[0] = _rope_apply(ub[:, 256:512], tr_ref, 0, 256, HEAD_DIM // 2)
    vb_ref[0] = ub[:, 512:768]
    uc = proj1(OFF_C, OFF_C + W_C)
    qc_ref[0] = _rope_apply(uc[:, 0:256], ts_ref, 0, 256, ROPE_DIMS // 2)
    kc = _rope_apply(uc[:, 256:512], ts_ref, 0, 256, ROPE_DIMS // 2)
    kc_ref[0] = kc
    kcb_ref[0] = kc.astype(BF16)
    vc_ref[0] = uc[:, 512:768]
    vcb_ref[0] = uc[:, 512:768].astype(BF16)
    ui = proj3(OFF_I, OFF_I + W_I)
    qi_ref[0] = _rope_apply(ui[:, 0:256], ts_ref, 0, 256, ROPE_DIMS // 2)
    ki4 = _rope_apply(ui[:, 256:512], ts_ref, 0, 256, ROPE_DIMS // 2)
    ki_ref[0] = ki4
    kix_ref[0] = _indexer_key_operand(ki4)
    wi_ref[0] = ui[:, 512:640]
    uqk = proj3(OFF_D, OFF_D + 512)
    qd_ref[0] = _rope_apply(uqk[:, 0:256], ts_ref, 0, 256, ROPE_DIMS // 2)
    kd = _rope_apply(uqk[:, 256:512], ts_ref, 0, 256, ROPE_DIMS // 2)
    kd_ref[0] = kd
    kdb_ref[0] = kd.astype(BF16)
    kbar_ref[0, 0] = jnp.sum(kd, axis=0, keepdims=True) * (1.0 / MOBA_BLOCK)
    vd = proj1(OFF_D + 512, OFF_D + W_D)
    vd_ref[0] = vd
    vdb_ref[0] = vd.astype(BF16)


def _regroup_w_in(w_in):
    def cols(name):
        lo, hi = _C[name]
        return w_in[:, lo:hi]
    ki4 = jnp.tile(cols('c_ki'), (1, 4))
    wi_pad = jnp.pad(cols('c_wi'), ((0, 0), (0, 124)))
    w = jnp.concatenate([
        cols('a_r'), cols('a_k'), cols('a_v'), cols('a_wl'), cols('a_al'),
        cols('a_g'), cols('b_g'), cols('c_g'), cols('d_g'),
        cols('b_q'), cols('b_k'), cols('b_v'),
        cols('c_q'), cols('c_k'), cols('c_v'),
        cols('c_qi'), ki4, wi_pad,
        cols('d_q'), cols('d_k'), cols('d_v')], axis=1)
    return _split(w)


def _layer_in(x, mod, g_pre, w_hi, w_lo, tab_ret, tab_std, tm):
    b, t, _ = x.shape
    r = mod.shape[1]
    if r == 1:
        mod_spec = pl.BlockSpec((1, 1, 3 * D_MODEL), lambda i, j: (i, 0, 0))
    else:
        mod_spec = pl.BlockSpec((1, tm, 3 * D_MODEL), lambda i, j: (i, j, 0))
    widths = (W_A, W_G, 256, 256, 256, 256, 256, 256, 256, 256, 128, 256, 256, 256)
    seq = lambda w: pl.BlockSpec((1, tm, w), lambda i, j: (i, j, 0))
    out_shape = (tuple(jax.ShapeDtypeStruct((b, t, w), F32) for w in widths)
                 + tuple(jax.ShapeDtypeStruct((b, t, BR_W), BF16) for _ in range(5))
                 + (jax.ShapeDtypeStruct((b, t // tm, 1, BR_W), F32),))
    out_specs = (tuple(seq(w) for w in widths) + tuple(seq(BR_W) for _ in range(5))
                 + (pl.BlockSpec((1, 1, 1, BR_W), lambda i, j: (i, j, 0, 0)),))
    return pl.pallas_call(
        _layer_in_kernel,
        grid=(b, t // tm),
        in_specs=[pl.BlockSpec((1, tm, D_MODEL), lambda i, j: (i, j, 0)),
                  mod_spec,
                  pl.BlockSpec((1, D_MODEL), lambda i, j: (0, 0)),
                  pl.BlockSpec((D_MODEL, W_ALL), lambda i, j: (0, 0), pipeline_mode=pl.Buffered(1)),
                  pl.BlockSpec((D_MODEL, W_ALL), lambda i, j: (0, 0), pipeline_mode=pl.Buffered(1)),
                  pl.BlockSpec((3, tm, 256), lambda i, j: (0, j, 0)),
                  pl.BlockSpec((3, tm, 256), lambda i, j: (0, j, 0))],
        out_specs=out_specs,
        out_shape=out_shape,
        compiler_params=_cparams(("arbitrary", "arbitrary")),
        name="layer_in",
    )(x, mod, g_pre.reshape(1, D_MODEL), w_hi, w_lo, tab_ret, tab_std)


def _rwkv_pre_math(ua, prev, mu, w0, w2h, w2l, a0, a2h, a2l, kkp, ka, rk, bd):
    xs = ua + (prev - ua) * mu
    r = xs[:, 0:256]
    k = xs[:, 256:512]
    v = xs[:, 512:768]
    wl = xs[:, 768:832]
    al = xs[:, 832:896]
    zw = w0 + _mm3(jnp.tanh(wl), w2h, w2l)
    w_log = -jax.nn.softplus(-zw) - 0.5
    decay = jnp.exp(-jnp.exp(w_log))
    a = jax.nn.sigmoid(a0 + _mm3(al, a2h, a2l))
    kq = k * kkp
    kk = kq * lax.rsqrt(_headsum(kq * kq, bd) + 1e-12)
    k2 = k * (1.0 + (a - 1.0) * ka)
    bonus = _headsum(r * k2 * rk, bd) * v
    return r, decay, k2, v, kk, kk * a, bonus


def _rwkv_pre_shift_kernel(ua_ref, up_ref, p0_ref, mu_ref, w0_ref, w2h_ref, w2l_ref, a0_ref, a2h_ref, a2l_ref,
                           kkp_ref, ka_ref, rk_ref, r_ref, w_ref, k_ref, v_ref, kk_ref, b_ref, bo_ref):
    j = pl.program_id(1)
    ua = ua_ref[0]
    tm = ua.shape[0]
    first = jnp.where(j == 0, p0_ref[0], up_ref[0, 7:8, :])
    row = lax.broadcasted_iota(I32, ua.shape, 0)
    prev = jnp.where(row == 0, first, pltpu.roll(ua, 1, 0))
    outs = _rwkv_pre_math(ua, prev, mu_ref[...], w0_ref[...], w2h_ref[...], w2l_ref[...], a0_ref[...],
                          a2h_ref[...], a2l_ref[...], kkp_ref[...], ka_ref[...], rk_ref[...], _head_ones())
    for o_ref, o in zip((r_ref, w_ref, k_ref, v_ref, kk_ref, b_ref, bo_ref), outs):
        o_ref[0] = o


def _rwkv_pre_rows_kernel(ua_ref, pv_ref, mu_ref, w0_ref, w2h_ref, w2l_ref, a0_ref, a2h_ref, a2l_ref,
                          kkp_ref, ka_ref, rk_ref, r_ref, w_ref, k_ref, v_ref, kk_ref, b_ref, bo_ref):
    outs = _rwkv_pre_math(ua_ref[...], pv_ref[...], mu_ref[...], w0_ref[...], w2h_ref[...], w2l_ref[...],
                          a0_ref[...], a2h_ref[...], a2l_ref[...], kkp_ref[...], ka_ref[...], rk_ref[...],
                          _head_ones())
    for o_ref, o in zip((r_ref, w_ref, k_ref, v_ref, kk_ref, b_ref, bo_ref), outs):
        o_ref[...] = o


def _rwkv_params(p):
    w2h, w2l = _split(p['a_w2'])
    a2h, a2l = _split(p['a_a2'])
    row = lambda v: v.reshape(1, -1)
    return (row(p['a_mu']), row(p['a_w0']), w2h, w2l, row(p['a_a0']), a2h, a2l,
            row(p['a_kk']), row(p['a_ka']), row(p['a_rk']))


def _rwkv_pre_prompt(ua, prev0, p, tm):
    b, t, _ = ua.shape
    prm = _rwkv_params(p)
    full = lambda a: pl.BlockSpec(a.shape, lambda i, j: (0,) * a.ndim)
    blk8 = tm // 8
    return pl.pallas_call(
        _rwkv_pre_shift_kernel,
        grid=(b, t // tm),
        in_specs=[pl.BlockSpec((1, tm, A_SHIFT_W), lambda i, j: (i, j, 0)),
                  pl.BlockSpec((1, 8, A_SHIFT_W), lambda i, j: (i, jnp.maximum(j * blk8 - 1, 0), 0)),
                  pl.BlockSpec((1, 1, A_SHIFT_W), lambda i, j: (i, 0, 0))] + [full(a) for a in prm],
        out_specs=tuple(pl.BlockSpec((1, tm, BR_W), lambda i, j: (i, j, 0)) for _ in range(7)),
        out_shape=tuple(jax.ShapeDtypeStruct((b, t, BR_W), F32) for _ in range(7)),
        compiler_params=_cparams(("arbitrary", "arbitrary")),
        name="rwkv_pre",
    )(ua, ua, prev0.reshape(b, 1, A_SHIFT_W), *prm)


def _rwkv_pre_rows(ua, prev, p):
    n = ua.shape[0]
    prm = _rwkv_params(p)
    full = lambda a: pl.BlockSpec(a.shape, lambda: (0,) * a.ndim)
    return pl.pallas_call(
        _rwkv_pre_rows_kernel,
        in_specs=[full(ua), full(prev)] + [full(a) for a in prm],
        out_specs=tuple(pl.BlockSpec((n, BR_W), lambda: (0, 0)) for _ in range(7)),
        out_shape=tuple(jax.ShapeDtypeStruct((n, BR_W), F32) for _ in range(7)),
        name="rwkv_pre_rows",
    )(ua, prev, *prm)


def _rwkv_scan_kernel(r_ref, w_ref, k_ref, v_ref, kk_ref, b_ref, s0_ref, y_ref, sout_ref, s_scr, *, tc):
    c = pl.program_id(0)

    @pl.when(c == 0)
    def _():
        s_scr[...] = s0_ref[...]

    nb = s_scr.shape[0]
    bd = _head_ones()
    vi = lax.broadcasted_iota(I32, (HEAD_DIM, BR_W), 0)
    ci = lax.broadcasted_iota(I32, (HEAD_DIM, BR_W), 1)
    diag = ((ci % HEAD_DIM) == vi)[None]

    groups = [(g, min(g + RWKV_GROUP, nb)) for g in range(0, nb, RWKV_GROUP)]

    def step(t, carry):
        for lo, hi in groups:
            n = hi - lo
            row = lambda ref: ref[lo:hi, pl.ds(t, 1), :]
            s = s_scr[lo:hi]
            p = (s * row(kk_ref)).astype(BF16)
            vd = jnp.where(diag, row(v_ref), 0.0).astype(BF16)
            lhs = jnp.concatenate([p, vd], axis=1).reshape(n * 2 * HEAD_DIM, BR_W)
            res = _dot(lhs, bd).reshape(n, 2 * HEAD_DIM, BR_W)
            s = s * row(w_ref) - res[:, :HEAD_DIM] * row(b_ref) + res[:, HEAD_DIM:] * row(k_ref)
            s_scr[lo:hi] = s
            q = (s * row(r_ref)).astype(BF16).reshape(n * HEAD_DIM, BR_W)
            yb = _dot(q, bd).reshape(n, HEAD_DIM, BR_W)
            y_ref[lo:hi, pl.ds(t, 1), :] = jnp.sum(jnp.where(diag, yb, 0.0), axis=1, keepdims=True)
        return carry

    lax.fori_loop(0, tc, step, 0, unroll=4)

    @pl.when(c == pl.num_programs(0) - 1)
    def _():
        sout_ref[...] = s_scr[...]


def _rwkv_scan(r, w, k, v, kk, bb, s0, tc):
    b, t, _ = r.shape
    seq = pl.BlockSpec((b, tc, BR_W), lambda c: (0, c, 0))
    st = pl.BlockSpec((b, HEAD_DIM, BR_W), lambda c: (0, 0, 0))
    return pl.pallas_call(
        functools.partial(_rwkv_scan_kernel, tc=tc),
        grid=(t // tc,),
        in_specs=[seq] * 6 + [st],
        out_specs=(seq, st),
        out_shape=(jax.ShapeDtypeStruct((b, t, BR_W), F32), jax.ShapeDtypeStruct((b, HEAD_DIM, BR_W), F32)),
        scratch_shapes=[pltpu.VMEM((b, HEAD_DIM, BR_W), F32)],
        compiler_params=_cparams(("arbitrary",)),
        name="rwkv_scan",
    )(r, w, k, v, kk, bb, s0)


def _wkv_to_scan_layout(s):
    b = s.shape[0]
    return jnp.transpose(s, (0, 2, 1, 3)).reshape(b, HEAD_DIM, BR_W)


def _wkv_from_scan_layout(s):
    b = s.shape[0]
    return jnp.transpose(s.reshape(b, HEAD_DIM, N_HEADS, HEAD_DIM), (0, 2, 1, 3))


def _rwkv_step_kernel(r_ref, w_ref, k_ref, vc_ref, kk_ref, b_ref, s_ref, y_ref, so_ref):
    s = s_ref[...]
    sk = jnp.sum(s * kk_ref[...], axis=-1, keepdims=True)
    s = s * w_ref[...] - sk * b_ref[...] + vc_ref[...] * k_ref[...]
    so_ref[...] = s
    y_ref[...] = jnp.sum(s * r_ref[...], axis=-1, keepdims=True)


def _rwkv_step(r, w, k, v, kk, bb, s0):
    b = r.shape[0]
    n = b * N_HEADS
    rowf = lambda a: a.reshape(n, 1, HEAD_DIM)
    full = lambda shape: pl.BlockSpec(shape, lambda: (0,) * len(shape))
    y, s = pl.pallas_call(
        _rwkv_step_kernel,
        in_specs=[full((n, 1, HEAD_DIM))] * 3 + [full((n, HEAD_DIM, 1))] + [full((n, 1, HEAD_DIM))] * 2
                 + [full((n, HEAD_DIM, HEAD_DIM))],
        out_specs=(full((n, HEAD_DIM, 1)), full((n, HEAD_DIM, HEAD_DIM))),
        out_shape=(jax.ShapeDtypeStruct((n, HEAD_DIM, 1), F32), jax.ShapeDtypeStruct((n, HEAD_DIM, HEAD_DIM), F32)),
        compiler_params=pltpu.CompilerParams(vmem_limit_bytes=VMEM_LIMIT_BYTES),
        name="rwkv_step",
    )(rowf(r), rowf(w), rowf(k), v.reshape(n, HEAD_DIM, 1), rowf(kk), rowf(bb), s0.reshape(n, HEAD_DIM, HEAD_DIM))
    return y.reshape(b, BR_W), s.reshape(b, N_HEADS, HEAD_DIM, HEAD_DIM)


def _ret_tables(c):
    log_g = jnp.log(1.0 - jnp.power(2.0, -5.0 - jnp.arange(N_HEADS, dtype=jnp.float32)))
    i = jnp.arange(c, dtype=jnp.float32)
    diff = i[:, None] - i[None, :]
    dmat = jnp.where(diff[None] >= 0, jnp.exp(jnp.maximum(diff, 0.0)[None] * log_g[:, None, None]), 0.0)
    dq = jnp.exp((i[:, None] + 1.0) * log_g[None, :])
    dk = jnp.exp((c - 1.0 - i)[:, None] * log_g[None, :])
    ds = jnp.exp(c * log_g)
    lanes = lambda a: jnp.repeat(a, HEAD_DIM, axis=-1)
    return dmat, lanes(dq), lanes(dk), lanes(ds[None, :])


def _ret_kernel(q_ref, k_ref, v_ref, s0_ref, dm_ref, dq_ref, dk_ref, ds_ref, gw_ref, gb_ref,
                y_ref, so_ref, s_scr):
    j = pl.program_id(1)

    @pl.when(j == 0)
    def _():
        s_scr[...] = s0_ref[0]

    q = q_ref[0]
    k = k_ref[0] * (HEAD_DIM ** -0.5)
    v = v_ref[0]
    s = s_scr[...]
    bd = _head_ones()
    qb = q.astype(BF16)
    kb = k.astype(BF16)
    vb = v.astype(BF16)
    y = _dot(qb, s.astype(BF16)) * dq_ref[...]
    for h in range(N_HEADS):
        hm = _head_mask(h, (1, BR_W))
        att = _dot_nt(jnp.where(hm, q, 0.0).astype(BF16), kb) * dm_ref[h]
        y = y + jnp.where(hm, _dot(att.astype(BF16), vb), 0.0)
    kd = (k * dk_ref[...]).T.astype(BF16)
    s_scr[...] = s * ds_ref[...] + bd.astype(F32) * _dot(kd, vb)
    mu = _headsum(y, bd) * (1.0 / HEAD_DIM)
    d = y - mu
    var = _headsum(d * d, bd) * (1.0 / HEAD_DIM)
    y_ref[0] = d * lax.rsqrt(var + RET_GN_EPS) * gw_ref[...] + gb_ref[...]

    @pl.when(j == pl.num_programs(1) - 1)
    def _():
        so_ref[0] = s_scr[...]


def _ret_state_embed(s):
    b = s.shape[0]
    eye = jnp.eye(N_HEADS, dtype=s.dtype)
    return jnp.einsum('bhde,hg->bhdge', s, eye).reshape(b, BR_W, BR_W)


def _ret_state_extract(s):
    b = s.shape[0]
    s4 = s.reshape(b, N_HEADS, HEAD_DIM, N_HEADS, HEAD_DIM)
    return jnp.stack([s4[:, h, :, h, :] for h in range(N_HEADS)], axis=1)


def _retention_prompt(q, k, v, s0, gn_w, gn_b, c):
    b, t, _ = q.shape
    dmat, dq, dk, ds = _ret_tables(c)
    seq = pl.BlockSpec((1, c, BR_W), lambda i, j: (i, j, 0))
    st = pl.BlockSpec((1, BR_W, BR_W), lambda i, j: (i, 0, 0))
    const = lambda a: pl.BlockSpec(a.shape, lambda i, j: (0,) * a.ndim)
    gw = gn_w.reshape(1, BR_W)
    gb = gn_b.reshape(1, BR_W)
    y, s = pl.pallas_call(
        _ret_kernel,
        grid=(b, t // c),
        in_specs=[seq, seq, seq, st, const(dmat), const(dq), const(dk), const(ds), const(gw), const(gb)],
        out_specs=(seq, st),
        out_shape=(jax.ShapeDtypeStruct((b, t, BR_W), F32), jax.ShapeDtypeStruct((b, BR_W, BR_W), F32)),
        scratch_shapes=[pltpu.VMEM((BR_W, BR_W), F32)],
        compiler_params=_cparams(("arbitrary", "arbitrary")),
        name="retention",
    )(q, k, v, _ret_state_embed(s0), dmat, dq, dk, ds, gw, gb)
    return y, _ret_state_extract(s)


def _ret_step_kernel(qc_ref, kc_ref, qr_ref, kr_ref, v_ref, s_ref, g_ref, gw_ref, gb_ref, y_ref, so_ref):
    s = s_ref[...]
    g = g_ref[...]
    v = v_ref[...]
    qk = jnp.sum(qr_ref[...] * kr_ref[...], axis=-1, keepdims=True)
    y = qk * v + jnp.sum(qc_ref[...] * s, axis=1, keepdims=True) * g
    so_ref[...] = s * g + kc_ref[...] * v
    mu = jnp.mean(y, axis=-1, keepdims=True)
    d = y - mu
    var = jnp.mean(d * d, axis=-1, keepdims=True)
    y_ref[...] = d * lax.rsqrt(var + RET_GN_EPS) * gw_ref[...] + gb_ref[...]


def _retention_step(q, k, v, s0, gn_w, gn_b):
    b = q.shape[0]
    n = b * N_HEADS
    ks = k * (HEAD_DIM ** -0.5)
    g = 1.0 - jnp.power(2.0, -5.0 - jnp.arange(N_HEADS, dtype=jnp.float32))
    g = jnp.tile(g, (b,)).reshape(n, 1, 1)
    gw = jnp.tile(gn_w.reshape(N_HEADS, HEAD_DIM), (b, 1)).reshape(n, 1, HEAD_DIM)
    gb = jnp.tile(gn_b.reshape(N_HEADS, HEAD_DIM), (b, 1)).reshape(n, 1, HEAD_DIM)
    col = lambda a: a.reshape(n, HEAD_DIM, 1)
    row = lambda a: a.reshape(n, 1, HEAD_DIM)
    args = (col(q), col(ks), row(q), row(ks), row(v), s0.reshape(n, HEAD_DIM, HEAD_DIM), g, gw, gb)
    full = lambda a: pl.BlockSpec(a.shape, lambda: (0,) * a.ndim)
    y, s = pl.pallas_call(
        _ret_step_kernel,
        in_specs=[full(a) for a in args],
        out_specs=(pl.BlockSpec((n, 1, HEAD_DIM), lambda: (0, 0, 0)),
                   pl.BlockSpec((n, HEAD_DIM, HEAD_DIM), lambda: (0, 0, 0))),
        out_shape=(jax.ShapeDtypeStruct((n, 1, HEAD_DIM), F32), jax.ShapeDtypeStruct((n, HEAD_DIM, HEAD_DIM), F32)),
        compiler_params=pltpu.CompilerParams(vmem_limit_bytes=VMEM_LIMIT_BYTES),
        name="retention_step",
    )(*args)
    return y.reshape(b, BR_W), s.reshape(b, N_HEADS, HEAD_DIM, HEAD_DIM)


INT_MIN = -2 ** 31


def _sort_key(s):
    s = jnp.where(s == 0.0, 0.0, s)
    bits = lax.bitcast_convert_type(s, I32)
    return jnp.where(bits < 0, bits ^ jnp.int32(0x7FFFFFFF), bits)


def _kth_largest_key(count_ge, rows, k):
    def bit_step(i, tb):
        cand = tb + jnp.left_shift(jnp.int32(1), 31 - i)
        return jnp.where(count_ge(cand) >= k, cand, tb)

    return lax.fori_loop(0, 32, bit_step, jnp.full((rows, 128), INT_MIN, I32))


def _attn_queries(qf, qa_scr):
    qs = qf * (HEAD_DIM ** -0.5)
    for h in range(N_HEADS):
        qa_scr[h] = jnp.where(_head_mask(h, (1, BR_W)), qs, 0.0).astype(BF16)


def _flash_update(qa_scr, kb, vb, msk, stats, acc):
    new_stats = []
    alpha_l = jnp.zeros_like(acc)
    pv = jnp.zeros_like(acc)
    for h in range(N_HEADS):
        hm = _head_mask(h, (1, BR_W))
        m_old, l_old = stats[h]
        mk = msk[h] if isinstance(msk, (list, tuple)) else msk
        s = jnp.where(mk, _dot_nt(qa_scr[h], kb), NEG)
        m_new = jnp.maximum(m_old, jnp.max(s, axis=-1, keepdims=True))
        alpha = jnp.exp(m_old - m_new)
        p = jnp.exp(s - m_new)
        l_new = alpha * l_old + jnp.sum(p, axis=-1, keepdims=True)
        alpha_l = jnp.where(hm, alpha, alpha_l)
        pv = jnp.where(hm, _dot(p.astype(BF16), vb), pv)
        new_stats.append((m_new, l_new))
    return tuple(new_stats), acc * alpha_l + pv


def _flash_init(tq):
    stats = tuple((jnp.full((tq, 1), NEG, F32), jnp.zeros((tq, 1), F32)) for _ in range(N_HEADS))
    return stats, jnp.zeros((tq, BR_W), F32)


def _flash_finish(stats, acc):
    l_l = jnp.zeros_like(acc)
    for h in range(N_HEADS):
        l_l = jnp.where(_head_mask(h, (1, BR_W)), stats[h][1], l_l)
    return acc / l_l


I16 = jnp.int16
I16_MIN = -32768


def _kth_largest_i16(count_ge, rows, k):
    def bit_step(i, tb):
        cand = tb + jnp.left_shift(jnp.int32(1), 15 - i).astype(I16)
        take = jnp.broadcast_to(jnp.where(count_ge(cand) >= k, 1, 0), (rows, 128)).astype(I16)
        return jnp.where(take > 0, cand, tb)

    return lax.fori_loop(0, 16, bit_step, jnp.full((rows, 128), I16_MIN, I16))


def _dsa_prompt_kernel(q_ref, qi_ref, wi_ref, kb_ref, vb_ref, kx_ref, o_ref,
                       hi_scr, lo_scr, qs_scr, qa_scr, *, tq, kb, topk):
    qt = pl.program_id(1)
    nkc = (qt + 1) * (tq // kb)
    rowpos = qt * tq + lax.broadcasted_iota(I32, (tq, kb), 0)
    coli = lax.broadcasted_iota(I32, (tq, kb), 1)
    lane = lax.broadcasted_iota(I32, (tq, BR_W), 1)
    qi = qi_ref[0]
    wi = wi_ref[0] * (IDX_HEADS ** -0.5 * IDX_DIM ** -0.5)
    for h in range(IDX_HEADS):
        qm = jnp.where(lane // IDX_DIM == h, qi, 0.0)
        rep = (qm + pltpu.roll(qm, IDX_DIM, 1)) + (pltpu.roll(qm, 2 * IDX_DIM, 1) + pltpu.roll(qm, 3 * IDX_DIM, 1))
        q_hi, q_lo = _split(rep)
        qs_scr[h] = jnp.where((lane < IDX_DIM) | ((lane >= 2 * IDX_DIM) & (lane < 3 * IDX_DIM)), q_hi,
                              jnp.where(lane < 2 * IDX_DIM, q_lo, jnp.zeros_like(q_lo)))
    _attn_queries(q_ref[0], qa_scr)

    def score_chunk(c, carry):
        kx = kx_ref[0, pl.ds(pl.multiple_of(c * kb, kb), kb), :]
        s = jnp.zeros((tq, kb), F32)
        for h in range(IDX_HEADS):
            s = s + jnp.maximum(_dot_nt(qs_scr[h], kx), 0.0) * wi[:, h:h + 1]
        s = jnp.where(c * kb + coli <= rowpos, s, -jnp.inf)
        key = _sort_key(s)
        hi_scr[c] = jnp.right_shift(key, 16).astype(I16)
        lo_scr[c] = ((key & 0xFFFF) - 32768).astype(I16)
        return carry

    lax.fori_loop(0, nkc, score_chunk, 0)

    one = jnp.ones((tq, 128), I16)
    zero = jnp.zeros((tq, 128), I16)

    def counter(scr, cmp):
        def count(cand):
            def body(c, acc):
                kc = scr[c]
                for j in range(kb // 128):
                    acc = acc + jnp.where(cmp(kc[:, j * 128:(j + 1) * 128], cand), one, zero)
                return acc
            acc = lax.fori_loop(0, nkc, body, zero)
            return jnp.sum(acc.astype(F32), axis=-1, keepdims=True)
        return count

    th = _kth_largest_i16(counter(hi_scr, lambda a, b: a >= b), tq, float(topk))
    k2 = float(topk) - counter(hi_scr, lambda a, b: a > b)(th)

    def restrict(c, carry):
        hc = hi_scr[c]
        lc = lo_scr[c]
        lo_scr[c] = jnp.concatenate(
            [jnp.where(hc[:, j * 128:(j + 1) * 128] == th, lc[:, j * 128:(j + 1) * 128], jnp.int16(I16_MIN))
             for j in range(kb // 128)], axis=1)
        return carry

    lax.fori_loop(0, nkc, restrict, 0)
    tl = _kth_largest_i16(counter(lo_scr, lambda a, b: a >= b), tq, k2)
    rem = k2 - counter(lo_scr, lambda a, b: a > b)(tl)
    upper = (lax.broadcasted_iota(I32, (kb, kb), 0) <= lax.broadcasted_iota(I32, (kb, kb), 1)).astype(BF16)

    def attn_chunk(c, carry):
        stats, acc, run = carry
        hc = hi_scr[c]
        lc = lo_scr[c]
        cls = []
        for j in range(kb // 128):
            hj = hc[:, j * 128:(j + 1) * 128]
            lj = lc[:, j * 128:(j + 1) * 128]
            same = hj == th
            above = (hj > th) | (same & (lj > tl))
            cls.append(jnp.where(above, jnp.int16(2), jnp.where(same & (lj == tl), jnp.int16(1), jnp.int16(0))))
        cls = jnp.concatenate(cls, axis=1).astype(I32)
        eq = cls == 1
        eqf = jnp.where(eq, 1.0, 0.0)
        pre = _dot(eqf.astype(BF16), upper) + run
        sel = (cls == 2) | (eq & (pre <= rem))
        msk = sel & (c * kb + coli <= rowpos)
        off = pl.multiple_of(c * kb, kb)
        stats, acc = _flash_update(qa_scr, kb_ref[0, pl.ds(off, kb), :], vb_ref[0, pl.ds(off, kb), :],
                                   msk, stats, acc)
        return stats, acc, run + jnp.sum(eqf, axis=-1, keepdims=True)

    stats, acc = _flash_init(tq)
    stats, acc, _ = lax.fori_loop(0, nkc, attn_chunk, (stats, acc, jnp.zeros((tq, 1), F32)))
    o_ref[0] = _flash_finish(stats, acc)


def _dsa_prompt(q, qi, wi, kb16, vb16, kx, tq=256, kb=256):
    b, t, _ = q.shape
    topk = min(DSA_TOPK_MAX, t // 4)
    assert tq >= topk and tq % kb == 0
    tile = lambda w: pl.BlockSpec((1, tq, w), lambda i, j: (i, j, 0))
    full = pl.BlockSpec((1, t, BR_W), lambda i, j: (i, 0, 0))
    return pl.pallas_call(
        functools.partial(_dsa_prompt_kernel, tq=tq, kb=kb, topk=topk),
        grid=(b, t // tq),
        in_specs=[tile(BR_W), tile(BR_W), tile(128), full, full, full],
        out_specs=tile(BR_W),
        out_shape=jax.ShapeDtypeStruct((b, t, BR_W), F32),
        scratch_shapes=[pltpu.VMEM((t // kb, tq, kb), I16), pltpu.VMEM((t // kb, tq, kb), I16),
                        pltpu.VMEM((IDX_HEADS, tq, BR_W), BF16), pltpu.VMEM((N_HEADS, tq, BR_W), BF16)],
        compiler_params=_cparams(("arbitrary", "arbitrary")),
        name="dsa_prompt",
    )(q, qi, wi, kb16, vb16, kx)


def _top_blocks(g, col, limit, nsel):
    g = jnp.where(col < limit, g, -jnp.inf)
    selm = jnp.zeros(g.shape, jnp.bool_)
    for _ in range(nsel):
        mx = jnp.max(g, axis=-1, keepdims=True)
        idx = jnp.min(jnp.where(g == mx, col, jnp.int32(1 << 20)), axis=-1, keepdims=True)
        pick = col == idx
        selm = selm | (pick & (col < limit))
        g = jnp.where(pick, -jnp.inf, g)
    return selm


def _moba_prompt_kernel(q_ref, kbar_ref, kb_ref, vb_ref, o_ref, qa_scr, *, nsel):
    qt = pl.program_id(1)
    tq = MOBA_BLOCK
    qf = q_ref[0]
    kbar = kbar_ref[0]
    col = lax.broadcasted_iota(I32, (tq, 128), 1)
    sel = []
    for h in range(N_HEADS):
        g = _mm3_nt(jnp.where(_head_mask(h, (1, BR_W)), qf, 0.0), kbar)
        sel.append(jnp.where(_top_blocks(g, col, qt, nsel), 1.0, 0.0))
    _attn_queries(qf, qa_scr)

    tri = lax.broadcasted_iota(I32, (tq, tq), 1) <= lax.broadcasted_iota(I32, (tq, tq), 0)
    off = pl.multiple_of(qt * tq, tq)
    stats, acc = _flash_init(tq)
    stats, acc = _flash_update(qa_scr, kb_ref[0, pl.ds(off, tq), :], vb_ref[0, pl.ds(off, tq), :],
                               tri, stats, acc)

    def past(n, carry):
        stats, acc = carry
        o = pl.multiple_of(n * tq, tq)
        allow = [jnp.sum(jnp.where(col == n, sel[h], 0.0), axis=-1, keepdims=True) > 0.0
                 for h in range(N_HEADS)]
        return _flash_update(qa_scr, kb_ref[0, pl.ds(o, tq), :], vb_ref[0, pl.ds(o, tq), :],
                             allow, stats, acc)

    stats, acc = lax.fori_loop(0, qt, past, (stats, acc))
    o_ref[0] = _flash_finish(stats, acc)


def _moba_prompt(q, kbar, kb16, vb16):
    b, t, _ = q.shape
    nb = t // MOBA_BLOCK
    nsel = min(MOBA_TOPK, nb - 1)
    kbar = jnp.pad(kbar, ((0, 0), (0, 128 - nb), (0, 0)))
    tile = pl.BlockSpec((1, MOBA_BLOCK, BR_W), lambda i, j: (i, j, 0))
    full = pl.BlockSpec((1, t, BR_W), lambda i, j: (i, 0, 0))
    return pl.pallas_call(
        functools.partial(_moba_prompt_kernel, nsel=nsel),
        grid=(b, nb),
        in_specs=[tile, pl.BlockSpec((1, 128, BR_W), lambda i, j: (i, 0, 0)), full, full],
        out_specs=tile,
        out_shape=jax.ShapeDtypeStruct((b, t, BR_W), F32),
        scratch_shapes=[pltpu.VMEM((N_HEADS, MOBA_BLOCK, BR_W), BF16)],
        compiler_params=_cparams(("arbitrary", "arbitrary")),
        name="moba_prompt",
    )(q, kbar, kb16, vb16)


def _layer_out_kernel(x_ref, mod_ref, gpre_ref, gpost_ref, ya_ref, bo_ref, yb_ref, yc_ref, yd_ref, gt_ref,
                      lnw_ref, lnb_ref, wb_ref, wm_ref, wo_ref, o_ref):
    x = x_ref[0]
    y = x * lax.rsqrt(jnp.mean(x * x, axis=-1, keepdims=True) + RMS_EPS) * gpre_ref[...]
    shift = mod_ref[0, :, 0:D_MODEL]
    scale = mod_ref[0, :, D_MODEL:2 * D_MODEL]
    gate = mod_ref[0, :, 2 * D_MODEL:3 * D_MODEL]
    hb = (y * (1.0 + scale) + shift).astype(BF16)
    bd = _head_ones()
    ya = ya_ref[0]
    mu = _headsum(ya, bd) * (1.0 / HEAD_DIM)
    d = ya - mu
    var = _headsum(d * d, bd) * (1.0 / HEAD_DIM)
    ya = d * lax.rsqrt(var + RWKV_GN_EPS) * lnw_ref[...] + lnb_ref[...] + bo_ref[0]
    outs = (ya, yb_ref[0], yc_ref[0], yd_ref[0])
    merged = jnp.zeros(x.shape, F32)
    for n in range(4):
        o = outs[n] * jax.nn.silu(gt_ref[0, :, n * BR_W:(n + 1) * BR_W])
        merged = merged + jax.nn.sigmoid(_dot(hb, wm_ref[n])) * _dot(o.astype(BF16), wb_ref[n])
    z = _dot(merged.astype(BF16), wo_ref[...])
    z = z * lax.rsqrt(jnp.mean(z * z, axis=-1, keepdims=True) + RMS_EPS) * gpost_ref[...]
    o_ref[0] = x + gate * z


def _layer_out(x, mod, p, ya, bonus, yb, yc, yd, gates, tm):
    b, t, _ = x.shape
    r = mod.shape[1]
    if r == 1:
        mod_spec = pl.BlockSpec((1, 1, 3 * D_MODEL), lambda i, j: (i, 0, 0))
    else:
        mod_spec = pl.BlockSpec((1, tm, 3 * D_MODEL), lambda i, j: (i, j, 0))
    tile = lambda w: pl.BlockSpec((1, tm, w), lambda i, j: (i, j, 0))
    row = lambda v: v.reshape(1, -1)
    const = lambda shape: pl.BlockSpec(shape, lambda i, j: (0,) * len(shape), pipeline_mode=pl.Buffered(1))
    vec = lambda n: pl.BlockSpec((1, n), lambda i, j: (0, 0))
    return pl.pallas_call(
        _layer_out_kernel,
        grid=(b, t // tm),
        in_specs=[tile(D_MODEL), mod_spec, vec(D_MODEL), vec(D_MODEL),
                  tile(BR_W), tile(BR_W), tile(BR_W), tile(BR_W), tile(BR_W), tile(D_MODEL),
                  vec(BR_W), vec(BR_W),
                  const((4, BR_W, D_MODEL)), const((4, D_MODEL, D_MODEL)), const((D_MODEL, D_MODEL))],
        out_specs=tile(D_MODEL),
        out_shape=jax.ShapeDtypeStruct((b, t, D_MODEL), F32),
        compiler_params=_cparams(("arbitrary", "arbitrary")),
        name="layer_out",
    )(x, mod, row(p['g_pre']), row(p['g_post']), ya, bonus, yb, yc, yd, gates,
      row(p['a_ln_w']), row(p['a_ln_b']),
      p['w_branch'].astype(BF16), p['w_merge'].astype(BF16), p['w_out'].astype(BF16))


PAGES_PER_STEP = 8


def _page_specs(block, g_count):
    tail = (0,) * (len(block) - 1)

    def spec(g):
        return pl.BlockSpec(block, lambda i, j, pt: (pt[i, j * g_count + g],) + tail)
    return [spec(g) for g in range(g_count)]


def _head_rows(x):
    hr = lax.broadcasted_iota(I32, (N_HEADS, BR_W), 0)
    hl = lax.broadcasted_iota(I32, (N_HEADS, BR_W), 1) // HEAD_DIM
    return jnp.where(hr == hl, x, 0.0), hr == hl


def _dsa_scores_kernel(pt_ref, qm_ref, wi_ref, kn_ref, *refs, g_count):
    page_refs = refs[:g_count]
    o_ref, on_ref = refs[g_count], refs[g_count + 1]
    j = pl.program_id(1)
    qm = qm_ref[0]
    q_hi, q_lo = _split(qm)
    w = wi_ref[0] * (IDX_HEADS ** -0.5)

    def combine(d):
        return jnp.sum(jnp.maximum(d * (IDX_DIM ** -0.5), 0.0) * w, axis=0, keepdims=True)

    for g in range(g_count):
        k_hi, k_lo = _split(page_refs[g][0])
        d = _dot_nt(q_hi, k_hi) + (_dot_nt(q_lo, k_hi) + _dot_nt(q_hi, k_lo))
        o_ref[0, g] = combine(d)

    @pl.when(j == 0)
    def _():
        dn = jnp.sum(qm * kn_ref[0], axis=-1, keepdims=True)
        on_ref[0] = jnp.broadcast_to(combine(dn), (1, 128))


def _dsa_scores(qi, wi, ki_new, pool, page_table):
    b, n_pages = page_table.shape
    g_count = min(PAGES_PER_STEP, n_pages)
    qm = qi.reshape(b, IDX_HEADS, IDX_DIM)
    wcol = wi.reshape(b, IDX_HEADS, 1)
    kn = ki_new.reshape(b, 1, IDX_DIM)
    grid_spec = pltpu.PrefetchScalarGridSpec(
        num_scalar_prefetch=1,
        grid=(b, n_pages // g_count),
        in_specs=[pl.BlockSpec((1, IDX_HEADS, IDX_DIM), lambda i, j, pt: (i, 0, 0)),
                  pl.BlockSpec((1, IDX_HEADS, 1), lambda i, j, pt: (i, 0, 0)),
                  pl.BlockSpec((1, 1, IDX_DIM), lambda i, j, pt: (i, 0, 0))]
                 + _page_specs((1, PAGE_SIZE, IDX_DIM), g_count),
        out_specs=(pl.BlockSpec((1, g_count, 1, PAGE_SIZE), lambda i, j, pt: (i, j, 0, 0)),
                   pl.BlockSpec((1, 1, 128), lambda i, j, pt: (i, 0, 0))))
    sc, sc_new = pl.pallas_call(
        functools.partial(_dsa_scores_kernel, g_count=g_count),
        grid_spec=grid_spec,
        out_shape=(jax.ShapeDtypeStruct((b, n_pages, 1, PAGE_SIZE), F32),
                   jax.ShapeDtypeStruct((b, 1, 128), F32)),
        compiler_params=_cparams(("arbitrary", "arbitrary")),
        name="dsa_scores",
    )(page_table, qm, wcol, kn, *([pool] * g_count))
    return sc.reshape(b, n_pages * PAGE_SIZE), sc_new[:, 0, 0:1]


def _topk_rows_kernel(s_ref, o_ref, key_scr, *, n_valid, topk):
    nblk, rows, _ = s_ref.shape
    col = lax.broadcasted_iota(I32, (rows, 128), 1)

    def to_key(j, carry):
        s = jnp.where(j * 128 + col < n_valid, s_ref[j], -jnp.inf)
        key_scr[j] = _sort_key(s)
        return carry

    lax.fori_loop(0, nblk, to_key, 0)

    def counter(cmp):
        def count(cand):
            body = lambda j, acc: acc + jnp.where(cmp(key_scr[j], cand), 1.0, 0.0)
            acc = lax.fori_loop(0, nblk, body, jnp.zeros((rows, 128), F32))
            return jnp.sum(acc, axis=-1, keepdims=True)
        return count

    tb = _kth_largest_key(counter(lambda a, b: a >= b), rows, float(topk))
    rem = float(topk) - counter(lambda a, b: a > b)(tb)
    upper = (lax.broadcasted_iota(I32, (128, 128), 0) <= lax.broadcasted_iota(I32, (128, 128), 1)).astype(BF16)

    def select(j, run):
        kc = key_scr[j]
        eq = kc == tb
        eqf = jnp.where(eq, 1.0, 0.0)
        pre = _dot(eqf.astype(BF16), upper) + run
        sel = ((kc > tb) | (eq & (pre <= rem))) & (j * 128 + col < n_valid)
        o_ref[j] = jnp.where(sel, 1.0, 0.0)
        return run + jnp.sum(eqf, axis=-1, keepdims=True)

    lax.fori_loop(0, nblk, select, jnp.zeros((rows, 1), F32))


def _topk_rows(scores, topk):
    rows, n = scores.shape
    nblk = -(-n // 128)
    s = jnp.pad(scores, ((0, 0), (0, nblk * 128 - n)))
    s = jnp.transpose(s.reshape(rows, nblk, 128), (1, 0, 2))
    m = pl.pallas_call(
        functools.partial(_topk_rows_kernel, n_valid=n, topk=topk),
        in_specs=[pl.BlockSpec((nblk, rows, 128), lambda: (0, 0, 0))],
        out_specs=pl.BlockSpec((nblk, rows, 128), lambda: (0, 0, 0)),
        out_shape=jax.ShapeDtypeStruct((nblk, rows, 128), F32),
        scratch_shapes=[pltpu.VMEM((nblk, rows, 128), I32)],
        compiler_params=pltpu.CompilerParams(vmem_limit_bytes=VMEM_LIMIT_BYTES),
        name="topk_rows",
    )(s)
    return jnp.transpose(m, (1, 0, 2)).reshape(rows, nblk * 128)[:, :n]


def _dsa_attn_kernel(pt_ref, q_ref, kn_ref, vn_ref, mn_ref, msk_ref, *refs, g_count):
    k_refs = refs[:g_count]
    v_refs = refs[g_count:2 * g_count]
    o_ref = refs[2 * g_count]
    m_scr, l_scr, acc_scr = refs[2 * g_count + 1:]
    j = pl.program_id(1)
    q4 = q_ref[0]

    @pl.when(j == 0)
    def _():
        sn = jnp.sum(q4 * kn_ref[0], axis=-1, keepdims=True) * (HEAD_DIM ** -0.5)
        ok = mn_ref[0][:, 0:1] > 0.0
        m_scr[...] = jnp.broadcast_to(jnp.where(ok, sn, NEG), (N_HEADS, 128))
        l_scr[...] = jnp.broadcast_to(jnp.where(ok, 1.0, 0.0), (N_HEADS, 128))
        acc_scr[...] = jnp.where(ok, vn_ref[0], 0.0)

    mk = msk_ref[0] > 0.0
    qb = q4.astype(BF16)
    s = jnp.concatenate(
        [_dot_nt(qb[h:h + 1], jnp.concatenate([r[0, :, h, :] for r in k_refs], axis=0).astype(BF16))
         for h in range(N_HEADS)], axis=0)
    s = jnp.where(mk, s * (HEAD_DIM ** -0.5), NEG)
    m_old = m_scr[:, 0:1]
    m_new = jnp.maximum(m_old, jnp.max(s, axis=-1, keepdims=True))
    alpha = jnp.exp(m_old - m_new)
    p = jnp.where(mk, jnp.exp(s - m_new), 0.0)
    l_new = alpha * l_scr[:, 0:1] + jnp.sum(p, axis=-1, keepdims=True)
    pb = p.astype(BF16)
    pv = jnp.concatenate(
        [_dot(pb[h:h + 1], jnp.concatenate([r[0, :, h, :] for r in v_refs], axis=0).astype(BF16))
         for h in range(N_HEADS)], axis=0)
    acc = alpha * acc_scr[...] + pv
    m_scr[...] = jnp.broadcast_to(m_new, (N_HEADS, 128))
    l_scr[...] = jnp.broadcast_to(l_new, (N_HEADS, 128))
    acc_scr[...] = acc

    @pl.when(j == pl.num_programs(1) - 1)
    def _():
        o_ref[0] = acc / l_new


def _dsa_attn(q, k_new, v_new, mask, k_pool, v_pool, page_table):
    b, n_pages = page_table.shape
    g_count = min(PAGES_PER_STEP, n_pages)
    past = n_pages * PAGE_SIZE
    hd = lambda a: a.reshape(b, N_HEADS, HEAD_DIM)
    m_new = jnp.broadcast_to(mask[:, past:past + 1], (b, 128)).reshape(b, 1, 128)
    m_past = mask[:, :past].reshape(b, 1, past)
    hspec = pl.BlockSpec((1, N_HEADS, HEAD_DIM), lambda i, j, pt: (i, 0, 0))
    grid_spec = pltpu.PrefetchScalarGridSpec(
        num_scalar_prefetch=1,
        grid=(b, n_pages // g_count),
        in_specs=[hspec, hspec, hspec,
                  pl.BlockSpec((1, 1, 128), lambda i, j, pt: (i, 0, 0)),
                  pl.BlockSpec((1, 1, g_count * PAGE_SIZE), lambda i, j, pt: (i, 0, j))]
                 + _page_specs((1, PAGE_SIZE, N_HEADS, HEAD_DIM), g_count) * 2,
        out_specs=hspec,
        scratch_shapes=[pltpu.VMEM((N_HEADS, 128), F32), pltpu.VMEM((N_HEADS, 128), F32),
                        pltpu.VMEM((N_HEADS, HEAD_DIM), F32)])
    out = pl.pallas_call(
        functools.partial(_dsa_attn_kernel, g_count=g_count),
        grid_spec=grid_spec,
        out_shape=jax.ShapeDtypeStruct((b, N_HEADS, HEAD_DIM), F32),
        compiler_params=_cparams(("arbitrary", "arbitrary")),
        name="dsa_attn",
    )(page_table, hd(q), hd(k_new), hd(v_new), m_new, m_past, *([k_pool] * g_count), *([v_pool] * g_count))
    return out.reshape(b, BR_W)


def _kbar_kernel(pt_ref, *refs, g_count):
    page_refs = refs[:g_count]
    o_ref = refs[g_count]
    for g2 in range(g_count // 2):
        s = jnp.sum(page_refs[2 * g2][0], axis=0) + jnp.sum(page_refs[2 * g2 + 1][0], axis=0)
        o_ref[0, g2] = s * (1.0 / MOBA_BLOCK)


def _moba_kbar(k_pool, page_table):
    b, n_pages = page_table.shape
    g_count = min(PAGES_PER_STEP, n_pages)
    grid_spec = pltpu.PrefetchScalarGridSpec(
        num_scalar_prefetch=1,
        grid=(b, n_pages // g_count),
        in_specs=_page_specs((1, PAGE_SIZE, N_HEADS, HEAD_DIM), g_count),
        out_specs=pl.BlockSpec((1, g_count // 2, N_HEADS, HEAD_DIM), lambda i, j, pt: (i, j, 0, 0)))
    out = pl.pallas_call(
        functools.partial(_kbar_kernel, g_count=g_count),
        grid_spec=grid_spec,
        out_shape=jax.ShapeDtypeStruct((b, n_pages // 2, N_HEADS, HEAD_DIM), F32),
        compiler_params=_cparams(("arbitrary", "arbitrary")),
        name="moba_kbar",
    )(page_table, *([k_pool] * g_count))
    return out.reshape(b, n_pages // 2, BR_W)


def _moba_gate_kernel(q_ref, kbar_ref, o_ref, *, n_past, nsel):
    qbd, _ = _head_rows(q_ref[0])
    g = _mm3_nt(qbd, kbar_ref[0])
    col = lax.broadcasted_iota(I32, (N_HEADS, 128), 1)
    g = jnp.where(col < n_past, g, -jnp.inf)
    out = jnp.full((N_HEADS, 128), -1, I32)
    for i in range(nsel):
        mx = jnp.max(g, axis=-1, keepdims=True)
        idx = jnp.min(jnp.where(g == mx, col, jnp.int32(1 << 20)), axis=-1, keepdims=True)
        ok = idx < n_past
        out = jnp.where(col == i, jnp.where(ok, idx, -1), out)
        g = jnp.where(col == idx, -jnp.inf, g)
    o_ref[0] = out


def _moba_gate(q, kbar, nsel):
    b, n_past, _ = kbar.shape
    kb = jnp.pad(kbar, ((0, 0), (0, 128 - n_past), (0, 0)))
    out = pl.pallas_call(
        functools.partial(_moba_gate_kernel, n_past=n_past, nsel=nsel),
        grid=(b,),
        in_specs=[pl.BlockSpec((1, 1, BR_W), lambda i: (i, 0, 0)),
                  pl.BlockSpec((1, 128, BR_W), lambda i: (i, 0, 0))],
        out_specs=pl.BlockSpec((1, N_HEADS, 128), lambda i: (i, 0, 0)),
        out_shape=jax.ShapeDtypeStruct((b, N_HEADS, 128), I32),
        compiler_params=_cparams(("arbitrary",)),
        name="moba_gate",
    )(q.reshape(b, 1, BR_W), kb)
    return out[:, :, :nsel]


def _moba_attn_kernel(sel_ref, pt_ref, q_ref, kn_ref, vn_ref, *refs, nsel):
    n_pg = 2 * nsel
    k_refs = refs[:n_pg]
    v_refs = refs[n_pg:2 * n_pg]
    o_ref = refs[2 * n_pg]
    bi = pl.program_id(0)
    h = pl.program_id(1)
    q4 = q_ref[0]
    qb = q4.astype(BF16)
    sn = jnp.sum(q4 * kn_ref[0], axis=-1, keepdims=True) * (HEAD_DIM ** -0.5)
    s = jnp.concatenate(
        [_dot_nt(qb[g:g + 1], jnp.concatenate([r[0, :, g, :] for r in k_refs], axis=0).astype(BF16))
         for g in range(N_HEADS)], axis=0) * (HEAD_DIM ** -0.5)
    blk = lax.broadcasted_iota(I32, s.shape, 1) // MOBA_BLOCK
    mk = jnp.zeros(s.shape, jnp.bool_)
    for i in range(nsel):
        mk = mk | (blk == jnp.where(sel_ref[(bi * N_HEADS + h) * nsel + i] >= 0, i, -1))
    s = jnp.where(mk, s, NEG)
    m = jnp.maximum(jnp.max(s, axis=-1, keepdims=True), sn)
    p = jnp.where(mk, jnp.exp(s - m), 0.0)
    pn = jnp.exp(sn - m)
    pb = p.astype(BF16)
    pv = jnp.concatenate(
        [_dot(pb[g:g + 1], jnp.concatenate([r[0, :, g, :] for r in v_refs], axis=0).astype(BF16))
         for g in range(N_HEADS)], axis=0)
    out = (pv + pn * vn_ref[0]) / (jnp.sum(p, axis=-1, keepdims=True) + pn)

    @pl.when(h == 0)
    def _():
        o_ref[0] = jnp.zeros((N_HEADS, HEAD_DIM), F32)

    row = lax.broadcasted_iota(I32, (N_HEADS, HEAD_DIM), 0)
    o_ref[0] = jnp.where(row == h, out, o_ref[0])


def _moba_attn(q, k_new, v_new, sel, k_pool, v_pool, page_table):
    b, n_pages = page_table.shape
    nsel = sel.shape[-1]
    hd = lambda a: a.reshape(b, N_HEADS, HEAD_DIM)

    def pspec(i, half):
        def imap(bi, h, sel_ref, pt):
            blk = jnp.maximum(sel_ref[(bi * N_HEADS + h) * nsel + i], 0)
            return (pt[bi, 2 * blk + half], 0, 0, 0)
        return pl.BlockSpec((1, PAGE_SIZE, N_HEADS, HEAD_DIM), imap)

    pages = [pspec(i, half) for i in range(nsel) for half in range(2)]
    hspec = pl.BlockSpec((1, N_HEADS, HEAD_DIM), lambda bi, h, s, pt: (bi, 0, 0))
    grid_spec = pltpu.PrefetchScalarGridSpec(
        num_scalar_prefetch=2,
        grid=(b, N_HEADS),
        in_specs=[hspec, hspec, hspec] + pages + pages,
        out_specs=hspec)
    out = pl.pallas_call(
        functools.partial(_moba_attn_kernel, nsel=nsel),
        grid_spec=grid_spec,
        out_shape=jax.ShapeDtypeStruct((b, N_HEADS, HEAD_DIM), F32),
        compiler_params=_cparams(("arbitrary", "arbitrary")),
        name="moba_attn",
    )(sel.reshape(-1), page_table, hd(q), hd(k_new), hd(v_new),
      *([k_pool] * (2 * nsel)), *([v_pool] * (2 * nsel)))
    return out.reshape(b, BR_W)


def _sample_layer(x, c, p, a_shift, a_wkv, b_ret, ck_pool, cv_pool, cki_pool, dk_pool, dv_pool, page_table):
    b = x.shape[0]
    n_pages = page_table.shape[1]
    past = n_pages * PAGE_SIZE
    assert x.shape[1] == 1 and past % MOBA_BLOCK == 0
    pos = jnp.full((b,), past, I32)
    tab_ret = _rope_tables(pos, HEAD_DIM, RET_THETA, BR_W)
    tab_std = _rope_tables(pos, ROPE_DIMS, ROPE_THETA, BR_W)
    w_hi, w_lo = _regroup_w_in(p['w_in'])
    mod = _ada(c, p['w_ada'], p['b_ada']).reshape(1, b, 3 * D_MODEL)
    xr = x.reshape(1, b, D_MODEL)
    outs = _layer_in(xr, mod, p['g_pre'], w_hi, w_lo, tab_ret, tab_std, b)
    (ua, gates, qb, kb, vb, qc, kc, vc, qi, ki4, wi, qd, kd, vd) = [o[0] for o in outs[:14]]
    r, w, k2, v, kk, bb, bonus = _rwkv_pre_rows(ua, a_shift, p)
    ya, wkv = _rwkv_step(r, w, k2, v, kk, bb, a_wkv)
    yb, ret = _retention_step(qb, kb, vb, b_ret, p['b_gn_w'], p['b_gn_b'])
    ki = ki4[:, :IDX_DIM]
    sc_past, sc_new = _dsa_scores(qi, wi[:, :IDX_HEADS], ki, cki_pool, page_table)
    total = past + 1
    mask = _topk_rows(jnp.concatenate([sc_past, sc_new], axis=1), min(DSA_TOPK_MAX, total // 4))
    yc = _dsa_attn(qc, kc, vc, mask, ck_pool, cv_pool, page_table)
    n_past_blocks = past // MOBA_BLOCK
    nsel = min(MOBA_TOPK, n_past_blocks)
    if nsel > 0:
        sel = _moba_gate(qd, _moba_kbar(dk_pool, page_table), nsel)
        yd = _moba_attn(qd, kd, vd, sel, dk_pool, dv_pool, page_table)
    else:
        yd = vd
    row = lambda a: a.reshape(1, b, -1)
    x_new = _layer_out(xr, mod, p, row(ya), row(bonus), row(yb), row(yc), row(yd), gates.reshape(1, b, -1), b)
    heads = lambda a: a.reshape(b, 1, N_HEADS, HEAD_DIM)
    new_state = (ua, wkv, ret, heads(kc), heads(vc), ki.reshape(b, 1, IDX_DIM), heads(kd), heads(vd))
    return x_new.reshape(b, 1, D_MODEL), new_state


def _prompt_layer(x, c, p):
    b, t, _ = x.shape
    pos = jnp.arange(t)
    tab_ret = _rope_tables(pos, HEAD_DIM, RET_THETA, BR_W)
    tab_std = _rope_tables(pos, ROPE_DIMS, ROPE_THETA, BR_W)
    w_hi, w_lo = _regroup_w_in(p['w_in'])
    mod = _ada(c, p['w_ada'], p['b_ada']).reshape(b, 1, 3 * D_MODEL)
    tm = MOBA_BLOCK
    assert t % tm == 0
    (ua, gates, qb, kb, vb, qc, kc, vc, qi, ki4, wi, qd, kd, vd, kcb, vcb, kix, kdb, vdb, kbar) = _layer_in(
        x, mod, p['g_pre'], w_hi, w_lo, tab_ret, tab_std, tm)
    r, w, k2, v, kk, bb, bonus = _rwkv_pre_prompt(ua, jnp.zeros((b, A_SHIFT_W), F32), p, tm)
    ya, wkv = _rwkv_scan(r, w, k2, v, kk, bb, jnp.zeros((b, HEAD_DIM, BR_W), F32), min(64, t))
    yb, ret = _retention_prompt(qb, kb, vb, jnp.zeros((b, N_HEADS, HEAD_DIM, HEAD_DIM), F32),
                                p['b_gn_w'], p['b_gn_b'], min(256, t))
    yc = _dsa_prompt(qc, qi, wi, kcb, vcb, kix)
    yd = _moba_prompt(qd, kbar.reshape(b, t // tm, BR_W), kdb, vdb)
    x_new = _layer_out(x, mod, p, ya, bonus, yb, yc, yd, gates, tm)
    heads = lambda a: a.reshape(b, t, N_HEADS, HEAD_DIM)
    new_state = (ua[:, -1], _wkv_from_scan_layout(wkv), ret, heads(kc), heads(vc), ki4[..., :IDX_DIM],
                 heads(kd), heads(vd))
    return x_new, new_state


_PARAM_NAMES = ('w_ada', 'b_ada', 'g_pre', 'g_post', 'w_in', 'a_mu', 'a_w0', 'a_w2', 'a_a0', 'a_a2', 'a_kk', 'a_ka',
                'a_rk', 'a_ln_w', 'a_ln_b', 'b_gn_w', 'b_gn_b', 'w_branch', 'w_merge', 'w_out')


def kernel(x_prompt, x_sample, c_prompt, c_sample, state_a_shift, state_a_wkv, state_b_ret, cache_c_k, cache_c_v,
           cache_c_kidx, cache_d_k, cache_d_v, page_table, w_ada, b_ada, g_pre, g_post, w_in, a_mu, a_w0, a_w2,
           a_a0, a_a2, a_kk, a_ka, a_rk, a_ln_w, a_ln_b, b_gn_w, b_gn_b, w_branch, w_merge, w_out):
    stacked = dict(zip(_PARAM_NAMES, (w_ada, b_ada, g_pre, g_post, w_in, a_mu, a_w0, a_w2, a_a0, a_a2, a_kk, a_ka,
                                      a_rk, a_ln_w, a_ln_b, b_gn_w, b_gn_b, w_branch, w_merge, w_out)))
    depth = w_in.shape[0]
    n_pool = cache_c_k.shape[1]
    fold = lambda a: a.reshape((depth * n_pool,) + a.shape[2:])
    ck, cv, cki, dk, dv = (fold(a) for a in (cache_c_k, cache_c_v, cache_c_kidx, cache_d_k, cache_d_v))
    xp, xs = x_prompt, x_sample
    p_new, s_new = [], []
    for l in range(depth):
        p = {name: val[l] for name, val in stacked.items()}
        xp, st_p = _prompt_layer(xp, c_prompt, p)
        xs, st_s = _sample_layer(xs, c_sample, p, state_a_shift[l], state_a_wkv[l], state_b_ret[l],
                                 ck, cv, cki, dk, dv, page_table + l * n_pool)
        p_new.append(st_p)
        s_new.append(st_s)
    stack = lambda states, i: jnp.stack([s[i] for s in states])
    return ((xp, xs) + tuple(stack(p_new, i) for i in range(8)) + tuple(stack(s_new, i) for i in range(8)))
```

```python
import functools

import jax
import jax.numpy as jnp
import numpy as np
from jax import lax
from jax.experimental import pallas as pl
from jax.experimental.pallas import tpu as pltpu

F32 = jnp.float32
BF16 = jnp.bfloat16
I32 = jnp.int32

D_MODEL = 1024
PAGE_SIZE = 128
BR_W = 256
HEAD_DIM = 64
N_HEADS = 4
LORA_W = 64
LORA_A = 64
A_SHIFT_W = 3 * BR_W + LORA_W + LORA_A
ROPE_THETA = 500000.0
ROPE_DIMS = HEAD_DIM // 4
RET_THETA = 10000.0
IDX_HEADS = 4
IDX_DIM = 64
DSA_TOPK_MAX = 256
MOBA_BLOCK = 256
MOBA_TOPK = 3
RMS_EPS = 1e-6
RWKV_GN_EPS = 64e-5
RET_GN_EPS = 1e-5
NEG = -1e30
VMEM_LIMIT_BYTES = 56 * 1024 * 1024
RWKV_GROUP = 8

_C = {}
_off = 0
for _name, _n in (('a_r', 256), ('a_k', 256), ('a_v', 256), ('a_wl', 64), ('a_al', 64), ('a_g', 256),
                  ('b_q', 256), ('b_k', 256), ('b_v', 256), ('b_g', 256),
                  ('c_q', 256), ('c_k', 256), ('c_v', 256), ('c_qi', 256), ('c_ki', 64),
                  ('c_wi', 4), ('c_g', 256),
                  ('d_q', 256), ('d_k', 256), ('d_v', 256), ('d_g', 256)):
    _C[_name] = (_off, _off + _n)
    _off += _n

W_A, W_G, W_B, W_C, W_I, W_D = 896, 1024, 768, 768, 640, 768
OFF_A = 0
OFF_G = OFF_A + W_A
OFF_B = OFF_G + W_G
OFF_C = OFF_B + W_B
OFF_I = OFF_C + W_C
OFF_D = OFF_I + W_I
W_ALL = OFF_D + W_D


def _cparams(sem):
    return pltpu.CompilerParams(dimension_semantics=sem, vmem_limit_bytes=VMEM_LIMIT_BYTES)


def _split(x):
    hi = x.astype(BF16)
    lo = (x - hi.astype(F32)).astype(BF16)
    return hi, lo


def _dot(a, b):
    return jnp.dot(a, b, preferred_element_type=F32)


def _dot_nt(a, b):
    return lax.dot_general(a, b, (((1,), (1,)), ((), ())), preferred_element_type=F32)


def _mm1(a, b):
    return _dot(a.astype(BF16), b.astype(BF16))


def _mm3(a, b_hi, b_lo):
    a_hi, a_lo = _split(a)
    return _dot(a_hi, b_hi) + (_dot(a_lo, b_hi) + _dot(a_hi, b_lo))


def _mm3_nt(a, b):
    a_hi, a_lo = _split(a)
    b_hi, b_lo = _split(b)
    return _dot_nt(a_hi, b_hi) + (_dot_nt(a_lo, b_hi) + _dot_nt(a_hi, b_lo))


def _head_ones():
    r = lax.broadcasted_iota(I32, (BR_W, BR_W), 0) // HEAD_DIM
    c = lax.broadcasted_iota(I32, (BR_W, BR_W), 1) // HEAD_DIM
    return jnp.where(r == c, 1.0, 0.0).astype(BF16)


def _headsum(x, bd):
    hi, lo = _split(x)
    return _dot(hi, bd) + _dot(lo, bd)


def _head_mask(h, shape):
    c = lax.broadcasted_iota(I32, shape, len(shape) - 1) // HEAD_DIM
    return c == h


def _ada_kernel(c_ref, w_ref, b_ref, o_ref):
    w = w_ref[...]
    w_hi, w_lo = _split(w)
    o_ref[...] = _mm3(c_ref[...], w_hi, w_lo) + b_ref[...]


def _ada(c, w_ada, b_ada):
    bc = c.shape[0]
    n = w_ada.shape[1]
    tn = 1024
    return pl.pallas_call(
        _ada_kernel,
        grid=(n // tn,),
        in_specs=[pl.BlockSpec((bc, D_MODEL), lambda j: (0, 0)),
                  pl.BlockSpec((D_MODEL, tn), lambda j: (0, j)),
                  pl.BlockSpec((1, tn), lambda j: (0, j))],
        out_specs=pl.BlockSpec((bc, tn), lambda j: (0, j)),
        out_shape=jax.ShapeDtypeStruct((bc, n), F32),
        compiler_params=_cparams(("arbitrary",)),
        name="ada",
    )(c, w_ada, b_ada.reshape(1, n))


def _rope_tables(pos, rot_dims, theta, width):
    half = rot_dims // 2
    inv = jnp.power(jnp.float32(theta), -jnp.arange(half, dtype=jnp.float32) / half)
    ang = pos.astype(jnp.float32)[:, None] * inv[None, :]
    cos = jnp.cos(ang)
    sin = jnp.sin(ang)
    t = pos.shape[0]
    one = jnp.ones((t, HEAD_DIM - rot_dims), F32)
    zero = jnp.zeros((t, HEAD_DIM - rot_dims), F32)
    zh = jnp.zeros((t, half), F32)
    cos_h = jnp.concatenate([cos, cos, one], axis=1)
    up_h = jnp.concatenate([-sin, zh, zero], axis=1)
    dn_h = jnp.concatenate([zh, sin, zero], axis=1)
    reps = width // HEAD_DIM
    return jnp.stack([jnp.tile(cos_h, (1, reps)), jnp.tile(up_h, (1, reps)), jnp.tile(dn_h, (1, reps))])


def _rope_apply(x, tab_ref, lo, hi, half):
    n = hi - lo
    cos = tab_ref[0, :, lo:hi]
    up = tab_ref[1, :, lo:hi]
    dn = tab_ref[2, :, lo:hi]
    return x * cos + pltpu.roll(x, n - half, 1) * up + pltpu.roll(x, half, 1) * dn


def _indexer_key_operand(ki4):
    k_hi, k_lo = _split(ki4)
    lane = lax.broadcasted_iota(I32, ki4.shape, 1)
    return jnp.where(lane < 2 * IDX_DIM, k_hi, jnp.where(lane < 3 * IDX_DIM, k_lo, jnp.zeros_like(k_lo)))


def _layer_in_kernel(x_ref, mod_ref, g_ref, wh_ref, wl_ref, tr_ref, ts_ref,
                     ua_ref, gt_ref, qb_ref, kb_ref, vb_ref, qc_ref, kc_ref, vc_ref,
                     qi_ref, ki_ref, wi_ref, qd_ref, kd_ref, vd_ref, *extra_refs):
    x = x_ref[0]
    y = x * lax.rsqrt(jnp.mean(x * x, axis=-1, keepdims=True) + RMS_EPS) * g_ref[...]
    shift = mod_ref[0, :, 0:D_MODEL]
    scale = mod_ref[0, :, D_MODEL:2 * D_MODEL]
    h = y * (1.0 + scale) + shift
    h_hi, h_lo = _split(h)

    def proj3(lo, hi):
        b_hi = wh_ref[:, lo:hi]
        b_lo = wl_ref[:, lo:hi]
        return _dot(h_hi, b_hi) + (_dot(h_lo, b_hi) + _dot(h_hi, b_lo))

    def proj1(lo, hi):
        return _dot(h_hi, wh_ref[:, lo:hi])

    ua_ref[0] = proj3(OFF_A, OFF_A + W_A)
    gt_ref[0] = proj1(OFF_G, OFF_G + W_G)
    ub = proj1(OFF_B, OFF_B + W_B)
    qb_ref[0] = _rope_apply(ub[:, 0:256], tr_ref, 0, 256, HEAD_DIM // 2)
    kb_ref[0] = _rope_apply(ub[:, 256:512], tr_ref, 0, 256, HEAD_DIM // 2)
    vb_ref[0] = ub[:, 512:768]
    uc = proj1(OFF_C, OFF_C + W_C)
    qc_ref[0] = _rope_apply(uc[:, 0:256], ts_ref, 0, 256, ROPE_DIMS // 2)
    kc = _rope_apply(uc[:, 256:512], ts_ref, 0, 256, ROPE_DIMS // 2)
    kc_ref[0] = kc
    vc = uc[:, 512:768]
    vc_ref[0] = vc
    ui = proj3(OFF_I, OFF_I + W_I)
    qi_ref[0] = _rope_apply(ui[:, 0:256], ts_ref, 0, 256, ROPE_DIMS // 2)
    ki4 = _rope_apply(ui[:, 256:512], ts_ref, 0, 256, ROPE_DIMS // 2)
    ki_ref[0] = ki4
    wi_ref[0] = ui[:, 512:640]
    uqk = proj3(OFF_D, OFF_D + 512)
    qd_ref[0] = _rope_apply(uqk[:, 0:256], ts_ref, 0, 256, ROPE_DIMS // 2)
    kd = _rope_apply(uqk[:, 256:512], ts_ref, 0, 256, ROPE_DIMS // 2)
    kd_ref[0] = kd
    vd = proj1(OFF_D + 512, OFF_D + W_D)
    vd_ref[0] = vd
    if extra_refs:
        kcb_ref, vct_ref, kix_ref, kdb_ref, vdt_ref, kbar_ref = extra_refs
        kcb_ref[0] = kc.astype(BF16)
        vct_ref[0, 0] = vc.T.astype(BF16)
        kix_ref[0] = _indexer_key_operand(ki4)
        kdb_ref[0] = kd.astype(BF16)
        vdt_ref[0, 0] = vd.T.astype(BF16)
        kbar_ref[0, 0] = jnp.sum(kd, axis=0, keepdims=True) * (1.0 / MOBA_BLOCK)


def _regroup_w_in(w_in):
    def cols(name):
        lo, hi = _C[name]
        return w_in[:, lo:hi]
    ki4 = jnp.tile(cols('c_ki'), (1, 4))
    wi_pad = jnp.pad(cols('c_wi'), ((0, 0), (0, 124)))
    w = jnp.concatenate([
        cols('a_r'), cols('a_k'), cols('a_v'), cols('a_wl'), cols('a_al'),
        cols('a_g'), cols('b_g'), cols('c_g'), cols('d_g'),
        cols('b_q'), cols('b_k'), cols('b_v'),
        cols('c_q'), cols('c_k'), cols('c_v'),
        cols('c_qi'), ki4, wi_pad,
        cols('d_q'), cols('d_k'), cols('d_v')], axis=1)
    return _split(w)


def _layer_in(x, mod, g_pre, w_hi, w_lo, tab_ret, tab_std, tm, attention_operands=False):
    b, t, _ = x.shape
    r = mod.shape[1]
    if r == 1:
        mod_spec = pl.BlockSpec((1, 1, 3 * D_MODEL), lambda i, j: (i, 0, 0))
    else:
        mod_spec = pl.BlockSpec((1, tm, 3 * D_MODEL), lambda i, j: (i, j, 0))
    widths = (W_A, W_G, 256, 256, 256, 256, 256, 256, 256, 256, 128, 256, 256, 256)
    seq = lambda w: pl.BlockSpec((1, tm, w), lambda i, j: (i, j, 0))
    out_shape = tuple(jax.ShapeDtypeStruct((b, t, w), F32) for w in widths)
    out_specs = tuple(seq(w) for w in widths)
    if attention_operands:
        rows = jax.ShapeDtypeStruct((b, t, BR_W), BF16)
        cols = jax.ShapeDtypeStruct((b, t // tm, BR_W, tm), BF16)
        col_spec = pl.BlockSpec((1, 1, BR_W, tm), lambda i, j: (i, j, 0, 0))
        out_shape += (rows, cols, rows, rows, cols, jax.ShapeDtypeStruct((b, t // tm, 1, BR_W), F32))
        out_specs += (seq(BR_W), col_spec, seq(BR_W), seq(BR_W), col_spec,
                      pl.BlockSpec((1, 1, 1, BR_W), lambda i, j: (i, j, 0, 0)))
    return pl.pallas_call(
        _layer_in_kernel,
        grid=(b, t // tm),
        in_specs=[pl.BlockSpec((1, tm, D_MODEL), lambda i, j: (i, j, 0)),
                  mod_spec,
                  pl.BlockSpec((1, D_MODEL), lambda i, j: (0, 0)),
                  pl.BlockSpec((D_MODEL, W_ALL), lambda i, j: (0, 0), pipeline_mode=pl.Buffered(1)),
                  pl.BlockSpec((D_MODEL, W_ALL), lambda i, j: (0, 0), pipeline_mode=pl.Buffered(1)),
                  pl.BlockSpec((3, tm, 256), lambda i, j: (0, j, 0)),
                  pl.BlockSpec((3, tm, 256), lambda i, j: (0, j, 0))],
        out_specs=out_specs,
        out_shape=out_shape,
        compiler_params=_cparams(("arbitrary", "arbitrary")),
        name="layer_in",
    )(x, mod, g_pre.reshape(1, D_MODEL), w_hi, w_lo, tab_ret, tab_std)


def _rwkv_pre_math(ua, prev, mu, w0, w2h, w2l, a0, a2h, a2l, kkp, ka, rk, bd):
    xs = ua + (prev - ua) * mu
    r = xs[:, 0:256]
    k = xs[:, 256:512]
    v = xs[:, 512:768]
    wl = xs[:, 768:832]
    al = xs[:, 832:896]
    zw = w0 + _mm3(jnp.tanh(wl), w2h, w2l)
    w_log = -jax.nn.softplus(-zw) - 0.5
    decay = jnp.exp(-jnp.exp(w_log))
    a = jax.nn.sigmoid(a0 + _mm3(al, a2h, a2l))
    kq = k * kkp
    kk = kq * lax.rsqrt(_headsum(kq * kq, bd) + 1e-12)
    k2 = k * (1.0 + (a - 1.0) * ka)
    bonus = _headsum(r * k2 * rk, bd) * v
    return r, decay, k2, v, kk, kk * a, bonus


def _rwkv_pre_shift_kernel(ua_ref, up_ref, p0_ref, mu_ref, w0_ref, w2h_ref, w2l_ref, a0_ref, a2h_ref, a2l_ref,
                           kkp_ref, ka_ref, rk_ref, r_ref, w_ref, k_ref, v_ref, kk_ref, b_ref, bo_ref):
    j = pl.program_id(1)
    ua = ua_ref[0]
    tm = ua.shape[0]
    first = jnp.where(j == 0, p0_ref[0], up_ref[0, 7:8, :])
    row = lax.broadcasted_iota(I32, ua.shape, 0)
    prev = jnp.where(row == 0, first, pltpu.roll(ua, 1, 0))
    outs = _rwkv_pre_math(ua, prev, mu_ref[...], w0_ref[...], w2h_ref[...], w2l_ref[...], a0_ref[...],
                          a2h_ref[...], a2l_ref[...], kkp_ref[...], ka_ref[...], rk_ref[...], _head_ones())
    for o_ref, o in zip((r_ref, w_ref, k_ref, v_ref, kk_ref, b_ref, bo_ref), outs):
        o_ref[0] = o


def _rwkv_pre_rows_kernel(ua_ref, pv_ref, mu_ref, w0_ref, w2h_ref, w2l_ref, a0_ref, a2h_ref, a2l_ref,
                          kkp_ref, ka_ref, rk_ref, r_ref, w_ref, k_ref, v_ref, kk_ref, b_ref, bo_ref):
    outs = _rwkv_pre_math(ua_ref[...], pv_ref[...], mu_ref[...], w0_ref[...], w2h_ref[...], w2l_ref[...],
                          a0_ref[...], a2h_ref[...], a2l_ref[...], kkp_ref[...], ka_ref[...], rk_ref[...],
                          _head_ones())
    for o_ref, o in zip((r_ref, w_ref, k_ref, v_ref, kk_ref, b_ref, bo_ref), outs):
        o_ref[...] = o


def _rwkv_params(p):
    w2h, w2l = _split(p['a_w2'])
    a2h, a2l = _split(p['a_a2'])
    row = lambda v: v.reshape(1, -1)
    return (row(p['a_mu']), row(p['a_w0']), w2h, w2l, row(p['a_a0']), a2h, a2l,
            row(p['a_kk']), row(p['a_ka']), row(p['a_rk']))


def _rwkv_pre_prompt(ua, prev0, p, tm):
    b, t, _ = ua.shape
    prm = _rwkv_params(p)
    full = lambda a: pl.BlockSpec(a.shape, lambda i, j: (0,) * a.ndim)
    blk8 = tm // 8
    return pl.pallas_call(
        _rwkv_pre_shift_kernel,
        grid=(b, t // tm),
        in_specs=[pl.BlockSpec((1, tm, A_SHIFT_W), lambda i, j: (i, j, 0)),
                  pl.BlockSpec((1, 8, A_SHIFT_W), lambda i, j: (i, jnp.maximum(j * blk8 - 1, 0), 0)),
                  pl.BlockSpec((1, 1, A_SHIFT_W), lambda i, j: (i, 0, 0))] + [full(a) for a in prm],
        out_specs=tuple(pl.BlockSpec((1, tm, BR_W), lambda i, j: (i, j, 0)) for _ in range(7)),
        out_shape=tuple(jax.ShapeDtypeStruct((b, t, BR_W), F32) for _ in range(7)),
        compiler_params=_cparams(("arbitrary", "arbitrary")),
        name="rwkv_pre",
    )(ua, ua, prev0.reshape(b, 1, A_SHIFT_W), *prm)


def _rwkv_pre_rows(ua, prev, p):
    n = ua.shape[0]
    prm = _rwkv_params(p)
    full = lambda a: pl.BlockSpec(a.shape, lambda: (0,) * a.ndim)
    return pl.pallas_call(
        _rwkv_pre_rows_kernel,
        in_specs=[full(ua), full(prev)] + [full(a) for a in prm],
        out_specs=tuple(pl.BlockSpec((n, BR_W), lambda: (0, 0)) for _ in range(7)),
        out_shape=tuple(jax.ShapeDtypeStruct((n, BR_W), F32) for _ in range(7)),
        name="rwkv_pre_rows",
    )(ua, prev, *prm)


def _rwkv_scan_kernel(r_ref, w_ref, k_ref, v_ref, kk_ref, b_ref, s0_ref, y_ref, sout_ref, s_scr, *, tc):
    c = pl.program_id(0)

    @pl.when(c == 0)
    def _():
        s_scr[...] = s0_ref[...]

    nb = s_scr.shape[0]
    bd = _head_ones()
    vi = lax.broadcasted_iota(I32, (HEAD_DIM, BR_W), 0)
    ci = lax.broadcasted_iota(I32, (HEAD_DIM, BR_W), 1)
    diag = ((ci % HEAD_DIM) == vi)[None]

    groups = [(g, min(g + RWKV_GROUP, nb)) for g in range(0, nb, RWKV_GROUP)]

    def step(t, carry):
        for lo, hi in groups:
            n = hi - lo
            row = lambda ref: ref[lo:hi, pl.ds(t, 1), :]
            s = s_scr[lo:hi]
            p = (s * row(kk_ref)).astype(BF16)
            vd = jnp.where(diag, row(v_ref), 0.0).astype(BF16)
            lhs = jnp.concatenate([p, vd], axis=1).reshape(n * 2 * HEAD_DIM, BR_W)
            res = _dot(lhs, bd).reshape(n, 2 * HEAD_DIM, BR_W)
            s = s * row(w_ref) - res[:, :HEAD_DIM] * row(b_ref) + res[:, HEAD_DIM:] * row(k_ref)
            s_scr[lo:hi] = s
            q = (s * row(r_ref)).astype(BF16).reshape(n * HEAD_DIM, BR_W)
            yb = _dot(q, bd).reshape(n, HEAD_DIM, BR_W)
            y_ref[lo:hi, pl.ds(t, 1), :] = jnp.sum(jnp.where(diag, yb, 0.0), axis=1, keepdims=True)
        return carry

    lax.fori_loop(0, tc, step, 0, unroll=4)

    @pl.when(c == pl.num_programs(0) - 1)
    def _():
        sout_ref[...] = s_scr[...]


def _rwkv_scan(r, w, k, v, kk, bb, s0, tc):
    b, t, _ = r.shape
    seq = pl.BlockSpec((b, tc, BR_W), lambda c: (0, c, 0))
    st = pl.BlockSpec((b, HEAD_DIM, BR_W), lambda c: (0, 0, 0))
    return pl.pallas_call(
        functools.partial(_rwkv_scan_kernel, tc=tc),
        grid=(t // tc,),
        in_specs=[seq] * 6 + [st],
        out_specs=(seq, st),
        out_shape=(jax.ShapeDtypeStruct((b, t, BR_W), F32), jax.ShapeDtypeStruct((b, HEAD_DIM, BR_W), F32)),
        scratch_shapes=[pltpu.VMEM((b, HEAD_DIM, BR_W), F32)],
        compiler_params=_cparams(("arbitrary",)),
        name="rwkv_scan",
    )(r, w, k, v, kk, bb, s0)


def _wkv_to_scan_layout(s):
    b = s.shape[0]
    return jnp.transpose(s, (0, 2, 1, 3)).reshape(b, HEAD_DIM, BR_W)


def _wkv_from_scan_layout(s):
    b = s.shape[0]
    return jnp.transpose(s.reshape(b, HEAD_DIM, N_HEADS, HEAD_DIM), (0, 2, 1, 3))


def _rwkv_step_kernel(r_ref, w_ref, k_ref, vc_ref, kk_ref, b_ref, s_ref, y_ref, so_ref):
    s = s_ref[...]
    sk = jnp.sum(s * kk_ref[...], axis=-1, keepdims=True)
    s = s * w_ref[...] - sk * b_ref[...] + vc_ref[...] * k_ref[...]
    so_ref[...] = s
    y_ref[...] = jnp.sum(s * r_ref[...], axis=-1, keepdims=True)


def _rwkv_step(r, w, k, v, kk, bb, s0):
    b = r.shape[0]
    n = b * N_HEADS
    rowf = lambda a: a.reshape(n, 1, HEAD_DIM)
    full = lambda shape: pl.BlockSpec(shape, lambda: (0,) * len(shape))
    y, s = pl.pallas_call(
        _rwkv_step_kernel,
        in_specs=[full((n, 1, HEAD_DIM))] * 3 + [full((n, HEAD_DIM, 1))] + [full((n, 1, HEAD_DIM))] * 2
                 + [full((n, HEAD_DIM, HEAD_DIM))],
        out_specs=(full((n, HEAD_DIM, 1)), full((n, HEAD_DIM, HEAD_DIM))),
        out_shape=(jax.ShapeDtypeStruct((n, HEAD_DIM, 1), F32), jax.ShapeDtypeStruct((n, HEAD_DIM, HEAD_DIM), F32)),
        compiler_params=pltpu.CompilerParams(vmem_limit_bytes=VMEM_LIMIT_BYTES),
        name="rwkv_step",
    )(rowf(r), rowf(w), rowf(k), v.reshape(n, HEAD_DIM, 1), rowf(kk), rowf(bb), s0.reshape(n, HEAD_DIM, HEAD_DIM))
    return y.reshape(b, BR_W), s.reshape(b, N_HEADS, HEAD_DIM, HEAD_DIM)


def _ret_tables(c):
    log_g = jnp.log(1.0 - jnp.power(2.0, -5.0 - jnp.arange(N_HEADS, dtype=jnp.float32)))
    i = jnp.arange(c, dtype=jnp.float32)
    diff = i[:, None] - i[None, :]
    dmat = jnp.where(diff[None] >= 0, jnp.exp(jnp.maximum(diff, 0.0)[None] * log_g[:, None, None]), 0.0)
    dq = jnp.exp((i[:, None] + 1.0) * log_g[None, :])
    dk = jnp.exp((c - 1.0 - i)[:, None] * log_g[None, :])
    ds = jnp.exp(c * log_g)
    lanes = lambda a: jnp.repeat(a, HEAD_DIM, axis=-1)
    return dmat, lanes(dq), lanes(dk), lanes(ds[None, :])


def _ret_kernel(q_ref, k_ref, v_ref, s0_ref, dm_ref, dq_ref, dk_ref, ds_ref, gw_ref, gb_ref,
                y_ref, so_ref, s_scr):
    j = pl.program_id(1)

    @pl.when(j == 0)
    def _():
        s_scr[...] = s0_ref[0]

    q = q_ref[0]
    k = k_ref[0] * (HEAD_DIM ** -0.5)
    v = v_ref[0]
    s = s_scr[...]
    bd = _head_ones()
    qb = q.astype(BF16)
    kb = k.astype(BF16)
    vb = v.astype(BF16)
    y = _dot(qb, s.astype(BF16)) * dq_ref[...]
    for h in range(N_HEADS):
        hm = _head_mask(h, (1, BR_W))
        att = _dot_nt(jnp.where(hm, q, 0.0).astype(BF16), kb) * dm_ref[h]
        y = y + jnp.where(hm, _dot(att.astype(BF16), vb), 0.0)
    kd = (k * dk_ref[...]).T.astype(BF16)
    s_scr[...] = s * ds_ref[...] + bd.astype(F32) * _dot(kd, vb)
    mu = _headsum(y, bd) * (1.0 / HEAD_DIM)
    d = y - mu
    var = _headsum(d * d, bd) * (1.0 / HEAD_DIM)
    y_ref[0] = d * lax.rsqrt(var + RET_GN_EPS) * gw_ref[...] + gb_ref[...]

    @pl.when(j == pl.num_programs(1) - 1)
    def _():
        so_ref[0] = s_scr[...]


def _ret_state_embed(s):
    b = s.shape[0]
    eye = jnp.eye(N_HEADS, dtype=s.dtype)
    return jnp.einsum('bhde,hg->bhdge', s, eye).reshape(b, BR_W, BR_W)


def _ret_state_extract(s):
    b = s.shape[0]
    s4 = s.reshape(b, N_HEADS, HEAD_DIM, N_HEADS, HEAD_DIM)
    return jnp.stack([s4[:, h, :, h, :] for h in range(N_HEADS)], axis=1)


def _retention_prompt(q, k, v, s0, gn_w, gn_b, c):
    b, t, _ = q.shape
    dmat, dq, dk, ds = _ret_tables(c)
    seq = pl.BlockSpec((1, c, BR_W), lambda i, j: (i, j, 0))
    st = pl.BlockSpec((1, BR_W, BR_W), lambda i, j: (i, 0, 0))
    const = lambda a: pl.BlockSpec(a.shape, lambda i, j: (0,) * a.ndim)
    gw = gn_w.reshape(1, BR_W)
    gb = gn_b.reshape(1, BR_W)
    y, s = pl.pallas_call(
        _ret_kernel,
        grid=(b, t // c),
        in_specs=[seq, seq, seq, st, const(dmat), const(dq), const(dk), const(ds), const(gw), const(gb)],
        out_specs=(seq, st),
        out_shape=(jax.ShapeDtypeStruct((b, t, BR_W), F32), jax.ShapeDtypeStruct((b, BR_W, BR_W), F32)),
        scratch_shapes=[pltpu.VMEM((BR_W, BR_W), F32)],
        compiler_params=_cparams(("arbitrary", "arbitrary")),
        name="retention",
    )(q, k, v, _ret_state_embed(s0), dmat, dq, dk, ds, gw, gb)
    return y, _ret_state_extract(s)


def _ret_step_kernel(qc_ref, kc_ref, qr_ref, kr_ref, v_ref, s_ref, g_ref, gw_ref, gb_ref, y_ref, so_ref):
    s = s_ref[...]
    g = g_ref[...]
    v = v_ref[...]
    qk = jnp.sum(qr_ref[...] * kr_ref[...], axis=-1, keepdims=True)
    y = qk * v + jnp.sum(qc_ref[...] * s, axis=1, keepdims=True) * g
    so_ref[...] = s * g + kc_ref[...] * v
    mu = jnp.mean(y, axis=-1, keepdims=True)
    d = y - mu
    var = jnp.mean(d * d, axis=-1, keepdims=True)
    y_ref[...] = d * lax.rsqrt(var + RET_GN_EPS) * gw_ref[...] + gb_ref[...]


def _retention_step(q, k, v, s0, gn_w, gn_b):
    b = q.shape[0]
    n = b * N_HEADS
    ks = k * (HEAD_DIM ** -0.5)
    g = 1.0 - jnp.power(2.0, -5.0 - jnp.arange(N_HEADS, dtype=jnp.float32))
    g = jnp.tile(g, (b,)).reshape(n, 1, 1)
    gw = jnp.tile(gn_w.reshape(N_HEADS, HEAD_DIM), (b, 1)).reshape(n, 1, HEAD_DIM)
    gb = jnp.tile(gn_b.reshape(N_HEADS, HEAD_DIM), (b, 1)).reshape(n, 1, HEAD_DIM)
    col = lambda a: a.reshape(n, HEAD_DIM, 1)
    row = lambda a: a.reshape(n, 1, HEAD_DIM)
    args = (col(q), col(ks), row(q), row(ks), row(v), s0.reshape(n, HEAD_DIM, HEAD_DIM), g, gw, gb)
    full = lambda a: pl.BlockSpec(a.shape, lambda: (0,) * a.ndim)
    y, s = pl.pallas_call(
        _ret_step_kernel,
        in_specs=[full(a) for a in args],
        out_specs=(pl.BlockSpec((n, 1, HEAD_DIM), lambda: (0, 0, 0)),
                   pl.BlockSpec((n, HEAD_DIM, HEAD_DIM), lambda: (0, 0, 0))),
        out_shape=(jax.ShapeDtypeStruct((n, 1, HEAD_DIM), F32), jax.ShapeDtypeStruct((n, HEAD_DIM, HEAD_DIM), F32)),
        compiler_params=pltpu.CompilerParams(vmem_limit_bytes=VMEM_LIMIT_BYTES),
        name="retention_step",
    )(*args)
    return y.reshape(b, BR_W), s.reshape(b, N_HEADS, HEAD_DIM, HEAD_DIM)


INT_MIN = -2 ** 31


def _sort_key(s):
    s = jnp.where(s == 0.0, 0.0, s)
    bits = lax.bitcast_convert_type(s, I32)
    return jnp.where(bits < 0, bits ^ jnp.int32(0x7FFFFFFF), bits)


def _kth_largest_key(count_ge, shape, k):
    def bit_step(i, tb):
        cand = tb + jnp.left_shift(jnp.int32(1), 31 - i)
        return jnp.where(count_ge(cand) >= k, cand, tb)

    return lax.fori_loop(0, 32, bit_step, jnp.full(shape, INT_MIN, I32))


def _attn_queries(qf, qa_scr):
    qs = qf * (HEAD_DIM ** -0.5)
    for h in range(N_HEADS):
        qa_scr[h] = jnp.where(_head_mask(h, (1, BR_W)), qs, 0.0).astype(BF16)


def _flash_update_t(qa_scr, kb, vt, msk, stats, acc):
    new_stats = []
    parts = []
    for h in range(N_HEADS):
        m_old, l_old = stats[h]
        mk = msk[h] if isinstance(msk, (list, tuple)) else msk
        s = jnp.where(mk, _dot_nt(kb, qa_scr[h]), NEG)
        m_new = jnp.maximum(m_old, jnp.max(s, axis=0, keepdims=True))
        alpha = jnp.exp(m_old - m_new)
        p = jnp.exp(s - m_new)
        l_new = alpha * l_old + jnp.sum(p, axis=0, keepdims=True)
        rows = slice(h * HEAD_DIM, (h + 1) * HEAD_DIM)
        parts.append(acc[rows] * alpha + _dot(vt[rows], p.astype(BF16)))
        new_stats.append((m_new, l_new))
    return tuple(new_stats), jnp.concatenate(parts, axis=0)


def _flash_init_t(tq):
    stats = tuple((jnp.full((1, tq), NEG, F32), jnp.zeros((1, tq), F32)) for _ in range(N_HEADS))
    return stats, jnp.zeros((BR_W, tq), F32)


def _flash_finish_t(stats, acc):
    out = jnp.concatenate([acc[h * HEAD_DIM:(h + 1) * HEAD_DIM] / stats[h][1] for h in range(N_HEADS)], axis=0)
    return out.T


def _dsa_prompt_kernel(q_ref, qi_ref, wi_ref, kb_ref, vt_ref, kx_ref, o_ref,
                       key_scr, qs_scr, qa_scr, *, tq, kb, topk):
    qt = pl.program_id(1)
    nkc = (qt + 1) * (tq // kb)
    qpos = qt * tq + lax.broadcasted_iota(I32, (kb, tq), 1)
    kofs = lax.broadcasted_iota(I32, (kb, tq), 0)
    lane = lax.broadcasted_iota(I32, (tq, BR_W), 1)
    qi = qi_ref[0]
    for h in range(IDX_HEADS):
        qm = jnp.where(lane // IDX_DIM == h, qi, 0.0)
        rep = (qm + pltpu.roll(qm, IDX_DIM, 1)) + (pltpu.roll(qm, 2 * IDX_DIM, 1) + pltpu.roll(qm, 3 * IDX_DIM, 1))
        q_hi, q_lo = _split(rep)
        qs_scr[h] = jnp.where((lane < IDX_DIM) | ((lane >= 2 * IDX_DIM) & (lane < 3 * IDX_DIM)), q_hi,
                              jnp.where(lane < 2 * IDX_DIM, q_lo, jnp.zeros_like(q_lo)))
    _attn_queries(q_ref[0], qa_scr)
    w_t = wi_ref[0].T * (IDX_HEADS ** -0.5 * IDX_DIM ** -0.5)

    def score_chunk(c, carry):
        kx = kx_ref[0, pl.ds(pl.multiple_of(c * kb, kb), kb), :]
        s = jnp.zeros((kb, tq), F32)
        for h in range(IDX_HEADS):
            s = s + jnp.maximum(_dot_nt(kx, qs_scr[h]), 0.0) * w_t[h:h + 1, :]
        s = jnp.where(c * kb + kofs <= qpos, s, -jnp.inf)
        key_scr[c] = _sort_key(s)
        return carry

    lax.fori_loop(0, nkc, score_chunk, 0)

    def counter(cmp):
        def count(cand):
            def body(c, acc):
                kc = key_scr[c].reshape(4, kb // 32, 8, tq)
                return acc + jnp.sum(jnp.where(cmp(kc, cand), 1.0, 0.0), axis=1)
            acc = lax.fori_loop(0, nkc, body, jnp.zeros((4, 8, tq), F32))
            return jnp.sum(jnp.sum(acc, axis=0), axis=0, keepdims=True)
        return count

    thr = _kth_largest_key(counter(lambda a, b: a >= b), (1, tq), float(topk))
    rem = float(topk) - counter(lambda a, b: a > b)(thr)
    lower = (lax.broadcasted_iota(I32, (kb, kb), 1) <= lax.broadcasted_iota(I32, (kb, kb), 0)).astype(BF16)

    def attn_chunk(c, carry):
        stats, acc, run = carry
        kc = key_scr[c]
        eq = kc == thr
        eqf = jnp.where(eq, 1.0, 0.0)
        pre = _dot(lower, eqf.astype(BF16)) + run
        sel = (kc > thr) | (eq & (pre <= rem))
        msk = sel & (c * kb + kofs <= qpos)
        off = pl.multiple_of(c * kb, kb)
        stats, acc = _flash_update_t(qa_scr, kb_ref[0, pl.ds(off, kb), :], vt_ref[0, c], msk, stats, acc)
        return stats, acc, run + jnp.sum(eqf, axis=0, keepdims=True)

    stats, acc = _flash_init_t(tq)
    stats, acc, _ = lax.fori_loop(0, nkc, attn_chunk, (stats, acc, jnp.zeros((1, tq), F32)))
    o_ref[0] = _flash_finish_t(stats, acc)


def _dsa_prompt(q, qi, wi, kb16, vt16, kx, tq=256, kb=256):
    b, t, _ = q.shape
    topk = min(DSA_TOPK_MAX, t // 4)
    assert tq >= topk and tq == kb
    tile = lambda w: pl.BlockSpec((1, tq, w), lambda i, j: (i, j, 0))
    full = pl.BlockSpec((1, t, BR_W), lambda i, j: (i, 0, 0))
    return pl.pallas_call(
        functools.partial(_dsa_prompt_kernel, tq=tq, kb=kb, topk=topk),
        grid=(b, t // tq),
        in_specs=[tile(BR_W), tile(BR_W), tile(128), full,
                  pl.BlockSpec((1, t // kb, BR_W, kb), lambda i, j: (i, 0, 0, 0)), full],
        out_specs=tile(BR_W),
        out_shape=jax.ShapeDtypeStruct((b, t, BR_W), F32),
        scratch_shapes=[pltpu.VMEM((t // kb, kb, tq), I32),
                        pltpu.VMEM((IDX_HEADS, tq, BR_W), BF16), pltpu.VMEM((N_HEADS, tq, BR_W), BF16)],
        compiler_params=_cparams(("arbitrary", "arbitrary")),
        name="dsa_prompt",
    )(q, qi, wi, kb16, vt16, kx)


def _top_blocks_t(g, row, limit, nsel):
    g = jnp.where(row < limit, g, -jnp.inf)
    selm = jnp.zeros(g.shape, jnp.bool_)
    for _ in range(nsel):
        mx = jnp.max(g, axis=0, keepdims=True)
        idx = jnp.min(jnp.where(g == mx, row, jnp.int32(1 << 20)), axis=0, keepdims=True)
        pick = row == idx
        selm = selm | (pick & (row < limit))
        g = jnp.where(pick, -jnp.inf, g)
    return selm


def _moba_prompt_kernel(q_ref, kbar_ref, kb_ref, vt_ref, o_ref, qa_scr, sel_scr, *, nsel):
    qt = pl.program_id(1)
    tq = MOBA_BLOCK
    qf = q_ref[0]
    kbar = kbar_ref[0]
    row = lax.broadcasted_iota(I32, (kbar.shape[0], tq), 0)
    for h in range(N_HEADS):
        g = _mm3_nt(kbar, jnp.where(_head_mask(h, (1, BR_W)), qf, 0.0))
        sel_scr[h] = jnp.where(_top_blocks_t(g, row, qt, nsel), 1.0, 0.0)
    _attn_queries(qf, qa_scr)

    tri = lax.broadcasted_iota(I32, (tq, tq), 0) <= lax.broadcasted_iota(I32, (tq, tq), 1)
    off = pl.multiple_of(qt * tq, tq)
    stats, acc = _flash_init_t(tq)
    stats, acc = _flash_update_t(qa_scr, kb_ref[0, pl.ds(off, tq), :], vt_ref[0, qt], tri, stats, acc)

    def past(n, carry):
        stats, acc = carry
        o = pl.multiple_of(n * tq, tq)
        allow = [sel_scr[h, pl.ds(n, 1), :] > 0.0 for h in range(N_HEADS)]
        return _flash_update_t(qa_scr, kb_ref[0, pl.ds(o, tq), :], vt_ref[0, n], allow, stats, acc)

    stats, acc = lax.fori_loop(0, qt, past, (stats, acc))
    o_ref[0] = _flash_finish_t(stats, acc)


def _moba_prompt(q, kbar, kb16, vt16):
    b, t, _ = q.shape
    nb = t // MOBA_BLOCK
    nsel = min(MOBA_TOPK, nb - 1)
    nbp = -(-nb // 16) * 16
    kbar = jnp.pad(kbar, ((0, 0), (0, nbp - nb), (0, 0)))
    tile = pl.BlockSpec((1, MOBA_BLOCK, BR_W), lambda i, j: (i, j, 0))
    full = pl.BlockSpec((1, t, BR_W), lambda i, j: (i, 0, 0))
    return pl.pallas_call(
        functools.partial(_moba_prompt_kernel, nsel=nsel),
        grid=(b, nb),
        in_specs=[tile, pl.BlockSpec((1, nbp, BR_W), lambda i, j: (i, 0, 0)), full,
                  pl.BlockSpec((1, nb, BR_W, MOBA_BLOCK), lambda i, j: (i, 0, 0, 0))],
        out_specs=tile,
        out_shape=jax.ShapeDtypeStruct((b, t, BR_W), F32),
        scratch_shapes=[pltpu.VMEM((N_HEADS, MOBA_BLOCK, BR_W), BF16),
                        pltpu.VMEM((N_HEADS, nbp, MOBA_BLOCK), F32)],
        compiler_params=_cparams(("arbitrary", "arbitrary")),
        name="moba_prompt",
    )(q, kbar, kb16, vt16)


def _layer_out_kernel(x_ref, mod_ref, gpre_ref, gpost_ref, ya_ref, bo_ref, yb_ref, yc_ref, yd_ref, gt_ref,
                      lnw_ref, lnb_ref, wb_ref, wm_ref, wo_ref, o_ref):
    x = x_ref[0]
    y = x * lax.rsqrt(jnp.mean(x * x, axis=-1, keepdims=True) + RMS_EPS) * gpre_ref[...]
    shift = mod_ref[0, :, 0:D_MODEL]
    scale = mod_ref[0, :, D_MODEL:2 * D_MODEL]
    gate = mod_ref[0, :, 2 * D_MODEL:3 * D_MODEL]
    hb = (y * (1.0 + scale) + shift).astype(BF16)
    bd = _head_ones()
    ya = ya_ref[0]
    mu = _headsum(ya, bd) * (1.0 / HEAD_DIM)
    d = ya - mu
    var = _headsum(d * d, bd) * (1.0 / HEAD_DIM)
    ya = d * lax.rsqrt(var + RWKV_GN_EPS) * lnw_ref[...] + lnb_ref[...] + bo_ref[0]
    outs = (ya, yb_ref[0], yc_ref[0], yd_ref[0])
    merged = jnp.zeros(x.shape, F32)
    for n in range(4):
        o = outs[n] * jax.nn.silu(gt_ref[0, :, n * BR_W:(n + 1) * BR_W])
        merged = merged + jax.nn.sigmoid(_dot(hb, wm_ref[n])) * _dot(o.astype(BF16), wb_ref[n])
    z = _dot(merged.astype(BF16), wo_ref[...])
    z = z * lax.rsqrt(jnp.mean(z * z, axis=-1, keepdims=True) + RMS_EPS) * gpost_ref[...]
    o_ref[0] = x + gate * z


def _layer_out(x, mod, p, ya, bonus, yb, yc, yd, gates, tm):
    b, t, _ = x.shape
    r = mod.shape[1]
    if r == 1:
        mod_spec = pl.BlockSpec((1, 1, 3 * D_MODEL), lambda i, j: (i, 0, 0))
    else:
        mod_spec = pl.BlockSpec((1, tm, 3 * D_MODEL), lambda i, j: (i, j, 0))
    tile = lambda w: pl.BlockSpec((1, tm, w), lambda i, j: (i, j, 0))
    row = lambda v: v.reshape(1, -1)
    const = lambda shape: pl.BlockSpec(shape, lambda i, j: (0,) * len(shape), pipeline_mode=pl.Buffered(1))
    vec = lambda n: pl.BlockSpec((1, n), lambda i, j: (0, 0))
    return pl.pallas_call(
        _layer_out_kernel,
        grid=(b, t // tm),
        in_specs=[tile(D_MODEL), mod_spec, vec(D_MODEL), vec(D_MODEL),
                  tile(BR_W), tile(BR_W), tile(BR_W), tile(BR_W), tile(BR_W), tile(D_MODEL),
                  vec(BR_W), vec(BR_W),
                  const((4, BR_W, D_MODEL)), const((4, D_MODEL, D_MODEL)), const((D_MODEL, D_MODEL))],
        out_specs=tile(D_MODEL),
        out_shape=jax.ShapeDtypeStruct((b, t, D_MODEL), F32),
        compiler_params=_cparams(("arbitrary", "arbitrary")),
        name="layer_out",
    )(x, mod, row(p['g_pre']), row(p['g_post']), ya, bonus, yb, yc, yd, gates,
      row(p['a_ln_w']), row(p['a_ln_b']),
      p['w_branch'].astype(BF16), p['w_merge'].astype(BF16), p['w_out'].astype(BF16))


PAGES_PER_STEP = 8


def _page_specs(block, g_count):
    tail = (0,) * (len(block) - 1)

    def spec(g):
        return pl.BlockSpec(block, lambda i, j, pt: (pt[i, j * g_count + g],) + tail)
    return [spec(g) for g in range(g_count)]


def _head_rows(x):
    hr = lax.broadcasted_iota(I32, (N_HEADS, BR_W), 0)
    hl = lax.broadcasted_iota(I32, (N_HEADS, BR_W), 1) // HEAD_DIM
    return jnp.where(hr == hl, x, 0.0), hr == hl


def _dsa_scores_kernel(pt_ref, qm_ref, wi_ref, kn_ref, *refs, g_count):
    page_refs = refs[:g_count]
    o_ref, on_ref = refs[g_count], refs[g_count + 1]
    j = pl.program_id(1)
    qm = qm_ref[0]
    q_hi, q_lo = _split(qm)
    w = wi_ref[0] * (IDX_HEADS ** -0.5)

    def combine(d):
        return jnp.sum(jnp.maximum(d * (IDX_DIM ** -0.5), 0.0) * w, axis=0, keepdims=True)

    for g in range(g_count):
        k_hi, k_lo = _split(page_refs[g][0])
        d = _dot_nt(q_hi, k_hi) + (_dot_nt(q_lo, k_hi) + _dot_nt(q_hi, k_lo))
        o_ref[0, g] = combine(d)

    @pl.when(j == 0)
    def _():
        dn = jnp.sum(qm * kn_ref[0], axis=-1, keepdims=True)
        on_ref[0] = jnp.broadcast_to(combine(dn), (1, 128))


def _dsa_scores(qi, wi, ki_new, pool, page_table):
    b, n_pages = page_table.shape
    g_count = min(PAGES_PER_STEP, n_pages)
    qm = qi.reshape(b, IDX_HEADS, IDX_DIM)
    wcol = wi.reshape(b, IDX_HEADS, 1)
    kn = ki_new.reshape(b, 1, IDX_DIM)
    grid_spec = pltpu.PrefetchScalarGridSpec(
        num_scalar_prefetch=1,
        grid=(b, n_pages // g_count),
        in_specs=[pl.BlockSpec((1, IDX_HEADS, IDX_DIM), lambda i, j, pt: (i, 0, 0)),
                  pl.BlockSpec((1, IDX_HEADS, 1), lambda i, j, pt: (i, 0, 0)),
                  pl.BlockSpec((1, 1, IDX_DIM), lambda i, j, pt: (i, 0, 0))]
                 + _page_specs((1, PAGE_SIZE, IDX_DIM), g_count),
        out_specs=(pl.BlockSpec((1, g_count, 1, PAGE_SIZE), lambda i, j, pt: (i, j, 0, 0)),
                   pl.BlockSpec((1, 1, 128), lambda i, j, pt: (i, 0, 0))))
    sc, sc_new = pl.pallas_call(
        functools.partial(_dsa_scores_kernel, g_count=g_count),
        grid_spec=grid_spec,
        out_shape=(jax.ShapeDtypeStruct((b, n_pages, 1, PAGE_SIZE), F32),
                   jax.ShapeDtypeStruct((b, 1, 128), F32)),
        compiler_params=_cparams(("arbitrary", "arbitrary")),
        name="dsa_scores",
    )(page_table, qm, wcol, kn, *([pool] * g_count))
    return sc.reshape(b, n_pages * PAGE_SIZE), sc_new[:, 0, 0:1]


def _topk_rows_kernel(s_ref, o_ref, key_scr, *, n_valid, topk):
    nblk, rows, _ = s_ref.shape
    col = lax.broadcasted_iota(I32, (rows, 128), 1)

    def to_key(j, carry):
        s = jnp.where(j * 128 + col < n_valid, s_ref[j], -jnp.inf)
        key_scr[j] = _sort_key(s)
        return carry

    lax.fori_loop(0, nblk, to_key, 0)

    def counter(cmp):
        def count(cand):
            body = lambda j, acc: acc + jnp.where(cmp(key_scr[j], cand), 1.0, 0.0)
            acc = lax.fori_loop(0, nblk, body, jnp.zeros((rows, 128), F32))
            return jnp.sum(acc, axis=-1, keepdims=True)
        return count

    tb = _kth_largest_key(counter(lambda a, b: a >= b), (rows, 128), float(topk))
    rem = float(topk) - counter(lambda a, b: a > b)(tb)
    upper = (lax.broadcasted_iota(I32, (128, 128), 0) <= lax.broadcasted_iota(I32, (128, 128), 1)).astype(BF16)

    def select(j, run):
        kc = key_scr[j]
        eq = kc == tb
        eqf = jnp.where(eq, 1.0, 0.0)
        pre = _dot(eqf.astype(BF16), upper) + run
        sel = ((kc > tb) | (eq & (pre <= rem))) & (j * 128 + col < n_valid)
        o_ref[j] = jnp.where(sel, 1.0, 0.0)
        return run + jnp.sum(eqf, axis=-1, keepdims=True)

    lax.fori_loop(0, nblk, select, jnp.zeros((rows, 1), F32))


def _topk_rows(scores, topk):
    rows, n = scores.shape
    nblk = -(-n // 128)
    s = jnp.pad(scores, ((0, 0), (0, nblk * 128 - n)))
    s = jnp.transpose(s.reshape(rows, nblk, 128), (1, 0, 2))
    m = pl.pallas_call(
        functools.partial(_topk_rows_kernel, n_valid=n, topk=topk),
        in_specs=[pl.BlockSpec((nblk, rows, 128), lambda: (0, 0, 0))],
        out_specs=pl.BlockSpec((nblk, rows, 128), lambda: (0, 0, 0)),
        out_shape=jax.ShapeDtypeStruct((nblk, rows, 128), F32),
        scratch_shapes=[pltpu.VMEM((nblk, rows, 128), I32)],
        compiler_params=pltpu.CompilerParams(vmem_limit_bytes=VMEM_LIMIT_BYTES),
        name="topk_rows",
    )(s)
    return jnp.transpose(m, (1, 0, 2)).reshape(rows, nblk * 128)[:, :n]


def _dsa_attn_kernel(pt_ref, q_ref, kn_ref, vn_ref, mn_ref, msk_ref, *refs, g_count):
    k_refs = refs[:g_count]
    v_refs = refs[g_count:2 * g_count]
    o_ref = refs[2 * g_count]
    m_scr, l_scr, acc_scr = refs[2 * g_count + 1:]
    j = pl.program_id(1)
    qbd, hsel = _head_rows(q_ref[0])

    @pl.when(j == 0)
    def _():
        sn = jnp.sum(qbd * kn_ref[0], axis=-1, keepdims=True) * (HEAD_DIM ** -0.5)
        ok = mn_ref[0][:, 0:1] > 0.0
        m_scr[...] = jnp.broadcast_to(jnp.where(ok, sn, NEG), (N_HEADS, 128))
        l_scr[...] = jnp.broadcast_to(jnp.where(ok, 1.0, 0.0), (N_HEADS, 128))
        acc_scr[...] = jnp.where(ok, jnp.broadcast_to(vn_ref[0], (N_HEADS, BR_W)), 0.0)

    kcat = jnp.concatenate([r[0] for r in k_refs], axis=0).astype(BF16)
    vcat = jnp.concatenate([r[0] for r in v_refs], axis=0).astype(BF16)
    mk = msk_ref[0] > 0.0
    s = jnp.where(mk, _dot_nt(qbd.astype(BF16), kcat) * (HEAD_DIM ** -0.5), NEG)
    m_old = m_scr[:, 0:1]
    m_new = jnp.maximum(m_old, jnp.max(s, axis=-1, keepdims=True))
    alpha = jnp.exp(m_old - m_new)
    p = jnp.where(mk, jnp.exp(s - m_new), 0.0)
    l_new = alpha * l_scr[:, 0:1] + jnp.sum(p, axis=-1, keepdims=True)
    acc = alpha * acc_scr[...] + _dot(p.astype(BF16), vcat)
    m_scr[...] = jnp.broadcast_to(m_new, (N_HEADS, 128))
    l_scr[...] = jnp.broadcast_to(l_new, (N_HEADS, 128))
    acc_scr[...] = acc

    @pl.when(j == pl.num_programs(1) - 1)
    def _():
        o_ref[0] = jnp.sum(jnp.where(hsel, acc / l_new, 0.0), axis=0, keepdims=True)


def _dsa_attn(q, k_new, v_new, mask, k_pool, v_pool, page_table):
    b, n_pages = page_table.shape
    g_count = min(PAGES_PER_STEP, n_pages)
    past = n_pages * PAGE_SIZE
    row = lambda a: a.reshape(b, 1, BR_W)
    m_new = jnp.broadcast_to(mask[:, past:past + 1], (b, 128)).reshape(b, 1, 128)
    m_past = mask[:, :past].reshape(b, 1, past)
    rspec = pl.BlockSpec((1, 1, BR_W), lambda i, j, pt: (i, 0, 0))
    grid_spec = pltpu.PrefetchScalarGridSpec(
        num_scalar_prefetch=1,
        grid=(b, n_pages // g_count),
        in_specs=[rspec, rspec, rspec,
                  pl.BlockSpec((1, 1, 128), lambda i, j, pt: (i, 0, 0)),
                  pl.BlockSpec((1, 1, g_count * PAGE_SIZE), lambda i, j, pt: (i, 0, j))]
                 + _page_specs((1, PAGE_SIZE, BR_W), g_count) * 2,
        out_specs=rspec,
        scratch_shapes=[pltpu.VMEM((N_HEADS, 128), F32), pltpu.VMEM((N_HEADS, 128), F32),
                        pltpu.VMEM((N_HEADS, BR_W), F32)])
    out = pl.pallas_call(
        functools.partial(_dsa_attn_kernel, g_count=g_count),
        grid_spec=grid_spec,
        out_shape=jax.ShapeDtypeStruct((b, 1, BR_W), F32),
        compiler_params=_cparams(("arbitrary", "arbitrary")),
        name="dsa_attn",
    )(page_table, row(q), row(k_new), row(v_new), m_new, m_past, *([k_pool] * g_count), *([v_pool] * g_count))
    return out.reshape(b, BR_W)


def _kbar_kernel(pt_ref, *refs, g_count):
    page_refs = refs[:g_count]
    o_ref = refs[g_count]
    for g2 in range(g_count // 2):
        s = (jnp.sum(page_refs[2 * g2][0].astype(F32), axis=0, keepdims=True)
             + jnp.sum(page_refs[2 * g2 + 1][0].astype(F32), axis=0, keepdims=True))
        o_ref[0, g2] = s * (1.0 / MOBA_BLOCK)


def _moba_kbar(k_pool, page_table):
    b, n_pages = page_table.shape
    g_count = min(PAGES_PER_STEP, n_pages)
    grid_spec = pltpu.PrefetchScalarGridSpec(
        num_scalar_prefetch=1,
        grid=(b, n_pages // g_count),
        in_specs=_page_specs((1, PAGE_SIZE, BR_W), g_count),
        out_specs=pl.BlockSpec((1, g_count // 2, 1, BR_W), lambda i, j, pt: (i, j, 0, 0)))
    out = pl.pallas_call(
        functools.partial(_kbar_kernel, g_count=g_count),
        grid_spec=grid_spec,
        out_shape=jax.ShapeDtypeStruct((b, n_pages // 2, 1, BR_W), F32),
        compiler_params=_cparams(("arbitrary", "arbitrary")),
        name="moba_kbar",
    )(page_table, *([k_pool] * g_count))
    return out.reshape(b, n_pages // 2, BR_W)


def _moba_gate_kernel(q_ref, kbar_ref, o_ref, *, n_past, nsel):
    qbd, _ = _head_rows(q_ref[0])
    g = _mm3_nt(qbd, kbar_ref[0])
    col = lax.broadcasted_iota(I32, (N_HEADS, 128), 1)
    g = jnp.where(col < n_past, g, -jnp.inf)
    out = jnp.full((N_HEADS, 128), -1, I32)
    for i in range(nsel):
        mx = jnp.max(g, axis=-1, keepdims=True)
        idx = jnp.min(jnp.where(g == mx, col, jnp.int32(1 << 20)), axis=-1, keepdims=True)
        ok = idx < n_past
        out = jnp.where(col == i, jnp.where(ok, idx, -1), out)
        g = jnp.where(col == idx, -jnp.inf, g)
    o_ref[0] = out


def _moba_gate(q, kbar, nsel):
    b, n_past, _ = kbar.shape
    kb = jnp.pad(kbar, ((0, 0), (0, 128 - n_past), (0, 0)))
    out = pl.pallas_call(
        functools.partial(_moba_gate_kernel, n_past=n_past, nsel=nsel),
        grid=(b,),
        in_specs=[pl.BlockSpec((1, 1, BR_W), lambda i: (i, 0, 0)),
                  pl.BlockSpec((1, 128, BR_W), lambda i: (i, 0, 0))],
        out_specs=pl.BlockSpec((1, N_HEADS, 128), lambda i: (i, 0, 0)),
        out_shape=jax.ShapeDtypeStruct((b, N_HEADS, 128), I32),
        compiler_params=_cparams(("arbitrary",)),
        name="moba_gate",
    )(q.reshape(b, 1, BR_W), kb)
    return out[:, :, :nsel]


def _moba_attn_kernel(sel_ref, pt_ref, q_ref, kn_ref, vn_ref, *refs, nsel):
    n_pg = 2 * nsel
    k_refs = refs[:n_pg]
    v_refs = refs[n_pg:2 * n_pg]
    o_ref = refs[2 * n_pg]
    bi = pl.program_id(0)
    h = pl.program_id(1)
    hm = lax.broadcasted_iota(I32, (1, BR_W), 1) // HEAD_DIM == h
    qh = jnp.where(hm, q_ref[0], 0.0)
    sn = jnp.sum(qh * kn_ref[0], axis=-1, keepdims=True) * (HEAD_DIM ** -0.5)
    kcat = jnp.concatenate([r[0] for r in k_refs], axis=0).astype(BF16)
    vcat = jnp.concatenate([r[0] for r in v_refs], axis=0).astype(BF16)
    s = _dot_nt(qh.astype(BF16), kcat) * (HEAD_DIM ** -0.5)
    blk = lax.broadcasted_iota(I32, s.shape, 1) // MOBA_BLOCK
    mk = jnp.zeros(s.shape, jnp.bool_)
    for i in range(nsel):
        mk = mk | (blk == jnp.where(sel_ref[(bi * N_HEADS + h) * nsel + i] >= 0, i, -1))
    s = jnp.where(mk, s, NEG)
    m = jnp.maximum(jnp.max(s, axis=-1, keepdims=True), sn)
    p = jnp.where(mk, jnp.exp(s - m), 0.0)
    pn = jnp.exp(sn - m)
    out = (_dot(p.astype(BF16), vcat) + pn * vn_ref[0]) / (jnp.sum(p, axis=-1, keepdims=True) + pn)

    @pl.when(h == 0)
    def _():
        o_ref[0] = jnp.zeros((1, BR_W), F32)

    o_ref[0] = o_ref[0] + jnp.where(hm, out, 0.0)


def _moba_attn(q, k_new, v_new, sel, k_pool, v_pool, page_table):
    b, n_pages = page_table.shape
    nsel = sel.shape[-1]
    row = lambda a: a.reshape(b, 1, BR_W)

    def pspec(i, half):
        def imap(bi, h, sel_ref, pt):
            blk = jnp.maximum(sel_ref[(bi * N_HEADS + h) * nsel + i], 0)
            return (pt[bi, 2 * blk + half], 0, 0)
        return pl.BlockSpec((1, PAGE_SIZE, BR_W), imap)

    pages = [pspec(i, half) for i in range(nsel) for half in range(2)]
    rspec = pl.BlockSpec((1, 1, BR_W), lambda bi, h, s, pt: (bi, 0, 0))
    grid_spec = pltpu.PrefetchScalarGridSpec(
        num_scalar_prefetch=2,
        grid=(b, N_HEADS),
        in_specs=[rspec, rspec, rspec] + pages + pages,
        out_specs=rspec)
    out = pl.pallas_call(
        functools.partial(_moba_attn_kernel, nsel=nsel),
        grid_spec=grid_spec,
        out_shape=jax.ShapeDtypeStruct((b, 1, BR_W), F32),
        compiler_params=_cparams(("arbitrary", "arbitrary")),
        name="moba_attn",
    )(sel.reshape(-1), page_table, row(q), row(k_new), row(v_new),
      *([k_pool] * (2 * nsel)), *([v_pool] * (2 * nsel)))
    return out.reshape(b, BR_W)


def _sample_layer(x, c, p, a_shift, a_wkv, b_ret, ck_pool, cv_pool, cki_pool, dk_pool, dv_pool, page_table):
    b = x.shape[0]
    n_pages = page_table.shape[1]
    past = n_pages * PAGE_SIZE
    assert x.shape[1] == 1 and past % MOBA_BLOCK == 0
    pos = jnp.full((b,), past, I32)
    tab_ret = _rope_tables(pos, HEAD_DIM, RET_THETA, BR_W)
    tab_std = _rope_tables(pos, ROPE_DIMS, ROPE_THETA, BR_W)
    w_hi, w_lo = _regroup_w_in(p['w_in'])
    mod = _ada(c, p['w_ada'], p['b_ada']).reshape(1, b, 3 * D_MODEL)
    xr = x.reshape(1, b, D_MODEL)
    outs = _layer_in(xr, mod, p['g_pre'], w_hi, w_lo, tab_ret, tab_std, b)
    (ua, gates, qb, kb, vb, qc, kc, vc, qi, ki4, wi, qd, kd, vd) = [o[0] for o in outs]
    r, w, k2, v, kk, bb, bonus = _rwkv_pre_rows(ua, a_shift, p)
    ya, wkv = _rwkv_step(r, w, k2, v, kk, bb, a_wkv)
    yb, ret = _retention_step(qb, kb, vb, b_ret, p['b_gn_w'], p['b_gn_b'])
    ki = ki4[:, :IDX_DIM]
    sc_past, sc_new = _dsa_scores(qi, wi[:, :IDX_HEADS], ki, cki_pool, page_table)
    total = past + 1
    mask = _topk_rows(jnp.concatenate([sc_past, sc_new], axis=1), min(DSA_TOPK_MAX, total // 4))
    yc = _dsa_attn(qc, kc, vc, mask, ck_pool, cv_pool, page_table)
    n_past_blocks = past // MOBA_BLOCK
    nsel = min(MOBA_TOPK, n_past_blocks)
    if nsel > 0:
        sel = _moba_gate(qd, _moba_kbar(dk_pool, page_table), nsel)
        yd = _moba_attn(qd, kd, vd, sel, dk_pool, dv_pool, page_table)
    else:
        yd = vd
    row = lambda a: a.reshape(1, b, -1)
    x_new = _layer_out(xr, mod, p, row(ya), row(bonus), row(yb), row(yc), row(yd), gates.reshape(1, b, -1), b)
    heads = lambda a: a.reshape(b, 1, N_HEADS, HEAD_DIM)
    new_state = (ua, wkv, ret, heads(kc), heads(vc), ki.reshape(b, 1, IDX_DIM), heads(kd), heads(vd))
    return x_new.reshape(b, 1, D_MODEL), new_state


def _prompt_layer(x, c, p):
    b, t, _ = x.shape
    pos = jnp.arange(t)
    tab_ret = _rope_tables(pos, HEAD_DIM, RET_THETA, BR_W)
    tab_std = _rope_tables(pos, ROPE_DIMS, ROPE_THETA, BR_W)
    w_hi, w_lo = _regroup_w_in(p['w_in'])
    mod = _ada(c, p['w_ada'], p['b_ada']).reshape(b, 1, 3 * D_MODEL)
    tm = MOBA_BLOCK
    assert t % tm == 0
    (ua, gates, qb, kb, vb, qc, kc, vc, qi, ki4, wi, qd, kd, vd, kcb, vct, kix, kdb, vdt, kbar) = _layer_in(
        x, mod, p['g_pre'], w_hi, w_lo, tab_ret, tab_std, tm, attention_operands=True)
    r, w, k2, v, kk, bb, bonus = _rwkv_pre_prompt(ua, jnp.zeros((b, A_SHIFT_W), F32), p, tm)
    ya, wkv = _rwkv_scan(r, w, k2, v, kk, bb, jnp.zeros((b, HEAD_DIM, BR_W), F32), min(64, t))
    yb, ret = _retention_prompt(qb, kb, vb, jnp.zeros((b, N_HEADS, HEAD_DIM, HEAD_DIM), F32),
                                p['b_gn_w'], p['b_gn_b'], min(256, t))
    yc = _dsa_prompt(qc, qi, wi, kcb, vct, kix)
    yd = _moba_prompt(qd, kbar.reshape(b, t // tm, BR_W), kdb, vdt)
    x_new = _layer_out(x, mod, p, ya, bonus, yb, yc, yd, gates, tm)
    heads = lambda a: a.reshape(b, t, N_HEADS, HEAD_DIM)
    new_state = (ua[:, -1], _wkv_from_scan_layout(wkv), ret, heads(kc), heads(vc), ki4[..., :IDX_DIM],
                 heads(kd), heads(vd))
    return x_new, new_state


_PARAM_NAMES = ('w_ada', 'b_ada', 'g_pre', 'g_post', 'w_in', 'a_mu', 'a_w0', 'a_w2', 'a_a0', 'a_a2', 'a_kk', 'a_ka',
                'a_rk', 'a_ln_w', 'a_ln_b', 'b_gn_w', 'b_gn_b', 'w_branch', 'w_merge', 'w_out')


def kernel(x_prompt, x_sample, c_prompt, c_sample, state_a_shift, state_a_wkv, state_b_ret, cache_c_k, cache_c_v,
           cache_c_kidx, cache_d_k, cache_d_v, page_table, w_ada, b_ada, g_pre, g_post, w_in, a_mu, a_w0, a_w2,
           a_a0, a_a2, a_kk, a_ka, a_rk, a_ln_w, a_ln_b, b_gn_w, b_gn_b, w_branch, w_merge, w_out):
    stacked = dict(zip(_PARAM_NAMES, (w_ada, b_ada, g_pre, g_post, w_in, a_mu, a_w0, a_w2, a_a0, a_a2, a_kk, a_ka,
                                      a_rk, a_ln_w, a_ln_b, b_gn_w, b_gn_b, w_branch, w_merge, w_out)))
    depth = w_in.shape[0]
    n_pool = cache_c_k.shape[1]
    fold = lambda a: a.reshape(depth * n_pool, PAGE_SIZE, -1)
    ck, cv, dk, dv = (fold(a.astype(BF16)) for a in (cache_c_k, cache_c_v, cache_d_k, cache_d_v))
    cki = fold(cache_c_kidx)
    xp, xs = x_prompt, x_sample
    p_new, s_new = [], []
    for l in range(depth):
        p = {name: val[l] for name, val in stacked.items()}
        xp, st_p = _prompt_layer(xp, c_prompt, p)
        xs, st_s = _sample_layer(xs, c_sample, p, state_a_shift[l], state_a_wkv[l], state_b_ret[l],
                                 ck, cv, cki, dk, dv, page_table + l * n_pool)
        p_new.append(st_p)
        s_new.append(st_s)
    stack = lambda states, i: jnp.stack([s[i] for s in states])
    return ((xp, xs) + tuple(stack(p_new, i) for i in range(8)) + tuple(stack(s_new, i) for i in range(8)))
```

```python
import functools

import jax
import jax.numpy as jnp
import numpy as np
from jax import lax
from jax.experimental import pallas as pl
from jax.experimental.pallas import tpu as pltpu

F32 = jnp.float32
BF16 = jnp.bfloat16
I32 = jnp.int32

D_MODEL = 1024
PAGE_SIZE = 128
BR_W = 256
HEAD_DIM = 64
N_HEADS = 4
LORA_W = 64
LORA_A = 64
A_SHIFT_W = 3 * BR_W + LORA_W + LORA_A
ROPE_THETA = 500000.0
ROPE_DIMS = HEAD_DIM // 4
RET_THETA = 10000.0
IDX_HEADS = 4
IDX_DIM = 64
DSA_TOPK_MAX = 256
MOBA_BLOCK = 256
MOBA_TOPK = 3
RMS_EPS = 1e-6
RWKV_GN_EPS = 64e-5
RET_GN_EPS = 1e-5
NEG = -1e30
VMEM_LIMIT_BYTES = 56 * 1024 * 1024
RWKV_GROUP = 8

_C = {}
_off = 0
for _name, _n in (('a_r', 256), ('a_k', 256), ('a_v', 256), ('a_wl', 64), ('a_al', 64), ('a_g', 256),
                  ('b_q', 256), ('b_k', 256), ('b_v', 256), ('b_g', 256),
                  ('c_q', 256), ('c_k', 256), ('c_v', 256), ('c_qi', 256), ('c_ki', 64),
                  ('c_wi', 4), ('c_g', 256),
                  ('d_q', 256), ('d_k', 256), ('d_v', 256), ('d_g', 256)):
    _C[_name] = (_off, _off + _n)
    _off += _n

W_A, W_G, W_B, W_C, W_I, W_D = 896, 1024, 768, 768, 640, 768
OFF_A = 0
OFF_G = OFF_A + W_A
OFF_B = OFF_G + W_G
OFF_C = OFF_B + W_B
OFF_I = OFF_C + W_C
OFF_D = OFF_I + W_I
W_ALL = OFF_D + W_D


def _cparams(sem):
    return pltpu.CompilerParams(dimension_semantics=sem, vmem_limit_bytes=VMEM_LIMIT_BYTES)


def _split(x):
    hi = x.astype(BF16)
    lo = (x - hi.astype(F32)).astype(BF16)
    return hi, lo


def _dot(a, b):
    return jnp.dot(a, b, preferred_element_type=F32)


def _dot_nt(a, b):
    return lax.dot_general(a, b, (((1,), (1,)), ((), ())), preferred_element_type=F32)


def _mm1(a, b):
    return _dot(a.astype(BF16), b.astype(BF16))


def _mm3(a, b_hi, b_lo):
    a_hi, a_lo = _split(a)
    return _dot(a_hi, b_hi) + (_dot(a_lo, b_hi) + _dot(a_hi, b_lo))


def _mm3_nt(a, b):
    a_hi, a_lo = _split(a)
    b_hi, b_lo = _split(b)
    return _dot_nt(a_hi, b_hi) + (_dot_nt(a_lo, b_hi) + _dot_nt(a_hi, b_lo))


def _head_ones():
    r = lax.broadcasted_iota(I32, (BR_W, BR_W), 0) // HEAD_DIM
    c = lax.broadcasted_iota(I32, (BR_W, BR_W), 1) // HEAD_DIM
    return jnp.where(r == c, 1.0, 0.0).astype(BF16)


def _headsum(x, bd):
    hi, lo = _split(x)
    return _dot(hi, bd) + _dot(lo, bd)


def _head_mask(h, shape):
    c = lax.broadcasted_iota(I32, shape, len(shape) - 1) // HEAD_DIM
    return c == h


def _ada_kernel(c_ref, w_ref, b_ref, o_ref):
    w = w_ref[...]
    w_hi, w_lo = _split(w)
    o_ref[...] = _mm3(c_ref[...], w_hi, w_lo) + b_ref[...]


def _ada(c, w_ada, b_ada):
    bc = c.shape[0]
    n = w_ada.shape[1]
    tn = 1024
    return pl.pallas_call(
        _ada_kernel,
        grid=(n // tn,),
        in_specs=[pl.BlockSpec((bc, D_MODEL), lambda j: (0, 0)),
                  pl.BlockSpec((D_MODEL, tn), lambda j: (0, j)),
                  pl.BlockSpec((1, tn), lambda j: (0, j))],
        out_specs=pl.BlockSpec((bc, tn), lambda j: (0, j)),
        out_shape=jax.ShapeDtypeStruct((bc, n), F32),
        compiler_params=_cparams(("arbitrary",)),
        name="ada",
    )(c, w_ada, b_ada.reshape(1, n))


def _rope_tables(pos, rot_dims, theta, width):
    half = rot_dims // 2
    inv = jnp.power(jnp.float32(theta), -jnp.arange(half, dtype=jnp.float32) / half)
    ang = pos.astype(jnp.float32)[:, None] * inv[None, :]
    cos = jnp.cos(ang)
    sin = jnp.sin(ang)
    t = pos.shape[0]
    one = jnp.ones((t, HEAD_DIM - rot_dims), F32)
    zero = jnp.zeros((t, HEAD_DIM - rot_dims), F32)
    zh = jnp.zeros((t, half), F32)
    cos_h = jnp.concatenate([cos, cos, one], axis=1)
    up_h = jnp.concatenate([-sin, zh, zero], axis=1)
    dn_h = jnp.concatenate([zh, sin, zero], axis=1)
    reps = width // HEAD_DIM
    return jnp.stack([jnp.tile(cos_h, (1, reps)), jnp.tile(up_h, (1, reps)), jnp.tile(dn_h, (1, reps))])


def _rope_apply(x, tab_ref, lo, hi, half):
    n = hi - lo
    cos = tab_ref[0, :, lo:hi]
    up = tab_ref[1, :, lo:hi]
    dn = tab_ref[2, :, lo:hi]
    return x * cos + pltpu.roll(x, n - half, 1) * up + pltpu.roll(x, half, 1) * dn


def _indexer_key_operand(ki4):
    k_hi, k_lo = _split(ki4)
    lane = lax.broadcasted_iota(I32, ki4.shape, 1)
    return jnp.where(lane < 2 * IDX_DIM, k_hi, jnp.where(lane < 3 * IDX_DIM, k_lo, jnp.zeros_like(k_lo)))


def _layer_in_kernel(x_ref, mod_ref, g_ref, wh_ref, wl_ref, tr_ref, ts_ref,
                     ua_ref, gt_ref, qb_ref, kb_ref, vb_ref, qc_ref, kc_ref, vc_ref,
                     qi_ref, ki_ref, wi_ref, qd_ref, kd_ref, vd_ref, *extra_refs):
    x = x_ref[0]
    y = x * lax.rsqrt(jnp.mean(x * x, axis=-1, keepdims=True) + RMS_EPS) * g_ref[...]
    shift = mod_ref[0, :, 0:D_MODEL]
    scale = mod_ref[0, :, D_MODEL:2 * D_MODEL]
    h = y * (1.0 + scale) + shift
    h_hi, h_lo = _split(h)

    def proj3(lo, hi):
        b_hi = wh_ref[:, lo:hi]
        b_lo = wl_ref[:, lo:hi]
        return _dot(h_hi, b_hi) + (_dot(h_lo, b_hi) + _dot(h_hi, b_lo))

    def proj1(lo, hi):
        return _dot(h_hi, wh_ref[:, lo:hi])

    ua_ref[0] = proj3(OFF_A, OFF_A + W_A)
    gt_ref[0] = proj1(OFF_G, OFF_G + W_G)
    ub = proj1(OFF_B, OFF_B + W_B)
    qb_ref[0] = _rope_apply(ub[:, 0:256], tr_ref, 0, 256, HEAD_DIM // 2)
    kb_ref[0] = _rope_apply(ub[:, 256:512], tr_ref, 0, 256, HEAD_DIM // 2)
    vb_ref[0] = ub[:, 512:768]
    uc = proj1(OFF_C, OFF_C + W_C)
    qc_ref[0] = _rope_apply(uc[:, 0:256], ts_ref, 0, 256, ROPE_DIMS // 2)
    kc = _rope_apply(uc[:, 256:512], ts_ref, 0, 256, ROPE_DIMS // 2)
    kc_ref[0] = kc
    vc = uc[:, 512:768]
    vc_ref[0] = vc
    ui = proj3(OFF_I, OFF_I + W_I)
    qi_ref[0] = _rope_apply(ui[:, 0:256], ts_ref, 0, 256, ROPE_DIMS // 2)
    ki4 = _rope_apply(ui[:, 256:512], ts_ref, 0, 256, ROPE_DIMS // 2)
    ki_ref[0] = ki4
    wi_ref[0] = ui[:, 512:640]
    uqk = proj3(OFF_D, OFF_D + 512)
    qd_ref[0] = _rope_apply(uqk[:, 0:256], ts_ref, 0, 256, ROPE_DIMS // 2)
    kd = _rope_apply(uqk[:, 256:512], ts_ref, 0, 256, ROPE_DIMS // 2)
    kd_ref[0] = kd
    vd = proj1(OFF_D + 512, OFF_D + W_D)
    vd_ref[0] = vd
    if extra_refs:
        kcb_ref, vct_ref, kix_ref, kdb_ref, vdt_ref, kbar_ref = extra_refs
        kcb_ref[0] = kc.astype(BF16)
        vct_ref[0, 0] = vc.T.astype(BF16)
        kix_ref[0] = _indexer_key_operand(ki4)
        kdb_ref[0] = kd.astype(BF16)
        vdt_ref[0, 0] = vd.T.astype(BF16)
        kbar_ref[0, 0] = jnp.sum(kd, axis=0, keepdims=True) * (1.0 / MOBA_BLOCK)


def _regroup_w_in(w_in):
    def cols(name):
        lo, hi = _C[name]
        return w_in[:, lo:hi]
    ki4 = jnp.tile(cols('c_ki'), (1, 4))
    wi_pad = jnp.pad(cols('c_wi'), ((0, 0), (0, 124)))
    w = jnp.concatenate([
        cols('a_r'), cols('a_k'), cols('a_v'), cols('a_wl'), cols('a_al'),
        cols('a_g'), cols('b_g'), cols('c_g'), cols('d_g'),
        cols('b_q'), cols('b_k'), cols('b_v'),
        cols('c_q'), cols('c_k'), cols('c_v'),
        cols('c_qi'), ki4, wi_pad,
        cols('d_q'), cols('d_k'), cols('d_v')], axis=1)
    return _split(w)


def _layer_in(x, mod, g_pre, w_hi, w_lo, tab_ret, tab_std, tm, attention_operands=False):
    b, t, _ = x.shape
    r = mod.shape[1]
    if r == 1:
        mod_spec = pl.BlockSpec((1, 1, 3 * D_MODEL), lambda i, j: (i, 0, 0))
    else:
        mod_spec = pl.BlockSpec((1, tm, 3 * D_MODEL), lambda i, j: (i, j, 0))
    widths = (W_A, W_G, 256, 256, 256, 256, 256, 256, 256, 256, 128, 256, 256, 256)
    seq = lambda w: pl.BlockSpec((1, tm, w), lambda i, j: (i, j, 0))
    out_shape = tuple(jax.ShapeDtypeStruct((b, t, w), F32) for w in widths)
    out_specs = tuple(seq(w) for w in widths)
    if attention_operands:
        rows = jax.ShapeDtypeStruct((b, t, BR_W), BF16)
        cols = jax.ShapeDtypeStruct((b, t // tm, BR_W, tm), BF16)
        col_spec = pl.BlockSpec((1, 1, BR_W, tm), lambda i, j: (i, j, 0, 0))
        out_shape += (rows, cols, rows, rows, cols, jax.ShapeDtypeStruct((b, t // tm, 1, BR_W), F32))
        out_specs += (seq(BR_W), col_spec, seq(BR_W), seq(BR_W), col_spec,
                      pl.BlockSpec((1, 1, 1, BR_W), lambda i, j: (i, j, 0, 0)))
    return pl.pallas_call(
        _layer_in_kernel,
        grid=(b, t // tm),
        in_specs=[pl.BlockSpec((1, tm, D_MODEL), lambda i, j: (i, j, 0)),
                  mod_spec,
                  pl.BlockSpec((1, D_MODEL), lambda i, j: (0, 0)),
                  pl.BlockSpec((D_MODEL, W_ALL), lambda i, j: (0, 0), pipeline_mode=pl.Buffered(1)),
                  pl.BlockSpec((D_MODEL, W_ALL), lambda i, j: (0, 0), pipeline_mode=pl.Buffered(1)),
                  pl.BlockSpec((3, tm, 256), lambda i, j: (0, j, 0)),
                  pl.BlockSpec((3, tm, 256), lambda i, j: (0, j, 0))],
        out_specs=out_specs,
        out_shape=out_shape,
        compiler_params=_cparams(("arbitrary", "arbitrary")),
        name="layer_in",
    )(x, mod, g_pre.reshape(1, D_MODEL), w_hi, w_lo, tab_ret, tab_std)


def _rwkv_pre_math(ua, prev, mu, w0, w2h, w2l, a0, a2h, a2l, kkp, ka, rk, bd):
    xs = ua + (prev - ua) * mu
    r = xs[:, 0:256]
    k = xs[:, 256:512]
    v = xs[:, 512:768]
    wl = xs[:, 768:832]
    al = xs[:, 832:896]
    zw = w0 + _mm3(jnp.tanh(wl), w2h, w2l)
    w_log = -jax.nn.softplus(-zw) - 0.5
    decay = jnp.exp(-jnp.exp(w_log))
    a = jax.nn.sigmoid(a0 + _mm3(al, a2h, a2l))
    kq = k * kkp
    kk = kq * lax.rsqrt(_headsum(kq * kq, bd) + 1e-12)
    k2 = k * (1.0 + (a - 1.0) * ka)
    bonus = _headsum(r * k2 * rk, bd) * v
    return r, decay, k2, v, kk, kk * a, bonus


def _rwkv_pre_shift_kernel(ua_ref, up_ref, p0_ref, mu_ref, w0_ref, w2h_ref, w2l_ref, a0_ref, a2h_ref, a2l_ref,
                           kkp_ref, ka_ref, rk_ref, r_ref, w_ref, k_ref, v_ref, kk_ref, b_ref, bo_ref):
    j = pl.program_id(1)
    ua = ua_ref[0]
    tm = ua.shape[0]
    first = jnp.where(j == 0, p0_ref[0], up_ref[0, 7:8, :])
    row = lax.broadcasted_iota(I32, ua.shape, 0)
    prev = jnp.where(row == 0, first, pltpu.roll(ua, 1, 0))
    outs = _rwkv_pre_math(ua, prev, mu_ref[...], w0_ref[...], w2h_ref[...], w2l_ref[...], a0_ref[...],
                          a2h_ref[...], a2l_ref[...], kkp_ref[...], ka_ref[...], rk_ref[...], _head_ones())
    for o_ref, o in zip((r_ref, w_ref, k_ref, v_ref, kk_ref, b_ref, bo_ref), outs):
        o_ref[0] = o


def _rwkv_pre_rows_kernel(ua_ref, pv_ref, mu_ref, w0_ref, w2h_ref, w2l_ref, a0_ref, a2h_ref, a2l_ref,
                          kkp_ref, ka_ref, rk_ref, r_ref, w_ref, k_ref, v_ref, kk_ref, b_ref, bo_ref):
    outs = _rwkv_pre_math(ua_ref[...], pv_ref[...], mu_ref[...], w0_ref[...], w2h_ref[...], w2l_ref[...],
                          a0_ref[...], a2h_ref[...], a2l_ref[...], kkp_ref[...], ka_ref[...], rk_ref[...],
                          _head_ones())
    for o_ref, o in zip((r_ref, w_ref, k_ref, v_ref, kk_ref, b_ref, bo_ref), outs):
        o_ref[...] = o


def _rwkv_params(p):
    w2h, w2l = _split(p['a_w2'])
    a2h, a2l = _split(p['a_a2'])
    row = lambda v: v.reshape(1, -1)
    return (row(p['a_mu']), row(p['a_w0']), w2h, w2l, row(p['a_a0']), a2h, a2l,
            row(p['a_kk']), row(p['a_ka']), row(p['a_rk']))


def _rwkv_pre_prompt(ua, prev0, p, tm):
    b, t, _ = ua.shape
    prm = _rwkv_params(p)
    full = lambda a: pl.BlockSpec(a.shape, lambda i, j: (0,) * a.ndim)
    blk8 = tm // 8
    return pl.pallas_call(
        _rwkv_pre_shift_kernel,
        grid=(b, t // tm),
        in_specs=[pl.BlockSpec((1, tm, A_SHIFT_W), lambda i, j: (i, j, 0)),
                  pl.BlockSpec((1, 8, A_SHIFT_W), lambda i, j: (i, jnp.maximum(j * blk8 - 1, 0), 0)),
                  pl.BlockSpec((1, 1, A_SHIFT_W), lambda i, j: (i, 0, 0))] + [full(a) for a in prm],
        out_specs=tuple(pl.BlockSpec((1, tm, BR_W), lambda i, j: (i, j, 0)) for _ in range(7)),
        out_shape=tuple(jax.ShapeDtypeStruct((b, t, BR_W), F32) for _ in range(7)),
        compiler_params=_cparams(("arbitrary", "arbitrary")),
        name="rwkv_pre",
    )(ua, ua, prev0.reshape(b, 1, A_SHIFT_W), *prm)


def _rwkv_pre_rows(ua, prev, p):
    n = ua.shape[0]
    prm = _rwkv_params(p)
    full = lambda a: pl.BlockSpec(a.shape, lambda: (0,) * a.ndim)
    return pl.pallas_call(
        _rwkv_pre_rows_kernel,
        in_specs=[full(ua), full(prev)] + [full(a) for a in prm],
        out_specs=tuple(pl.BlockSpec((n, BR_W), lambda: (0, 0)) for _ in range(7)),
        out_shape=tuple(jax.ShapeDtypeStruct((n, BR_W), F32) for _ in range(7)),
        name="rwkv_pre_rows",
    )(ua, prev, *prm)


def _rwkv_scan_kernel(r_ref, w_ref, k_ref, v_ref, kk_ref, b_ref, s0_ref, y_ref, sout_ref,
                      s_scr, z_scr, be_scr, ga_scr, *, tc):
    c = pl.program_id(0)

    @pl.when(c == 0)
    def _():
        s_scr[...] = s0_ref[...]

    nb = s_scr.shape[0]
    bd = _head_ones()
    vi = lax.broadcasted_iota(I32, (HEAD_DIM, BR_W), 0)
    ci = lax.broadcasted_iota(I32, (HEAD_DIM, BR_W), 1)
    diag = ((ci % HEAD_DIM) == vi)[None]

    for bi in range(nb):
        kk_next = pltpu.roll(kk_ref[bi], tc - 1, 0)
        z_scr[bi] = w_ref[bi] * kk_next
        be_scr[bi] = _headsum(b_ref[bi] * kk_next, bd)
        ga_scr[bi] = _headsum(k_ref[bi] * kk_next, bd)

    def pair(i, carry):
        t = 2 * i
        row = lambda ref, tt: ref[:, pl.ds(tt, 1), :]
        s = s_scr[...]
        lhs = jnp.concatenate([
            (s * row(kk_ref, t)).astype(BF16),
            (s * row(z_scr, t)).astype(BF16),
            jnp.where(diag, row(v_ref, t), 0.0).astype(BF16),
            jnp.where(diag, row(v_ref, t + 1), 0.0).astype(BF16)], axis=1)
        res = _dot(lhs.reshape(nb * 4 * HEAD_DIM, BR_W), bd).reshape(nb, 4 * HEAD_DIM, BR_W)
        sk1 = res[:, 0:HEAD_DIM]
        vc1 = res[:, 2 * HEAD_DIM:3 * HEAD_DIM]
        vc2 = res[:, 3 * HEAD_DIM:4 * HEAD_DIM]
        sk2 = res[:, HEAD_DIM:2 * HEAD_DIM] - sk1 * row(be_scr, t) + vc1 * row(ga_scr, t)
        s1 = s * row(w_ref, t) - sk1 * row(b_ref, t) + vc1 * row(k_ref, t)
        s2 = s1 * row(w_ref, t + 1) - sk2 * row(b_ref, t + 1) + vc2 * row(k_ref, t + 1)
        s_scr[...] = s2
        q = jnp.concatenate([(s1 * row(r_ref, t)).astype(BF16), (s2 * row(r_ref, t + 1)).astype(BF16)], axis=1)
        yb = _dot(q.reshape(nb * 2 * HEAD_DIM, BR_W), bd).reshape(nb, 2 * HEAD_DIM, BR_W)
        y_ref[:, pl.ds(t, 1), :] = jnp.sum(jnp.where(diag, yb[:, :HEAD_DIM], 0.0), axis=1, keepdims=True)
        y_ref[:, pl.ds(t + 1, 1), :] = jnp.sum(jnp.where(diag, yb[:, HEAD_DIM:], 0.0), axis=1, keepdims=True)
        return carry

    lax.fori_loop(0, tc // 2, pair, 0, unroll=2)

    @pl.when(c == pl.num_programs(0) - 1)
    def _():
        sout_ref[...] = s_scr[...]


def _rwkv_scan(r, w, k, v, kk, bb, s0, tc):
    b, t, _ = r.shape
    assert tc % 2 == 0 and t % tc == 0
    seq = pl.BlockSpec((b, tc, BR_W), lambda c: (0, c, 0))
    st = pl.BlockSpec((b, HEAD_DIM, BR_W), lambda c: (0, 0, 0))
    return pl.pallas_call(
        functools.partial(_rwkv_scan_kernel, tc=tc),
        grid=(t // tc,),
        in_specs=[seq] * 6 + [st],
        out_specs=(seq, st),
        out_shape=(jax.ShapeDtypeStruct((b, t, BR_W), F32), jax.ShapeDtypeStruct((b, HEAD_DIM, BR_W), F32)),
        scratch_shapes=[pltpu.VMEM((b, HEAD_DIM, BR_W), F32)] + [pltpu.VMEM((b, tc, BR_W), F32)] * 3,
        compiler_params=_cparams(("arbitrary",)),
        name="rwkv_scan",
    )(r, w, k, v, kk, bb, s0)


def _wkv_to_scan_layout(s):
    b = s.shape[0]
    return jnp.transpose(s, (0, 2, 1, 3)).reshape(b, HEAD_DIM, BR_W)


def _wkv_from_scan_layout(s):
    b = s.shape[0]
    return jnp.transpose(s.reshape(b, HEAD_DIM, N_HEADS, HEAD_DIM), (0, 2, 1, 3))


def _rwkv_step_kernel(r_ref, w_ref, k_ref, vc_ref, kk_ref, b_ref, s_ref, y_ref, so_ref):
    s = s_ref[...]
    sk = jnp.sum(s * kk_ref[...], axis=-1, keepdims=True)
    s = s * w_ref[...] - sk * b_ref[...] + vc_ref[...] * k_ref[...]
    so_ref[...] = s
    y_ref[...] = jnp.sum(s * r_ref[...], axis=-1, keepdims=True)


def _rwkv_step(r, w, k, v, kk, bb, s0):
    b = r.shape[0]
    n = b * N_HEADS
    rowf = lambda a: a.reshape(n, 1, HEAD_DIM)
    full = lambda shape: pl.BlockSpec(shape, lambda: (0,) * len(shape))
    y, s = pl.pallas_call(
        _rwkv_step_kernel,
        in_specs=[full((n, 1, HEAD_DIM))] * 3 + [full((n, HEAD_DIM, 1))] + [full((n, 1, HEAD_DIM))] * 2
                 + [full((n, HEAD_DIM, HEAD_DIM))],
        out_specs=(full((n, HEAD_DIM, 1)), full((n, HEAD_DIM, HEAD_DIM))),
        out_shape=(jax.ShapeDtypeStruct((n, HEAD_DIM, 1), F32), jax.ShapeDtypeStruct((n, HEAD_DIM, HEAD_DIM), F32)),
        compiler_params=pltpu.CompilerParams(vmem_limit_bytes=VMEM_LIMIT_BYTES),
        name="rwkv_step",
    )(rowf(r), rowf(w), rowf(k), v.reshape(n, HEAD_DIM, 1), rowf(kk), rowf(bb), s0.reshape(n, HEAD_DIM, HEAD_DIM))
    return y.reshape(b, BR_W), s.reshape(b, N_HEADS, HEAD_DIM, HEAD_DIM)


def _ret_tables(c):
    log_g = jnp.log(1.0 - jnp.power(2.0, -5.0 - jnp.arange(N_HEADS, dtype=jnp.float32)))
    i = jnp.arange(c, dtype=jnp.float32)
    diff = i[:, None] - i[None, :]
    dmat = jnp.where(diff[None] >= 0, jnp.exp(jnp.maximum(diff, 0.0)[None] * log_g[:, None, None]), 0.0)
    dq = jnp.exp((i[:, None] + 1.0) * log_g[None, :])
    dk = jnp.exp((c - 1.0 - i)[:, None] * log_g[None, :])
    ds = jnp.exp(c * log_g)
    lanes = lambda a: jnp.repeat(a, HEAD_DIM, axis=-1)
    return dmat, lanes(dq), lanes(dk), lanes(ds[None, :])


def _ret_kernel(q_ref, k_ref, v_ref, s0_ref, dm_ref, dq_ref, dk_ref, ds_ref, gw_ref, gb_ref,
                y_ref, so_ref, s_scr):
    j = pl.program_id(1)

    @pl.when(j == 0)
    def _():
        s_scr[...] = s0_ref[0]

    q = q_ref[0]
    k = k_ref[0] * (HEAD_DIM ** -0.5)
    v = v_ref[0]
    s = s_scr[...]
    bd = _head_ones()
    qb = q.astype(BF16)
    kb = k.astype(BF16)
    vb = v.astype(BF16)
    y = _dot(qb, s.astype(BF16)) * dq_ref[...]
    for h in range(N_HEADS):
        hm = _head_mask(h, (1, BR_W))
        att = _dot_nt(jnp.where(hm, q, 0.0).astype(BF16), kb) * dm_ref[h]
        y = y + jnp.where(hm, _dot(att.astype(BF16), vb), 0.0)
    kd = (k * dk_ref[...]).T.astype(BF16)
    s_scr[...] = s * ds_ref[...] + bd.astype(F32) * _dot(kd, vb)
    mu = _headsum(y, bd) * (1.0 / HEAD_DIM)
    d = y - mu
    var = _headsum(d * d, bd) * (1.0 / HEAD_DIM)
    y_ref[0] = d * lax.rsqrt(var + RET_GN_EPS) * gw_ref[...] + gb_ref[...]

    @pl.when(j == pl.num_programs(1) - 1)
    def _():
        so_ref[0] = s_scr[...]


def _ret_state_embed(s):
    b = s.shape[0]
    eye = jnp.eye(N_HEADS, dtype=s.dtype)
    return jnp.einsum('bhde,hg->bhdge', s, eye).reshape(b, BR_W, BR_W)


def _ret_state_extract(s):
    b = s.shape[0]
    s4 = s.reshape(b, N_HEADS, HEAD_DIM, N_HEADS, HEAD_DIM)
    return jnp.stack([s4[:, h, :, h, :] for h in range(N_HEADS)], axis=1)


def _retention_prompt(q, k, v, s0, gn_w, gn_b, c):
    b, t, _ = q.shape
    dmat, dq, dk, ds = _ret_tables(c)
    seq = pl.BlockSpec((1, c, BR_W), lambda i, j: (i, j, 0))
    st = pl.BlockSpec((1, BR_W, BR_W), lambda i, j: (i, 0, 0))
    const = lambda a: pl.BlockSpec(a.shape, lambda i, j: (0,) * a.ndim)
    gw = gn_w.reshape(1, BR_W)
    gb = gn_b.reshape(1, BR_W)
    y, s = pl.pallas_call(
        _ret_kernel,
        grid=(b, t // c),
        in_specs=[seq, seq, seq, st, const(dmat), const(dq), const(dk), const(ds), const(gw), const(gb)],
        out_specs=(seq, st),
        out_shape=(jax.ShapeDtypeStruct((b, t, BR_W), F32), jax.ShapeDtypeStruct((b, BR_W, BR_W), F32)),
        scratch_shapes=[pltpu.VMEM((BR_W, BR_W), F32)],
        compiler_params=_cparams(("arbitrary", "arbitrary")),
        name="retention",
    )(q, k, v, _ret_state_embed(s0), dmat, dq, dk, ds, gw, gb)
    return y, _ret_state_extract(s)


def _ret_step_kernel(qc_ref, kc_ref, qr_ref, kr_ref, v_ref, s_ref, g_ref, gw_ref, gb_ref, y_ref, so_ref):
    s = s_ref[...]
    g = g_ref[...]
    v = v_ref[...]
    qk = jnp.sum(qr_ref[...] * kr_ref[...], axis=-1, keepdims=True)
    y = qk * v + jnp.sum(qc_ref[...] * s, axis=1, keepdims=True) * g
    so_ref[...] = s * g + kc_ref[...] * v
    mu = jnp.mean(y, axis=-1, keepdims=True)
    d = y - mu
    var = jnp.mean(d * d, axis=-1, keepdims=True)
    y_ref[...] = d * lax.rsqrt(var + RET_GN_EPS) * gw_ref[...] + gb_ref[...]


def _retention_step(q, k, v, s0, gn_w, gn_b):
    b = q.shape[0]
    n = b * N_HEADS
    ks = k * (HEAD_DIM ** -0.5)
    g = 1.0 - jnp.power(2.0, -5.0 - jnp.arange(N_HEADS, dtype=jnp.float32))
    g = jnp.tile(g, (b,)).reshape(n, 1, 1)
    gw = jnp.tile(gn_w.reshape(N_HEADS, HEAD_DIM), (b, 1)).reshape(n, 1, HEAD_DIM)
    gb = jnp.tile(gn_b.reshape(N_HEADS, HEAD_DIM), (b, 1)).reshape(n, 1, HEAD_DIM)
    col = lambda a: a.reshape(n, HEAD_DIM, 1)
    row = lambda a: a.reshape(n, 1, HEAD_DIM)
    args = (col(q), col(ks), row(q), row(ks), row(v), s0.reshape(n, HEAD_DIM, HEAD_DIM), g, gw, gb)
    full = lambda a: pl.BlockSpec(a.shape, lambda: (0,) * a.ndim)
    y, s = pl.pallas_call(
        _ret_step_kernel,
        in_specs=[full(a) for a in args],
        out_specs=(pl.BlockSpec((n, 1, HEAD_DIM), lambda: (0, 0, 0)),
                   pl.BlockSpec((n, HEAD_DIM, HEAD_DIM), lambda: (0, 0, 0))),
        out_shape=(jax.ShapeDtypeStruct((n, 1, HEAD_DIM), F32), jax.ShapeDtypeStruct((n, HEAD_DIM, HEAD_DIM), F32)),
        compiler_params=pltpu.CompilerParams(vmem_limit_bytes=VMEM_LIMIT_BYTES),
        name="retention_step",
    )(*args)
    return y.reshape(b, BR_W), s.reshape(b, N_HEADS, HEAD_DIM, HEAD_DIM)


INT_MIN = -2 ** 31


def _sort_key(s):
    s = jnp.where(s == 0.0, 0.0, s)
    bits = lax.bitcast_convert_type(s, I32)
    return jnp.where(bits < 0, bits ^ jnp.int32(0x7FFFFFFF), bits)


def _kth_largest_key(count_ge, shape, k):
    def bit_step(i, tb):
        cand = tb + jnp.left_shift(jnp.int32(1), 31 - i)
        return jnp.where(count_ge(cand) >= k, cand, tb)

    return lax.fori_loop(0, 32, bit_step, jnp.full(shape, INT_MIN, I32))


def _attn_queries(qf, qa_scr):
    qs = qf * (HEAD_DIM ** -0.5)
    for h in range(N_HEADS):
        qa_scr[h] = jnp.where(_head_mask(h, (1, BR_W)), qs, 0.0).astype(BF16)


def _flash_update_t(qa_scr, kb, vt, msk, stats, acc):
    new_stats = []
    parts = []
    for h in range(N_HEADS):
        m_old, l_old = stats[h]
        mk = msk[h] if isinstance(msk, (list, tuple)) else msk
        s = jnp.where(mk, _dot_nt(kb, qa_scr[h]), NEG)
        m_new = jnp.maximum(m_old, jnp.max(s, axis=0, keepdims=True))
        alpha = jnp.exp(m_old - m_new)
        p = jnp.exp(s - m_new)
        l_new = alpha * l_old + jnp.sum(p, axis=0, keepdims=True)
        rows = slice(h * HEAD_DIM, (h + 1) * HEAD_DIM)
        parts.append(acc[rows] * alpha + _dot(vt[rows], p.astype(BF16)))
        new_stats.append((m_new, l_new))
    return tuple(new_stats), jnp.concatenate(parts, axis=0)


def _flash_init_t(tq):
    stats = tuple((jnp.full((1, tq), NEG, F32), jnp.zeros((1, tq), F32)) for _ in range(N_HEADS))
    return stats, jnp.zeros((BR_W, tq), F32)


def _flash_finish_t(stats, acc):
    out = jnp.concatenate([acc[h * HEAD_DIM:(h + 1) * HEAD_DIM] / stats[h][1] for h in range(N_HEADS)], axis=0)
    return out.T


def _dsa_prompt_kernel(q_ref, qi_ref, wi_ref, kb_ref, vt_ref, kx_ref, o_ref,
                       key_scr, qs_scr, qa_scr, *, tq, kb, topk):
    qt = pl.program_id(1)
    nkc = (qt + 1) * (tq // kb)
    qpos = qt * tq + lax.broadcasted_iota(I32, (kb, tq), 1)
    kofs = lax.broadcasted_iota(I32, (kb, tq), 0)
    lane = lax.broadcasted_iota(I32, (tq, BR_W), 1)
    qi = qi_ref[0]
    for h in range(IDX_HEADS):
        qm = jnp.where(lane // IDX_DIM == h, qi, 0.0)
        rep = (qm + pltpu.roll(qm, IDX_DIM, 1)) + (pltpu.roll(qm, 2 * IDX_DIM, 1) + pltpu.roll(qm, 3 * IDX_DIM, 1))
        q_hi, q_lo = _split(rep)
        qs_scr[h] = jnp.where((lane < IDX_DIM) | ((lane >= 2 * IDX_DIM) & (lane < 3 * IDX_DIM)), q_hi,
                              jnp.where(lane < 2 * IDX_DIM, q_lo, jnp.zeros_like(q_lo)))
    _attn_queries(q_ref[0], qa_scr)
    w_t = wi_ref[0].T * (IDX_HEADS ** -0.5 * IDX_DIM ** -0.5)

    def score_chunk(c, carry):
        kx = kx_ref[0, pl.ds(pl.multiple_of(c * kb, kb), kb), :]
        d = _dot_nt(kx, qs_scr[...].reshape(IDX_HEADS * tq, BR_W))
        s = jnp.zeros((kb, tq), F32)
        for h in range(IDX_HEADS):
            s = s + jnp.maximum(d[:, h * tq:(h + 1) * tq], 0.0) * w_t[h:h + 1, :]
        s = jnp.where(c * kb + kofs <= qpos, s, -jnp.inf)
        key_scr[c] = _sort_key(s)
        return carry

    lax.fori_loop(0, nkc, score_chunk, 0)

    def counter(cmp):
        def count(cand):
            def body(c, acc):
                kc = key_scr[c].reshape(4, kb // 32, 8, tq)
                return acc + jnp.sum(jnp.where(cmp(kc, cand), 1.0, 0.0), axis=1)
            acc = lax.fori_loop(0, nkc, body, jnp.zeros((4, 8, tq), F32))
            return jnp.sum(jnp.sum(acc, axis=0), axis=0, keepdims=True)
        return count

    thr = _kth_largest_key(counter(lambda a, b: a >= b), (1, tq), float(topk))
    rem = float(topk) - counter(lambda a, b: a > b)(thr)
    lower = (lax.broadcasted_iota(I32, (kb, kb), 1) <= lax.broadcasted_iota(I32, (kb, kb), 0)).astype(BF16)

    def attn_chunk(c, carry):
        stats, acc, run = carry
        kc = key_scr[c]
        eq = kc == thr
        eqf = jnp.where(eq, 1.0, 0.0)
        pre = _dot(lower, eqf.astype(BF16)) + run
        sel = (kc > thr) | (eq & (pre <= rem))
        msk = sel & (c * kb + kofs <= qpos)
        off = pl.multiple_of(c * kb, kb)
        stats, acc = _flash_update_t(qa_scr, kb_ref[0, pl.ds(off, kb), :], vt_ref[0, c], msk, stats, acc)
        return stats, acc, run + jnp.sum(eqf, axis=0, keepdims=True)

    stats, acc = _flash_init_t(tq)
    stats, acc, _ = lax.fori_loop(0, nkc, attn_chunk, (stats, acc, jnp.zeros((1, tq), F32)))
    o_ref[0] = _flash_finish_t(stats, acc)


def _dsa_prompt(q, qi, wi, kb16, vt16, kx, tq=256, kb=256):
    b, t, _ = q.shape
    topk = min(DSA_TOPK_MAX, t // 4)
    assert tq >= topk and tq == kb
    tile = lambda w: pl.BlockSpec((1, tq, w), lambda i, j: (i, j, 0))
    full = pl.BlockSpec((1, t, BR_W), lambda i, j: (i, 0, 0))
    return pl.pallas_call(
        functools.partial(_dsa_prompt_kernel, tq=tq, kb=kb, topk=topk),
        grid=(b, t // tq),
        in_specs=[tile(BR_W), tile(BR_W), tile(128), full,
                  pl.BlockSpec((1, t // kb, BR_W, kb), lambda i, j: (i, 0, 0, 0)), full],
        out_specs=tile(BR_W),
        out_shape=jax.ShapeDtypeStruct((b, t, BR_W), F32),
        scratch_shapes=[pltpu.VMEM((t // kb, kb, tq), I32),
                        pltpu.VMEM((IDX_HEADS, tq, BR_W), BF16), pltpu.VMEM((N_HEADS, tq, BR_W), BF16)],
        compiler_params=_cparams(("arbitrary", "arbitrary")),
        name="dsa_prompt",
    )(q, qi, wi, kb16, vt16, kx)


def _top_blocks_t(g, row, limit, nsel):
    g = jnp.where(row < limit, g, -jnp.inf)
    selm = jnp.zeros(g.shape, jnp.bool_)
    for _ in range(nsel):
        mx = jnp.max(g, axis=0, keepdims=True)
        idx = jnp.min(jnp.where(g == mx, row, jnp.int32(1 << 20)), axis=0, keepdims=True)
        pick = row == idx
        selm = selm | (pick & (row < limit))
        g = jnp.where(pick, -jnp.inf, g)
    return selm


def _moba_prompt_kernel(q_ref, kbar_ref, kb_ref, vt_ref, o_ref, qa_scr, sel_scr, *, nsel):
    qt = pl.program_id(1)
    tq = MOBA_BLOCK
    qf = q_ref[0]
    kbar = kbar_ref[0]
    row = lax.broadcasted_iota(I32, (kbar.shape[0], tq), 0)
    for h in range(N_HEADS):
        g = _mm3_nt(kbar, jnp.where(_head_mask(h, (1, BR_W)), qf, 0.0))
        sel_scr[h] = jnp.where(_top_blocks_t(g, row, qt, nsel), 1.0, 0.0)
    _attn_queries(qf, qa_scr)

    tri = lax.broadcasted_iota(I32, (tq, tq), 0) <= lax.broadcasted_iota(I32, (tq, tq), 1)
    off = pl.multiple_of(qt * tq, tq)
    stats, acc = _flash_init_t(tq)
    stats, acc = _flash_update_t(qa_scr, kb_ref[0, pl.ds(off, tq), :], vt_ref[0, qt], tri, stats, acc)

    def past(n, carry):
        stats, acc = carry
        o = pl.multiple_of(n * tq, tq)
        allow = [sel_scr[h, pl.ds(n, 1), :] > 0.0 for h in range(N_HEADS)]
        return _flash_update_t(qa_scr, kb_ref[0, pl.ds(o, tq), :], vt_ref[0, n], allow, stats, acc)

    stats, acc = lax.fori_loop(0, qt, past, (stats, acc))
    o_ref[0] = _flash_finish_t(stats, acc)


def _moba_prompt(q, kbar, kb16, vt16):
    b, t, _ = q.shape
    nb = t // MOBA_BLOCK
    nsel = min(MOBA_TOPK, nb - 1)
    nbp = -(-nb // 16) * 16
    kbar = jnp.pad(kbar, ((0, 0), (0, nbp - nb), (0, 0)))
    tile = pl.BlockSpec((1, MOBA_BLOCK, BR_W), lambda i, j: (i, j, 0))
    full = pl.BlockSpec((1, t, BR_W), lambda i, j: (i, 0, 0))
    return pl.pallas_call(
        functools.partial(_moba_prompt_kernel, nsel=nsel),
        grid=(b, nb),
        in_specs=[tile, pl.BlockSpec((1, nbp, BR_W), lambda i, j: (i, 0, 0)), full,
                  pl.BlockSpec((1, nb, BR_W, MOBA_BLOCK), lambda i, j: (i, 0, 0, 0))],
        out_specs=tile,
        out_shape=jax.ShapeDtypeStruct((b, t, BR_W), F32),
        scratch_shapes=[pltpu.VMEM((N_HEADS, MOBA_BLOCK, BR_W), BF16),
                        pltpu.VMEM((N_HEADS, nbp, MOBA_BLOCK), F32)],
        compiler_params=_cparams(("arbitrary", "arbitrary")),
        name="moba_prompt",
    )(q, kbar, kb16, vt16)


def _layer_out_kernel(x_ref, mod_ref, gpre_ref, gpost_ref, ya_ref, bo_ref, yb_ref, yc_ref, yd_ref, gt_ref,
                      lnw_ref, lnb_ref, wb_ref, wm_ref, wo_ref, o_ref):
    x = x_ref[0]
    y = x * lax.rsqrt(jnp.mean(x * x, axis=-1, keepdims=True) + RMS_EPS) * gpre_ref[...]
    shift = mod_ref[0, :, 0:D_MODEL]
    scale = mod_ref[0, :, D_MODEL:2 * D_MODEL]
    gate = mod_ref[0, :, 2 * D_MODEL:3 * D_MODEL]
    hb = (y * (1.0 + scale) + shift).astype(BF16)
    bd = _head_ones()
    ya = ya_ref[0]
    mu = _headsum(ya, bd) * (1.0 / HEAD_DIM)
    d = ya - mu
    var = _headsum(d * d, bd) * (1.0 / HEAD_DIM)
    ya = d * lax.rsqrt(var + RWKV_GN_EPS) * lnw_ref[...] + lnb_ref[...] + bo_ref[0]
    outs = (ya, yb_ref[0], yc_ref[0], yd_ref[0])
    merged = jnp.zeros(x.shape, F32)
    for n in range(4):
        o = outs[n] * jax.nn.silu(gt_ref[0, :, n * BR_W:(n + 1) * BR_W])
        merged = merged + jax.nn.sigmoid(_dot(hb, wm_ref[n])) * _dot(o.astype(BF16), wb_ref[n])
    z = _dot(merged.astype(BF16), wo_ref[...])
    z = z * lax.rsqrt(jnp.mean(z * z, axis=-1, keepdims=True) + RMS_EPS) * gpost_ref[...]
    o_ref[0] = x + gate * z


def _layer_out(x, mod, p, ya, bonus, yb, yc, yd, gates, tm):
    b, t, _ = x.shape
    r = mod.shape[1]
    if r == 1:
        mod_spec = pl.BlockSpec((1, 1, 3 * D_MODEL), lambda i, j: (i, 0, 0))
    else:
        mod_spec = pl.BlockSpec((1, tm, 3 * D_MODEL), lambda i, j: (i, j, 0))
    tile = lambda w: pl.BlockSpec((1, tm, w), lambda i, j: (i, j, 0))
    row = lambda v: v.reshape(1, -1)
    const = lambda shape: pl.BlockSpec(shape, lambda i, j: (0,) * len(shape), pipeline_mode=pl.Buffered(1))
    vec = lambda n: pl.BlockSpec((1, n), lambda i, j: (0, 0))
    return pl.pallas_call(
        _layer_out_kernel,
        grid=(b, t // tm),
        in_specs=[tile(D_MODEL), mod_spec, vec(D_MODEL), vec(D_MODEL),
                  tile(BR_W), tile(BR_W), tile(BR_W), tile(BR_W), tile(BR_W), tile(D_MODEL),
                  vec(BR_W), vec(BR_W),
                  const((4, BR_W, D_MODEL)), const((4, D_MODEL, D_MODEL)), const((D_MODEL, D_MODEL))],
        out_specs=tile(D_MODEL),
        out_shape=jax.ShapeDtypeStruct((b, t, D_MODEL), F32),
        compiler_params=_cparams(("arbitrary", "arbitrary")),
        name="layer_out",
    )(x, mod, row(p['g_pre']), row(p['g_post']), ya, bonus, yb, yc, yd, gates,
      row(p['a_ln_w']), row(p['a_ln_b']),
      p['w_branch'].astype(BF16), p['w_merge'].astype(BF16), p['w_out'].astype(BF16))


PAGES_PER_STEP = 16


def _page_specs(block, g_count):
    tail = (0,) * (len(block) - 1)

    def spec(g):
        return pl.BlockSpec(block, lambda i, j, pt: (pt[i, j * g_count + g],) + tail)
    return [spec(g) for g in range(g_count)]


def _head_rows(x):
    hr = lax.broadcasted_iota(I32, (N_HEADS, BR_W), 0)
    hl = lax.broadcasted_iota(I32, (N_HEADS, BR_W), 1) // HEAD_DIM
    return jnp.where(hr == hl, x, 0.0), hr == hl


def _dsa_scores_kernel(pt_ref, qm_ref, wi_ref, kn_ref, *refs, g_count):
    page_refs = refs[:g_count]
    o_ref, on_ref = refs[g_count], refs[g_count + 1]
    j = pl.program_id(1)
    qm = qm_ref[0]
    q_hi, q_lo = _split(qm)
    w = wi_ref[0] * (IDX_HEADS ** -0.5)

    def combine(d):
        return jnp.sum(jnp.maximum(d * (IDX_DIM ** -0.5), 0.0) * w, axis=0, keepdims=True)

    for g in range(g_count):
        k_hi, k_lo = _split(page_refs[g][0])
        d = _dot_nt(q_hi, k_hi) + (_dot_nt(q_lo, k_hi) + _dot_nt(q_hi, k_lo))
        o_ref[0, g] = combine(d)

    @pl.when(j == 0)
    def _():
        dn = jnp.sum(qm * kn_ref[0], axis=-1, keepdims=True)
        on_ref[0] = jnp.broadcast_to(combine(dn), (1, 128))


def _dsa_scores(qi, wi, ki_new, pool, page_table):
    b, n_pages = page_table.shape
    g_count = min(PAGES_PER_STEP, n_pages)
    qm = qi.reshape(b, IDX_HEADS, IDX_DIM)
    wcol = wi.reshape(b, IDX_HEADS, 1)
    kn = ki_new.reshape(b, 1, IDX_DIM)
    grid_spec = pltpu.PrefetchScalarGridSpec(
        num_scalar_prefetch=1,
        grid=(b, n_pages // g_count),
        in_specs=[pl.BlockSpec((1, IDX_HEADS, IDX_DIM), lambda i, j, pt: (i, 0, 0)),
                  pl.BlockSpec((1, IDX_HEADS, 1), lambda i, j, pt: (i, 0, 0)),
                  pl.BlockSpec((1, 1, IDX_DIM), lambda i, j, pt: (i, 0, 0))]
                 + _page_specs((1, PAGE_SIZE, IDX_DIM), g_count),
        out_specs=(pl.BlockSpec((1, g_count, 1, PAGE_SIZE), lambda i, j, pt: (i, j, 0, 0)),
                   pl.BlockSpec((1, 1, 128), lambda i, j, pt: (i, 0, 0))))
    sc, sc_new = pl.pallas_call(
        functools.partial(_dsa_scores_kernel, g_count=g_count),
        grid_spec=grid_spec,
        out_shape=(jax.ShapeDtypeStruct((b, n_pages, 1, PAGE_SIZE), F32),
                   jax.ShapeDtypeStruct((b, 1, 128), F32)),
        compiler_params=_cparams(("arbitrary", "arbitrary")),
        name="dsa_scores",
    )(page_table, qm, wcol, kn, *([pool] * g_count))
    return sc.reshape(b, n_pages * PAGE_SIZE), sc_new[:, 0, 0:1]


def _topk_rows_kernel(s_ref, o_ref, key_scr, *, n_valid, topk):
    nblk, rows, _ = s_ref.shape
    col = lax.broadcasted_iota(I32, (rows, 128), 1)

    def to_key(j, carry):
        s = jnp.where(j * 128 + col < n_valid, s_ref[j], -jnp.inf)
        key_scr[j] = _sort_key(s)
        return carry

    lax.fori_loop(0, nblk, to_key, 0)

    def counter(cmp):
        def count(cand):
            body = lambda j, acc: acc + jnp.where(cmp(key_scr[j], cand), 1.0, 0.0)
            acc = lax.fori_loop(0, nblk, body, jnp.zeros((rows, 128), F32))
            return jnp.sum(acc, axis=-1, keepdims=True)
        return count

    tb = _kth_largest_key(counter(lambda a, b: a >= b), (rows, 128), float(topk))
    rem = float(topk) - counter(lambda a, b: a > b)(tb)
    upper = (lax.broadcasted_iota(I32, (128, 128), 0) <= lax.broadcasted_iota(I32, (128, 128), 1)).astype(BF16)

    def select(j, run):
        kc = key_scr[j]
        eq = kc == tb
        eqf = jnp.where(eq, 1.0, 0.0)
        pre = _dot(eqf.astype(BF16), upper) + run
        sel = ((kc > tb) | (eq & (pre <= rem))) & (j * 128 + col < n_valid)
        o_ref[j] = jnp.where(sel, 1.0, 0.0)
        return run + jnp.sum(eqf, axis=-1, keepdims=True)

    lax.fori_loop(0, nblk, select, jnp.zeros((rows, 1), F32))


def _topk_rows(scores, topk):
    rows, n = scores.shape
    nblk = -(-n // 128)
    s = jnp.pad(scores, ((0, 0), (0, nblk * 128 - n)))
    s = jnp.transpose(s.reshape(rows, nblk, 128), (1, 0, 2))
    m = pl.pallas_call(
        functools.partial(_topk_rows_kernel, n_valid=n, topk=topk),
        in_specs=[pl.BlockSpec((nblk, rows, 128), lambda: (0, 0, 0))],
        out_specs=pl.BlockSpec((nblk, rows, 128), lambda: (0, 0, 0)),
        out_shape=jax.ShapeDtypeStruct((nblk, rows, 128), F32),
        scratch_shapes=[pltpu.VMEM((nblk, rows, 128), I32)],
        compiler_params=pltpu.CompilerParams(vmem_limit_bytes=VMEM_LIMIT_BYTES),
        name="topk_rows",
    )(s)
    return jnp.transpose(m, (1, 0, 2)).reshape(rows, nblk * 128)[:, :n]


def _dsa_attn_kernel(pt_ref, q_ref, kn_ref, vn_ref, mn_ref, msk_ref, *refs, g_count):
    k_refs = refs[:g_count]
    v_refs = refs[g_count:2 * g_count]
    o_ref = refs[2 * g_count]
    m_scr, l_scr, acc_scr = refs[2 * g_count + 1:]
    j = pl.program_id(1)
    qbd, hsel = _head_rows(q_ref[0])

    @pl.when(j == 0)
    def _():
        sn = jnp.sum(qbd * kn_ref[0], axis=-1, keepdims=True) * (HEAD_DIM ** -0.5)
        ok = mn_ref[0][:, 0:1] > 0.0
        m_scr[...] = jnp.broadcast_to(jnp.where(ok, sn, NEG), (N_HEADS, 128))
        l_scr[...] = jnp.broadcast_to(jnp.where(ok, 1.0, 0.0), (N_HEADS, 128))
        acc_scr[...] = jnp.where(ok, jnp.broadcast_to(vn_ref[0], (N_HEADS, BR_W)), 0.0)

    kcat = jnp.concatenate([r[0] for r in k_refs], axis=0).astype(BF16)
    vcat = jnp.concatenate([r[0] for r in v_refs], axis=0).astype(BF16)
    mk = msk_ref[0] > 0.0
    s = jnp.where(mk, _dot_nt(qbd.astype(BF16), kcat) * (HEAD_DIM ** -0.5), NEG)
    m_old = m_scr[:, 0:1]
    m_new = jnp.maximum(m_old, jnp.max(s, axis=-1, keepdims=True))
    alpha = jnp.exp(m_old - m_new)
    p = jnp.where(mk, jnp.exp(s - m_new), 0.0)
    l_new = alpha * l_scr[:, 0:1] + jnp.sum(p, axis=-1, keepdims=True)
    acc = alpha * acc_scr[...] + _dot(p.astype(BF16), vcat)
    m_scr[...] = jnp.broadcast_to(m_new, (N_HEADS, 128))
    l_scr[...] = jnp.broadcast_to(l_new, (N_HEADS, 128))
    acc_scr[...] = acc

    @pl.when(j == pl.num_programs(1) - 1)
    def _():
        o_ref[0] = jnp.sum(jnp.where(hsel, acc / l_new, 0.0), axis=0, keepdims=True)


def _dsa_attn(q, k_new, v_new, mask, k_pool, v_pool, page_table):
    b, n_pages = page_table.shape
    g_count = min(PAGES_PER_STEP, n_pages)
    past = n_pages * PAGE_SIZE
    row = lambda a: a.reshape(b, 1, BR_W)
    m_new = jnp.broadcast_to(mask[:, past:past + 1], (b, 128)).reshape(b, 1, 128)
    m_past = mask[:, :past].reshape(b, 1, past)
    rspec = pl.BlockSpec((1, 1, BR_W), lambda i, j, pt: (i, 0, 0))
    grid_spec = pltpu.PrefetchScalarGridSpec(
        num_scalar_prefetch=1,
        grid=(b, n_pages // g_count),
        in_specs=[rspec, rspec, rspec,
                  pl.BlockSpec((1, 1, 128), lambda i, j, pt: (i, 0, 0)),
                  pl.BlockSpec((1, 1, g_count * PAGE_SIZE), lambda i, j, pt: (i, 0, j))]
                 + _page_specs((1, PAGE_SIZE, BR_W), g_count) * 2,
        out_specs=rspec,
        scratch_shapes=[pltpu.VMEM((N_HEADS, 128), F32), pltpu.VMEM((N_HEADS, 128), F32),
                        pltpu.VMEM((N_HEADS, BR_W), F32)])
    out = pl.pallas_call(
        functools.partial(_dsa_attn_kernel, g_count=g_count),
        grid_spec=grid_spec,
        out_shape=jax.ShapeDtypeStruct((b, 1, BR_W), F32),
        compiler_params=_cparams(("arbitrary", "arbitrary")),
        name="dsa_attn",
    )(page_table, row(q), row(k_new), row(v_new), m_new, m_past, *([k_pool] * g_count), *([v_pool] * g_count))
    return out.reshape(b, BR_W)


def _kbar_kernel(pt_ref, *refs, g_count):
    page_refs = refs[:g_count]
    o_ref = refs[g_count]
    for g2 in range(g_count // 2):
        s = (jnp.sum(page_refs[2 * g2][0].astype(F32), axis=0, keepdims=True)
             + jnp.sum(page_refs[2 * g2 + 1][0].astype(F32), axis=0, keepdims=True))
        o_ref[0, g2] = s * (1.0 / MOBA_BLOCK)


def _moba_kbar(k_pool, page_table):
    b, n_pages = page_table.shape
    g_count = min(PAGES_PER_STEP, n_pages)
    grid_spec = pltpu.PrefetchScalarGridSpec(
        num_scalar_prefetch=1,
        grid=(b, n_pages // g_count),
        in_specs=_page_specs((1, PAGE_SIZE, BR_W), g_count),
        out_specs=pl.BlockSpec((1, g_count // 2, 1, BR_W), lambda i, j, pt: (i, j, 0, 0)))
    out = pl.pallas_call(
        functools.partial(_kbar_kernel, g_count=g_count),
        grid_spec=grid_spec,
        out_shape=jax.ShapeDtypeStruct((b, n_pages // 2, 1, BR_W), F32),
        compiler_params=_cparams(("arbitrary", "arbitrary")),
        name="moba_kbar",
    )(page_table, *([k_pool] * g_count))
    return out.reshape(b, n_pages // 2, BR_W)


def _moba_gate_kernel(q_ref, kbar_ref, o_ref, *, n_past, nsel):
    qbd, _ = _head_rows(q_ref[0])
    g = _mm3_nt(qbd, kbar_ref[0])
    col = lax.broadcasted_iota(I32, (N_HEADS, 128), 1)
    g = jnp.where(col < n_past, g, -jnp.inf)
    out = jnp.full((N_HEADS, 128), -1, I32)
    for i in range(nsel):
        mx = jnp.max(g, axis=-1, keepdims=True)
        idx = jnp.min(jnp.where(g == mx, col, jnp.int32(1 << 20)), axis=-1, keepdims=True)
        ok = idx < n_past
        out = jnp.where(col == i, jnp.where(ok, idx, -1), out)
        g = jnp.where(col == idx, -jnp.inf, g)
    o_ref[0] = out


def _moba_gate(q, kbar, nsel):
    b, n_past, _ = kbar.shape
    kb = jnp.pad(kbar, ((0, 0), (0, 128 - n_past), (0, 0)))
    out = pl.pallas_call(
        functools.partial(_moba_gate_kernel, n_past=n_past, nsel=nsel),
        grid=(b,),
        in_specs=[pl.BlockSpec((1, 1, BR_W), lambda i: (i, 0, 0)),
                  pl.BlockSpec((1, 128, BR_W), lambda i: (i, 0, 0))],
        out_specs=pl.BlockSpec((1, N_HEADS, 128), lambda i: (i, 0, 0)),
        out_shape=jax.ShapeDtypeStruct((b, N_HEADS, 128), I32),
        compiler_params=_cparams(("arbitrary",)),
        name="moba_gate",
    )(q.reshape(b, 1, BR_W), kb)
    return out[:, :, :nsel]


def _moba_attn_kernel(sel_ref, pt_ref, q_ref, kn_ref, vn_ref, *refs, nsel):
    n_pg = 2 * nsel
    k_refs = refs[:n_pg]
    v_refs = refs[n_pg:2 * n_pg]
    o_ref = refs[2 * n_pg]
    bi = pl.program_id(0)
    h = pl.program_id(1)
    hm = lax.broadcasted_iota(I32, (1, BR_W), 1) // HEAD_DIM == h
    qh = jnp.where(hm, q_ref[0], 0.0)
    sn = jnp.sum(qh * kn_ref[0], axis=-1, keepdims=True) * (HEAD_DIM ** -0.5)
    kcat = jnp.concatenate([r[0] for r in k_refs], axis=0).astype(BF16)
    vcat = jnp.concatenate([r[0] for r in v_refs], axis=0).astype(BF16)
    s = _dot_nt(qh.astype(BF16), kcat) * (HEAD_DIM ** -0.5)
    blk = lax.broadcasted_iota(I32, s.shape, 1) // MOBA_BLOCK
    mk = jnp.zeros(s.shape, jnp.bool_)
    for i in range(nsel):
        mk = mk | (blk == jnp.where(sel_ref[(bi * N_HEADS + h) * nsel + i] >= 0, i, -1))
    s = jnp.where(mk, s, NEG)
    m = jnp.maximum(jnp.max(s, axis=-1, keepdims=True), sn)
    p = jnp.where(mk, jnp.exp(s - m), 0.0)
    pn = jnp.exp(sn - m)
    out = (_dot(p.astype(BF16), vcat) + pn * vn_ref[0]) / (jnp.sum(p, axis=-1, keepdims=True) + pn)

    @pl.when(h == 0)
    def _():
        o_ref[0] = jnp.zeros((1, BR_W), F32)

    o_ref[0] = o_ref[0] + jnp.where(hm, out, 0.0)


def _moba_attn(q, k_new, v_new, sel, k_pool, v_pool, page_table):
    b, n_pages = page_table.shape
    nsel = sel.shape[-1]
    row = lambda a: a.reshape(b, 1, BR_W)

    def pspec(i, half):
        def imap(bi, h, sel_ref, pt):
            blk = jnp.maximum(sel_ref[(bi * N_HEADS + h) * nsel + i], 0)
            return (pt[bi, 2 * blk + half], 0, 0)
        return pl.BlockSpec((1, PAGE_SIZE, BR_W), imap)

    pages = [pspec(i, half) for i in range(nsel) for half in range(2)]
    rspec = pl.BlockSpec((1, 1, BR_W), lambda bi, h, s, pt: (bi, 0, 0))
    grid_spec = pltpu.PrefetchScalarGridSpec(
        num_scalar_prefetch=2,
        grid=(b, N_HEADS),
        in_specs=[rspec, rspec, rspec] + pages + pages,
        out_specs=rspec)
    out = pl.pallas_call(
        functools.partial(_moba_attn_kernel, nsel=nsel),
        grid_spec=grid_spec,
        out_shape=jax.ShapeDtypeStruct((b, 1, BR_W), F32),
        compiler_params=_cparams(("arbitrary", "arbitrary")),
        name="moba_attn",
    )(sel.reshape(-1), page_table, row(q), row(k_new), row(v_new),
      *([k_pool] * (2 * nsel)), *([v_pool] * (2 * nsel)))
    return out.reshape(b, BR_W)


def _sample_layer(x, c, p, a_shift, a_wkv, b_ret, ck_pool, cv_pool, cki_pool, dk_pool, dv_pool, page_table):
    b = x.shape[0]
    n_pages = page_table.shape[1]
    past = n_pages * PAGE_SIZE
    assert x.shape[1] == 1 and past % MOBA_BLOCK == 0
    pos = jnp.full((b,), past, I32)
    tab_ret = _rope_tables(pos, HEAD_DIM, RET_THETA, BR_W)
    tab_std = _rope_tables(pos, ROPE_DIMS, ROPE_THETA, BR_W)
    w_hi, w_lo = _regroup_w_in(p['w_in'])
    mod = _ada(c, p['w_ada'], p['b_ada']).reshape(1, b, 3 * D_MODEL)
    xr = x.reshape(1, b, D_MODEL)
    outs = _layer_in(xr, mod, p['g_pre'], w_hi, w_lo, tab_ret, tab_std, b)
    (ua, gates, qb, kb, vb, qc, kc, vc, qi, ki4, wi, qd, kd, vd) = [o[0] for o in outs]
    r, w, k2, v, kk, bb, bonus = _rwkv_pre_rows(ua, a_shift, p)
    ya, wkv = _rwkv_step(r, w, k2, v, kk, bb, a_wkv)
    yb, ret = _retention_step(qb, kb, vb, b_ret, p['b_gn_w'], p['b_gn_b'])
    ki = ki4[:, :IDX_DIM]
    sc_past, sc_new = _dsa_scores(qi, wi[:, :IDX_HEADS], ki, cki_pool, page_table)
    total = past + 1
    mask = _topk_rows(jnp.concatenate([sc_past, sc_new], axis=1), min(DSA_TOPK_MAX, total // 4))
    yc = _dsa_attn(qc, kc, vc, mask, ck_pool, cv_pool, page_table)
    n_past_blocks = past // MOBA_BLOCK
    nsel = min(MOBA_TOPK, n_past_blocks)
    if nsel > 0:
        sel = _moba_gate(qd, _moba_kbar(dk_pool, page_table), nsel)
        yd = _moba_attn(qd, kd, vd, sel, dk_pool, dv_pool, page_table)
    else:
        yd = vd
    row = lambda a: a.reshape(1, b, -1)
    x_new = _layer_out(xr, mod, p, row(ya), row(bonus), row(yb), row(yc), row(yd), gates.reshape(1, b, -1), b)
    heads = lambda a: a.reshape(b, 1, N_HEADS, HEAD_DIM)
    new_state = (ua, wkv, ret, heads(kc), heads(vc), ki.reshape(b, 1, IDX_DIM), heads(kd), heads(vd))
    return x_new.reshape(b, 1, D_MODEL), new_state


def _prompt_layer(x, c, p):
    b, t, _ = x.shape
    pos = jnp.arange(t)
    tab_ret = _rope_tables(pos, HEAD_DIM, RET_THETA, BR_W)
    tab_std = _rope_tables(pos, ROPE_DIMS, ROPE_THETA, BR_W)
    w_hi, w_lo = _regroup_w_in(p['w_in'])
    mod = _ada(c, p['w_ada'], p['b_ada']).reshape(b, 1, 3 * D_MODEL)
    tm = MOBA_BLOCK
    assert t % tm == 0
    (ua, gates, qb, kb, vb, qc, kc, vc, qi, ki4, wi, qd, kd, vd, kcb, vct, kix, kdb, vdt, kbar) = _layer_in(
        x, mod, p['g_pre'], w_hi, w_lo, tab_ret, tab_std, tm, attention_operands=True)
    r, w, k2, v, kk, bb, bonus = _rwkv_pre_prompt(ua, jnp.zeros((b, A_SHIFT_W), F32), p, tm)
    ya, wkv = _rwkv_scan(r, w, k2, v, kk, bb, jnp.zeros((b, HEAD_DIM, BR_W), F32), min(64, t))
    yb, ret = _retention_prompt(qb, kb, vb, jnp.zeros((b, N_HEADS, HEAD_DIM, HEAD_DIM), F32),
                                p['b_gn_w'], p['b_gn_b'], min(256, t))
    yc = _dsa_prompt(qc, qi, wi, kcb, vct, kix)
    yd = _moba_prompt(qd, kbar.reshape(b, t // tm, BR_W), kdb, vdt)
    x_new = _layer_out(x, mod, p, ya, bonus, yb, yc, yd, gates, tm)
    heads = lambda a: a.reshape(b, t, N_HEADS, HEAD_DIM)
    new_state = (ua[:, -1], _wkv_from_scan_layout(wkv), ret, heads(kc), heads(vc), ki4[..., :IDX_DIM],
                 heads(kd), heads(vd))
    return x_new, new_state


_PARAM_NAMES = ('w_ada', 'b_ada', 'g_pre', 'g_post', 'w_in', 'a_mu', 'a_w0', 'a_w2', 'a_a0', 'a_a2', 'a_kk', 'a_ka',
                'a_rk', 'a_ln_w', 'a_ln_b', 'b_gn_w', 'b_gn_b', 'w_branch', 'w_merge', 'w_out')


def kernel(x_prompt, x_sample, c_prompt, c_sample, state_a_shift, state_a_wkv, state_b_ret, cache_c_k, cache_c_v,
           cache_c_kidx, cache_d_k, cache_d_v, page_table, w_ada, b_ada, g_pre, g_post, w_in, a_mu, a_w0, a_w2,
           a_a0, a_a2, a_kk, a_ka, a_rk, a_ln_w, a_ln_b, b_gn_w, b_gn_b, w_branch, w_merge, w_out):
    stacked = dict(zip(_PARAM_NAMES, (w_ada, b_ada, g_pre, g_post, w_in, a_mu, a_w0, a_w2, a_a0, a_a2, a_kk, a_ka,
                                      a_rk, a_ln_w, a_ln_b, b_gn_w, b_gn_b, w_branch, w_merge, w_out)))
    depth = w_in.shape[0]
    n_pool = cache_c_k.shape[1]
    fold = lambda a: a.reshape(depth * n_pool, PAGE_SIZE, -1)
    ck, cv, cki, dk, dv = (fold(a) for a in (cache_c_k, cache_c_v, cache_c_kidx, cache_d_k, cache_d_v))
    xp, xs = x_prompt, x_sample
    p_new, s_new = [], []
    for l in range(depth):
        p = {name: val[l] for name, val in stacked.items()}
        xp, st_p = _prompt_layer(xp, c_prompt, p)
        xs, st_s = _sample_layer(xs, c_sample, p, state_a_shift[l], state_a_wkv[l], state_b_ret[l],
                                 ck, cv, cki, dk, dv, page_table + l * n_pool)
        p_new.append(st_p)
        s_new.append(st_s)
    stack = lambda states, i: jnp.stack([s[i] for s in states])
    return ((xp, xs) + tuple(stack(p_new, i) for i in range(8)) + tuple(stack(s_new, i) for i in range(8)))
```

```python
import functools

import jax
import jax.numpy as jnp
import numpy as np
from jax import lax
from jax.experimental import pallas as pl
from jax.experimental.pallas import tpu as pltpu

F32 = jnp.float32
BF16 = jnp.bfloat16
I32 = jnp.int32

D_MODEL = 1024
PAGE_SIZE = 128
BR_W = 256
HEAD_DIM = 64
N_HEADS = 4
LORA_W = 64
LORA_A = 64
A_SHIFT_W = 3 * BR_W + LORA_W + LORA_A
ROPE_THETA = 500000.0
ROPE_DIMS = HEAD_DIM // 4
RET_THETA = 10000.0
IDX_HEADS = 4
IDX_DIM = 64
DSA_TOPK_MAX = 256
MOBA_BLOCK = 256
MOBA_TOPK = 3
RMS_EPS = 1e-6
RWKV_GN_EPS = 64e-5
RET_GN_EPS = 1e-5
NEG = -1e30
VMEM_LIMIT_BYTES = 56 * 1024 * 1024
RWKV_GROUP = 8

_C = {}
_off = 0
for _name, _n in (('a_r', 256), ('a_k', 256), ('a_v', 256), ('a_wl', 64), ('a_al', 64), ('a_g', 256),
                  ('b_q', 256), ('b_k', 256), ('b_v', 256), ('b_g', 256),
                  ('c_q', 256), ('c_k', 256), ('c_v', 256), ('c_qi', 256), ('c_ki', 64),
                  ('c_wi', 4), ('c_g', 256),
                  ('d_q', 256), ('d_k', 256), ('d_v', 256), ('d_g', 256)):
    _C[_name] = (_off, _off + _n)
    _off += _n

W_A, W_G, W_B, W_C, W_I, W_D = 896, 1024, 768, 768, 640, 768
OFF_A = 0
OFF_G = OFF_A + W_A
OFF_B = OFF_G + W_G
OFF_C = OFF_B + W_B
OFF_I = OFF_C + W_C
OFF_D = OFF_I + W_I
W_ALL = OFF_D + W_D


def _cparams(sem):
    return pltpu.CompilerParams(dimension_semantics=sem, vmem_limit_bytes=VMEM_LIMIT_BYTES)


def _split(x):
    hi = x.astype(BF16)
    lo = (x - hi.astype(F32)).astype(BF16)
    return hi, lo


def _dot(a, b):
    return jnp.dot(a, b, preferred_element_type=F32)


def _dot_nt(a, b):
    return lax.dot_general(a, b, (((1,), (1,)), ((), ())), preferred_element_type=F32)


def _mm1(a, b):
    return _dot(a.astype(BF16), b.astype(BF16))


def _mm3(a, b_hi, b_lo):
    a_hi, a_lo = _split(a)
    return _dot(a_hi, b_hi) + (_dot(a_lo, b_hi) + _dot(a_hi, b_lo))


def _mm3_nt(a, b):
    a_hi, a_lo = _split(a)
    b_hi, b_lo = _split(b)
    return _dot_nt(a_hi, b_hi) + (_dot_nt(a_lo, b_hi) + _dot_nt(a_hi, b_lo))


def _head_ones():
    r = lax.broadcasted_iota(I32, (BR_W, BR_W), 0) // HEAD_DIM
    c = lax.broadcasted_iota(I32, (BR_W, BR_W), 1) // HEAD_DIM
    return jnp.where(r == c, 1.0, 0.0).astype(BF16)


def _headsum(x, bd):
    hi, lo = _split(x)
    return _dot(hi, bd) + _dot(lo, bd)


def _head_mask(h, shape):
    c = lax.broadcasted_iota(I32, shape, len(shape) - 1) // HEAD_DIM
    return c == h


def _ada_kernel(c_ref, w_ref, b_ref, o_ref):
    w = w_ref[...]
    w_hi, w_lo = _split(w)
    o_ref[...] = _mm3(c_ref[...], w_hi, w_lo) + b_ref[...]


def _ada(c, w_ada, b_ada):
    bc = c.shape[0]
    n = w_ada.shape[1]
    tn = 1024
    return pl.pallas_call(
        _ada_kernel,
        grid=(n // tn,),
        in_specs=[pl.BlockSpec((bc, D_MODEL), lambda j: (0, 0)),
                  pl.BlockSpec((D_MODEL, tn), lambda j: (0, j)),
                  pl.BlockSpec((1, tn), lambda j: (0, j))],
        out_specs=pl.BlockSpec((bc, tn), lambda j: (0, j)),
        out_shape=jax.ShapeDtypeStruct((bc, n), F32),
        compiler_params=_cparams(("arbitrary",)),
        name="ada",
    )(c, w_ada, b_ada.reshape(1, n))


def _rope_tables(pos, rot_dims, theta, width):
    half = rot_dims // 2
    inv = jnp.power(jnp.float32(theta), -jnp.arange(half, dtype=jnp.float32) / half)
    ang = pos.astype(jnp.float32)[:, None] * inv[None, :]
    cos = jnp.cos(ang)
    sin = jnp.sin(ang)
    t = pos.shape[0]
    one = jnp.ones((t, HEAD_DIM - rot_dims), F32)
    zero = jnp.zeros((t, HEAD_DIM - rot_dims), F32)
    zh = jnp.zeros((t, half), F32)
    cos_h = jnp.concatenate([cos, cos, one], axis=1)
    up_h = jnp.concatenate([-sin, zh, zero], axis=1)
    dn_h = jnp.concatenate([zh, sin, zero], axis=1)
    reps = width // HEAD_DIM
    return jnp.stack([jnp.tile(cos_h, (1, reps)), jnp.tile(up_h, (1, reps)), jnp.tile(dn_h, (1, reps))])


def _rope_apply(x, tab_ref, lo, hi, half):
    n = hi - lo
    cos = tab_ref[0, :, lo:hi]
    up = tab_ref[1, :, lo:hi]
    dn = tab_ref[2, :, lo:hi]
    return x * cos + pltpu.roll(x, n - half, 1) * up + pltpu.roll(x, half, 1) * dn


def _indexer_key_operand(ki4):
    k_hi, k_lo = _split(ki4)
    lane = lax.broadcasted_iota(I32, ki4.shape, 1)
    return jnp.where(lane < 2 * IDX_DIM, k_hi, jnp.where(lane < 3 * IDX_DIM, k_lo, jnp.zeros_like(k_lo)))


def _layer_in_kernel(x_ref, mod_ref, g_ref, wh_ref, wl_ref, tr_ref, ts_ref,
                     ua_ref, gt_ref, qb_ref, kb_ref, vb_ref, qc_ref, kc_ref, vc_ref,
                     qi_ref, ki_ref, wi_ref, qd_ref, kd_ref, vd_ref, *extra_refs):
    x = x_ref[0]
    y = x * lax.rsqrt(jnp.mean(x * x, axis=-1, keepdims=True) + RMS_EPS) * g_ref[...]
    shift = mod_ref[0, :, 0:D_MODEL]
    scale = mod_ref[0, :, D_MODEL:2 * D_MODEL]
    h = y * (1.0 + scale) + shift
    h_hi, h_lo = _split(h)

    def proj3(lo, hi):
        b_hi = wh_ref[:, lo:hi]
        b_lo = wl_ref[:, lo:hi]
        return _dot(h_hi, b_hi) + (_dot(h_lo, b_hi) + _dot(h_hi, b_lo))

    def proj1(lo, hi):
        return _dot(h_hi, wh_ref[:, lo:hi])

    ua_ref[0] = proj1(OFF_A, OFF_A + W_A)
    gt_ref[0] = proj1(OFF_G, OFF_G + W_G)
    ub = proj1(OFF_B, OFF_B + W_B)
    qb_ref[0] = _rope_apply(ub[:, 0:256], tr_ref, 0, 256, HEAD_DIM // 2)
    kb_ref[0] = _rope_apply(ub[:, 256:512], tr_ref, 0, 256, HEAD_DIM // 2)
    vb_ref[0] = ub[:, 512:768]
    uc = proj1(OFF_C, OFF_C + W_C)
    qc_ref[0] = _rope_apply(uc[:, 0:256], ts_ref, 0, 256, ROPE_DIMS // 2)
    kc = _rope_apply(uc[:, 256:512], ts_ref, 0, 256, ROPE_DIMS // 2)
    kc_ref[0] = kc
    vc = uc[:, 512:768]
    vc_ref[0] = vc
    ui = proj3(OFF_I, OFF_I + W_I)
    qi_ref[0] = _rope_apply(ui[:, 0:256], ts_ref, 0, 256, ROPE_DIMS // 2)
    ki4 = _rope_apply(ui[:, 256:512], ts_ref, 0, 256, ROPE_DIMS // 2)
    ki_ref[0] = ki4
    wi_ref[0] = ui[:, 512:640]
    uqk = proj3(OFF_D, OFF_D + 512)
    qd_ref[0] = _rope_apply(uqk[:, 0:256], ts_ref, 0, 256, ROPE_DIMS // 2)
    kd = _rope_apply(uqk[:, 256:512], ts_ref, 0, 256, ROPE_DIMS // 2)
    kd_ref[0] = kd
    vd = proj1(OFF_D + 512, OFF_D + W_D)
    vd_ref[0] = vd
    if extra_refs:
        kcb_ref, vct_ref, kix_ref, kdb_ref, vdt_ref, kbar_ref = extra_refs
        kcb_ref[0] = kc.astype(BF16)
        vct_ref[0, 0] = vc.T.astype(BF16)
        kix_ref[0] = _indexer_key_operand(ki4)
        kdb_ref[0] = kd.astype(BF16)
        vdt_ref[0, 0] = vd.T.astype(BF16)
        kbar_ref[0, 0] = jnp.sum(kd, axis=0, keepdims=True) * (1.0 / MOBA_BLOCK)


def _regroup_w_in(w_in):
    def cols(name):
        lo, hi = _C[name]
        return w_in[:, lo:hi]
    ki4 = jnp.tile(cols('c_ki'), (1, 4))
    wi_pad = jnp.pad(cols('c_wi'), ((0, 0), (0, 124)))
    w = jnp.concatenate([
        cols('a_r'), cols('a_k'), cols('a_v'), cols('a_wl'), cols('a_al'),
        cols('a_g'), cols('b_g'), cols('c_g'), cols('d_g'),
        cols('b_q'), cols('b_k'), cols('b_v'),
        cols('c_q'), cols('c_k'), cols('c_v'),
        cols('c_qi'), ki4, wi_pad,
        cols('d_q'), cols('d_k'), cols('d_v')], axis=1)
    return _split(w)


def _layer_in(x, mod, g_pre, w_hi, w_lo, tab_ret, tab_std, tm, attention_operands=False):
    b, t, _ = x.shape
    r = mod.shape[1]
    if r == 1:
        mod_spec = pl.BlockSpec((1, 1, 3 * D_MODEL), lambda i, j: (i, 0, 0))
    else:
        mod_spec = pl.BlockSpec((1, tm, 3 * D_MODEL), lambda i, j: (i, j, 0))
    widths = (W_A, W_G, 256, 256, 256, 256, 256, 256, 256, 256, 128, 256, 256, 256)
    seq = lambda w: pl.BlockSpec((1, tm, w), lambda i, j: (i, j, 0))
    out_shape = tuple(jax.ShapeDtypeStruct((b, t, w), F32) for w in widths)
    out_specs = tuple(seq(w) for w in widths)
    if attention_operands:
        rows = jax.ShapeDtypeStruct((b, t, BR_W), BF16)
        cols = jax.ShapeDtypeStruct((b, t // tm, BR_W, tm), BF16)
        col_spec = pl.BlockSpec((1, 1, BR_W, tm), lambda i, j: (i, j, 0, 0))
        out_shape += (rows, cols, rows, rows, cols, jax.ShapeDtypeStruct((b, t // tm, 1, BR_W), F32))
        out_specs += (seq(BR_W), col_spec, seq(BR_W), seq(BR_W), col_spec,
                      pl.BlockSpec((1, 1, 1, BR_W), lambda i, j: (i, j, 0, 0)))
    return pl.pallas_call(
        _layer_in_kernel,
        grid=(b, t // tm),
        in_specs=[pl.BlockSpec((1, tm, D_MODEL), lambda i, j: (i, j, 0)),
                  mod_spec,
                  pl.BlockSpec((1, D_MODEL), lambda i, j: (0, 0)),
                  pl.BlockSpec((D_MODEL, W_ALL), lambda i, j: (0, 0), pipeline_mode=pl.Buffered(1)),
                  pl.BlockSpec((D_MODEL, W_ALL), lambda i, j: (0, 0), pipeline_mode=pl.Buffered(1)),
                  pl.BlockSpec((3, tm, 256), lambda i, j: (0, j, 0)),
                  pl.BlockSpec((3, tm, 256), lambda i, j: (0, j, 0))],
        out_specs=out_specs,
        out_shape=out_shape,
        compiler_params=_cparams(("arbitrary", "arbitrary")),
        name="layer_in",
    )(x, mod, g_pre.reshape(1, D_MODEL), w_hi, w_lo, tab_ret, tab_std)


def _rwkv_pre_math(ua, prev, mu, w0, w2h, w2l, a0, a2h, a2l, kkp, ka, rk, bd):
    xs = ua + (prev - ua) * mu
    r = xs[:, 0:256]
    k = xs[:, 256:512]
    v = xs[:, 512:768]
    wl = xs[:, 768:832]
    al = xs[:, 832:896]
    zw = w0 + _mm3(jnp.tanh(wl), w2h, w2l)
    w_log = -jax.nn.softplus(-zw) - 0.5
    decay = jnp.exp(-jnp.exp(w_log))
    a = jax.nn.sigmoid(a0 + _mm3(al, a2h, a2l))
    kq = k * kkp
    kk = kq * lax.rsqrt(_headsum(kq * kq, bd) + 1e-12)
    k2 = k * (1.0 + (a - 1.0) * ka)
    bonus = _headsum(r * k2 * rk, bd) * v
    return r, decay, k2, v, kk, kk * a, bonus


def _rwkv_pre_shift_kernel(ua_ref, up_ref, p0_ref, mu_ref, w0_ref, w2h_ref, w2l_ref, a0_ref, a2h_ref, a2l_ref,
                           kkp_ref, ka_ref, rk_ref, r_ref, w_ref, k_ref, v_ref, kk_ref, b_ref, bo_ref):
    j = pl.program_id(1)
    ua = ua_ref[0]
    tm = ua.shape[0]
    first = jnp.where(j == 0, p0_ref[0], up_ref[0, 7:8, :])
    row = lax.broadcasted_iota(I32, ua.shape, 0)
    prev = jnp.where(row == 0, first, pltpu.roll(ua, 1, 0))
    outs = _rwkv_pre_math(ua, prev, mu_ref[...], w0_ref[...], w2h_ref[...], w2l_ref[...], a0_ref[...],
                          a2h_ref[...], a2l_ref[...], kkp_ref[...], ka_ref[...], rk_ref[...], _head_ones())
    for o_ref, o in zip((r_ref, w_ref, k_ref, v_ref, kk_ref, b_ref, bo_ref), outs):
        o_ref[0] = o


def _rwkv_pre_rows_kernel(ua_ref, pv_ref, mu_ref, w0_ref, w2h_ref, w2l_ref, a0_ref, a2h_ref, a2l_ref,
                          kkp_ref, ka_ref, rk_ref, r_ref, w_ref, k_ref, v_ref, kk_ref, b_ref, bo_ref):
    outs = _rwkv_pre_math(ua_ref[...], pv_ref[...], mu_ref[...], w0_ref[...], w2h_ref[...], w2l_ref[...],
                          a0_ref[...], a2h_ref[...], a2l_ref[...], kkp_ref[...], ka_ref[...], rk_ref[...],
                          _head_ones())
    for o_ref, o in zip((r_ref, w_ref, k_ref, v_ref, kk_ref, b_ref, bo_ref), outs):
        o_ref[...] = o


def _rwkv_params(p):
    w2h, w2l = _split(p['a_w2'])
    a2h, a2l = _split(p['a_a2'])
    row = lambda v: v.reshape(1, -1)
    return (row(p['a_mu']), row(p['a_w0']), w2h, w2l, row(p['a_a0']), a2h, a2l,
            row(p['a_kk']), row(p['a_ka']), row(p['a_rk']))


def _rwkv_pre_prompt(ua, prev0, p, tm):
    b, t, _ = ua.shape
    prm = _rwkv_params(p)
    full = lambda a: pl.BlockSpec(a.shape, lambda i, j: (0,) * a.ndim)
    blk8 = tm // 8
    return pl.pallas_call(
        _rwkv_pre_shift_kernel,
        grid=(b, t // tm),
        in_specs=[pl.BlockSpec((1, tm, A_SHIFT_W), lambda i, j: (i, j, 0)),
                  pl.BlockSpec((1, 8, A_SHIFT_W), lambda i, j: (i, jnp.maximum(j * blk8 - 1, 0), 0)),
                  pl.BlockSpec((1, 1, A_SHIFT_W), lambda i, j: (i, 0, 0))] + [full(a) for a in prm],
        out_specs=tuple(pl.BlockSpec((1, tm, BR_W), lambda i, j: (i, j, 0)) for _ in range(7)),
        out_shape=tuple(jax.ShapeDtypeStruct((b, t, BR_W), F32) for _ in range(7)),
        compiler_params=_cparams(("arbitrary", "arbitrary")),
        name="rwkv_pre",
    )(ua, ua, prev0.reshape(b, 1, A_SHIFT_W), *prm)


def _rwkv_pre_rows(ua, prev, p):
    n = ua.shape[0]
    prm = _rwkv_params(p)
    full = lambda a: pl.BlockSpec(a.shape, lambda: (0,) * a.ndim)
    return pl.pallas_call(
        _rwkv_pre_rows_kernel,
        in_specs=[full(ua), full(prev)] + [full(a) for a in prm],
        out_specs=tuple(pl.BlockSpec((n, BR_W), lambda: (0, 0)) for _ in range(7)),
        out_shape=tuple(jax.ShapeDtypeStruct((n, BR_W), F32) for _ in range(7)),
        name="rwkv_pre_rows",
    )(ua, prev, *prm)


def _rwkv_scan_kernel(r_ref, w_ref, k_ref, v_ref, kk_ref, b_ref, s0_ref, y_ref, sout_ref,
                      s_scr, z_scr, be_scr, ga_scr, *, tc):
    c = pl.program_id(0)

    @pl.when(c == 0)
    def _():
        s_scr[...] = s0_ref[...]

    nb = s_scr.shape[0]
    bd = _head_ones()
    vi = lax.broadcasted_iota(I32, (HEAD_DIM, BR_W), 0)
    ci = lax.broadcasted_iota(I32, (HEAD_DIM, BR_W), 1)
    diag = ((ci % HEAD_DIM) == vi)[None]

    for bi in range(nb):
        kk_next = pltpu.roll(kk_ref[bi], tc - 1, 0)
        z_scr[bi] = w_ref[bi] * kk_next
        be_scr[bi] = _headsum(b_ref[bi] * kk_next, bd)
        ga_scr[bi] = _headsum(k_ref[bi] * kk_next, bd)

    def pair(i, carry):
        t = 2 * i
        row = lambda ref, tt: ref[:, pl.ds(tt, 1), :]
        s = s_scr[...]
        lhs = jnp.concatenate([
            (s * row(kk_ref, t)).astype(BF16),
            (s * row(z_scr, t)).astype(BF16),
            jnp.where(diag, row(v_ref, t), 0.0).astype(BF16),
            jnp.where(diag, row(v_ref, t + 1), 0.0).astype(BF16)], axis=1)
        res = _dot(lhs.reshape(nb * 4 * HEAD_DIM, BR_W), bd).reshape(nb, 4 * HEAD_DIM, BR_W)
        sk1 = res[:, 0:HEAD_DIM]
        vc1 = res[:, 2 * HEAD_DIM:3 * HEAD_DIM]
        vc2 = res[:, 3 * HEAD_DIM:4 * HEAD_DIM]
        sk2 = res[:, HEAD_DIM:2 * HEAD_DIM] - sk1 * row(be_scr, t) + vc1 * row(ga_scr, t)
        s1 = s * row(w_ref, t) - sk1 * row(b_ref, t) + vc1 * row(k_ref, t)
        s2 = s1 * row(w_ref, t + 1) - sk2 * row(b_ref, t + 1) + vc2 * row(k_ref, t + 1)
        s_scr[...] = s2
        q = jnp.concatenate([(s1 * row(r_ref, t)).astype(BF16), (s2 * row(r_ref, t + 1)).astype(BF16)], axis=1)
        yb = _dot(q.reshape(nb * 2 * HEAD_DIM, BR_W), bd).reshape(nb, 2 * HEAD_DIM, BR_W)
        y_ref[:, pl.ds(t, 1), :] = jnp.sum(jnp.where(diag, yb[:, :HEAD_DIM], 0.0), axis=1, keepdims=True)
        y_ref[:, pl.ds(t + 1, 1), :] = jnp.sum(jnp.where(diag, yb[:, HEAD_DIM:], 0.0), axis=1, keepdims=True)
        return carry

    lax.fori_loop(0, tc // 2, pair, 0, unroll=8)

    @pl.when(c == pl.num_programs(0) - 1)
    def _():
        sout_ref[...] = s_scr[...]


def _rwkv_scan(r, w, k, v, kk, bb, s0, tc):
    b, t, _ = r.shape
    assert tc % 2 == 0 and t % tc == 0
    seq = pl.BlockSpec((b, tc, BR_W), lambda c: (0, c, 0))
    st = pl.BlockSpec((b, HEAD_DIM, BR_W), lambda c: (0, 0, 0))
    return pl.pallas_call(
        functools.partial(_rwkv_scan_kernel, tc=tc),
        grid=(t // tc,),
        in_specs=[seq] * 6 + [st],
        out_specs=(seq, st),
        out_shape=(jax.ShapeDtypeStruct((b, t, BR_W), F32), jax.ShapeDtypeStruct((b, HEAD_DIM, BR_W), F32)),
        scratch_shapes=[pltpu.VMEM((b, HEAD_DIM, BR_W), F32)] + [pltpu.VMEM((b, tc, BR_W), F32)] * 3,
        compiler_params=_cparams(("arbitrary",)),
        name="rwkv_scan",
    )(r, w, k, v, kk, bb, s0)


def _wkv_to_scan_layout(s):
    b = s.shape[0]
    return jnp.transpose(s, (0, 2, 1, 3)).reshape(b, HEAD_DIM, BR_W)


def _wkv_from_scan_layout(s):
    b = s.shape[0]
    return jnp.transpose(s.reshape(b, HEAD_DIM, N_HEADS, HEAD_DIM), (0, 2, 1, 3))


def _rwkv_step_kernel(r_ref, w_ref, k_ref, vc_ref, kk_ref, b_ref, s_ref, y_ref, so_ref):
    s = s_ref[...]
    sk = jnp.sum(s * kk_ref[...], axis=-1, keepdims=True)
    s = s * w_ref[...] - sk * b_ref[...] + vc_ref[...] * k_ref[...]
    so_ref[...] = s
    y_ref[...] = jnp.sum(s * r_ref[...], axis=-1, keepdims=True)


def _rwkv_step(r, w, k, v, kk, bb, s0):
    b = r.shape[0]
    n = b * N_HEADS
    rowf = lambda a: a.reshape(n, 1, HEAD_DIM)
    full = lambda shape: pl.BlockSpec(shape, lambda: (0,) * len(shape))
    y, s = pl.pallas_call(
        _rwkv_step_kernel,
        in_specs=[full((n, 1, HEAD_DIM))] * 3 + [full((n, HEAD_DIM, 1))] + [full((n, 1, HEAD_DIM))] * 2
                 + [full((n, HEAD_DIM, HEAD_DIM))],
        out_specs=(full((n, HEAD_DIM, 1)), full((n, HEAD_DIM, HEAD_DIM))),
        out_shape=(jax.ShapeDtypeStruct((n, HEAD_DIM, 1), F32), jax.ShapeDtypeStruct((n, HEAD_DIM, HEAD_DIM), F32)),
        compiler_params=pltpu.CompilerParams(vmem_limit_bytes=VMEM_LIMIT_BYTES),
        name="rwkv_step",
    )(rowf(r), rowf(w), rowf(k), v.reshape(n, HEAD_DIM, 1), rowf(kk), rowf(bb), s0.reshape(n, HEAD_DIM, HEAD_DIM))
    return y.reshape(b, BR_W), s.reshape(b, N_HEADS, HEAD_DIM, HEAD_DIM)


def _ret_tables(c):
    log_g = jnp.log(1.0 - jnp.power(2.0, -5.0 - jnp.arange(N_HEADS, dtype=jnp.float32)))
    i = jnp.arange(c, dtype=jnp.float32)
    diff = i[:, None] - i[None, :]
    dmat = jnp.where(diff[None] >= 0, jnp.exp(jnp.maximum(diff, 0.0)[None] * log_g[:, None, None]), 0.0)
    dq = jnp.exp((i[:, None] + 1.0) * log_g[None, :])
    dk = jnp.exp((c - 1.0 - i)[:, None] * log_g[None, :])
    ds = jnp.exp(c * log_g)
    lanes = lambda a: jnp.repeat(a, HEAD_DIM, axis=-1)
    return dmat, lanes(dq), lanes(dk), lanes(ds[None, :])


def _ret_kernel(q_ref, k_ref, v_ref, s0_ref, dm_ref, dq_ref, dk_ref, ds_ref, gw_ref, gb_ref,
                y_ref, so_ref, s_scr):
    j = pl.program_id(1)

    @pl.when(j == 0)
    def _():
        s_scr[...] = s0_ref[0]

    q = q_ref[0]
    k = k_ref[0] * (HEAD_DIM ** -0.5)
    v = v_ref[0]
    s = s_scr[...]
    bd = _head_ones()
    qb = q.astype(BF16)
    kb = k.astype(BF16)
    vb = v.astype(BF16)
    y = _dot(qb, s.astype(BF16)) * dq_ref[...]
    for h in range(N_HEADS):
        hm = _head_mask(h, (1, BR_W))
        att = _dot_nt(jnp.where(hm, q, 0.0).astype(BF16), kb) * dm_ref[h]
        y = y + jnp.where(hm, _dot(att.astype(BF16), vb), 0.0)
    kd = (k * dk_ref[...]).T.astype(BF16)
    s_scr[...] = s * ds_ref[...] + bd.astype(F32) * _dot(kd, vb)
    mu = _headsum(y, bd) * (1.0 / HEAD_DIM)
    d = y - mu
    var = _headsum(d * d, bd) * (1.0 / HEAD_DIM)
    y_ref[0] = d * lax.rsqrt(var + RET_GN_EPS) * gw_ref[...] + gb_ref[...]

    @pl.when(j == pl.num_programs(1) - 1)
    def _():
        so_ref[0] = s_scr[...]


def _ret_state_embed(s):
    b = s.shape[0]
    eye = jnp.eye(N_HEADS, dtype=s.dtype)
    return jnp.einsum('bhde,hg->bhdge', s, eye).reshape(b, BR_W, BR_W)


def _ret_state_extract(s):
    b = s.shape[0]
    s4 = s.reshape(b, N_HEADS, HEAD_DIM, N_HEADS, HEAD_DIM)
    return jnp.stack([s4[:, h, :, h, :] for h in range(N_HEADS)], axis=1)


def _retention_prompt(q, k, v, s0, gn_w, gn_b, c):
    b, t, _ = q.shape
    dmat, dq, dk, ds = _ret_tables(c)
    seq = pl.BlockSpec((1, c, BR_W), lambda i, j: (i, j, 0))
    st = pl.BlockSpec((1, BR_W, BR_W), lambda i, j: (i, 0, 0))
    const = lambda a: pl.BlockSpec(a.shape, lambda i, j: (0,) * a.ndim)
    gw = gn_w.reshape(1, BR_W)
    gb = gn_b.reshape(1, BR_W)
    y, s = pl.pallas_call(
        _ret_kernel,
        grid=(b, t // c),
        in_specs=[seq, seq, seq, st, const(dmat), const(dq), const(dk), const(ds), const(gw), const(gb)],
        out_specs=(seq, st),
        out_shape=(jax.ShapeDtypeStruct((b, t, BR_W), F32), jax.ShapeDtypeStruct((b, BR_W, BR_W), F32)),
        scratch_shapes=[pltpu.VMEM((BR_W, BR_W), F32)],
        compiler_params=_cparams(("arbitrary", "arbitrary")),
        name="retention",
    )(q, k, v, _ret_state_embed(s0), dmat, dq, dk, ds, gw, gb)
    return y, _ret_state_extract(s)


def _ret_step_kernel(qc_ref, kc_ref, qr_ref, kr_ref, v_ref, s_ref, g_ref, gw_ref, gb_ref, y_ref, so_ref):
    s = s_ref[...]
    g = g_ref[...]
    v = v_ref[...]
    qk = jnp.sum(qr_ref[...] * kr_ref[...], axis=-1, keepdims=True)
    y = qk * v + jnp.sum(qc_ref[...] * s, axis=1, keepdims=True) * g
    so_ref[...] = s * g + kc_ref[...] * v
    mu = jnp.mean(y, axis=-1, keepdims=True)
    d = y - mu
    var = jnp.mean(d * d, axis=-1, keepdims=True)
    y_ref[...] = d * lax.rsqrt(var + RET_GN_EPS) * gw_ref[...] + gb_ref[...]


def _retention_step(q, k, v, s0, gn_w, gn_b):
    b = q.shape[0]
    n = b * N_HEADS
    ks = k * (HEAD_DIM ** -0.5)
    g = 1.0 - jnp.power(2.0, -5.0 - jnp.arange(N_HEADS, dtype=jnp.float32))
    g = jnp.tile(g, (b,)).reshape(n, 1, 1)
    gw = jnp.tile(gn_w.reshape(N_HEADS, HEAD_DIM), (b, 1)).reshape(n, 1, HEAD_DIM)
    gb = jnp.tile(gn_b.reshape(N_HEADS, HEAD_DIM), (b, 1)).reshape(n, 1, HEAD_DIM)
    col = lambda a: a.reshape(n, HEAD_DIM, 1)
    row = lambda a: a.reshape(n, 1, HEAD_DIM)
    args = (col(q), col(ks), row(q), row(ks), row(v), s0.reshape(n, HEAD_DIM, HEAD_DIM), g, gw, gb)
    full = lambda a: pl.BlockSpec(a.shape, lambda: (0,) * a.ndim)
    y, s = pl.pallas_call(
        _ret_step_kernel,
        in_specs=[full(a) for a in args],
        out_specs=(pl.BlockSpec((n, 1, HEAD_DIM), lambda: (0, 0, 0)),
                   pl.BlockSpec((n, HEAD_DIM, HEAD_DIM), lambda: (0, 0, 0))),
        out_shape=(jax.ShapeDtypeStruct((n, 1, HEAD_DIM), F32), jax.ShapeDtypeStruct((n, HEAD_DIM, HEAD_DIM), F32)),
        compiler_params=pltpu.CompilerParams(vmem_limit_bytes=VMEM_LIMIT_BYTES),
        name="retention_step",
    )(*args)
    return y.reshape(b, BR_W), s.reshape(b, N_HEADS, HEAD_DIM, HEAD_DIM)


INT_MIN = -2 ** 31


def _sort_key(s):
    s = jnp.where(s == 0.0, 0.0, s)
    bits = lax.bitcast_convert_type(s, I32)
    return jnp.where(bits < 0, bits ^ jnp.int32(0x7FFFFFFF), bits)


def _kth_largest_key(count_ge, shape, k):
    def bit_step(i, tb):
        cand = tb + jnp.left_shift(jnp.int32(1), 31 - i)
        return jnp.where(count_ge(cand) >= k, cand, tb)

    return lax.fori_loop(0, 32, bit_step, jnp.full(shape, INT_MIN, I32))


def _attn_queries(qf, qa_scr):
    qs = qf * (HEAD_DIM ** -0.5)
    for h in range(N_HEADS):
        qa_scr[h] = jnp.where(_head_mask(h, (1, BR_W)), qs, 0.0).astype(BF16)


def _flash_update_t(qa_scr, kb, vt, msk, stats, acc):
    new_stats = []
    parts = []
    for h in range(N_HEADS):
        m_old, l_old = stats[h]
        mk = msk[h] if isinstance(msk, (list, tuple)) else msk
        s = jnp.where(mk, _dot_nt(kb, qa_scr[h]), NEG)
        m_new = jnp.maximum(m_old, jnp.max(s, axis=0, keepdims=True))
        alpha = jnp.exp(m_old - m_new)
        p = jnp.exp(s - m_new)
        l_new = alpha * l_old + jnp.sum(p, axis=0, keepdims=True)
        rows = slice(h * HEAD_DIM, (h + 1) * HEAD_DIM)
        parts.append(acc[rows] * alpha + _dot(vt[rows], p.astype(BF16)))
        new_stats.append((m_new, l_new))
    return tuple(new_stats), jnp.concatenate(parts, axis=0)


def _flash_init_t(tq):
    stats = tuple((jnp.full((1, tq), NEG, F32), jnp.zeros((1, tq), F32)) for _ in range(N_HEADS))
    return stats, jnp.zeros((BR_W, tq), F32)


def _flash_finish_t(stats, acc):
    out = jnp.concatenate([acc[h * HEAD_DIM:(h + 1) * HEAD_DIM] / stats[h][1] for h in range(N_HEADS)], axis=0)
    return out.T


def _dsa_prompt_kernel(q_ref, qi_ref, wi_ref, kb_ref, vt_ref, kx_ref, o_ref,
                       key_scr, qs_scr, qa_scr, *, tq, kb, topk):
    qt = pl.program_id(1)
    nkc = (qt + 1) * (tq // kb)
    qpos = qt * tq + lax.broadcasted_iota(I32, (kb, tq), 1)
    kofs = lax.broadcasted_iota(I32, (kb, tq), 0)
    lane = lax.broadcasted_iota(I32, (tq, BR_W), 1)
    qi = qi_ref[0]
    for h in range(IDX_HEADS):
        qm = jnp.where(lane // IDX_DIM == h, qi, 0.0)
        rep = (qm + pltpu.roll(qm, IDX_DIM, 1)) + (pltpu.roll(qm, 2 * IDX_DIM, 1) + pltpu.roll(qm, 3 * IDX_DIM, 1))
        q_hi, q_lo = _split(rep)
        qs_scr[h] = jnp.where((lane < IDX_DIM) | ((lane >= 2 * IDX_DIM) & (lane < 3 * IDX_DIM)), q_hi,
                              jnp.where(lane < 2 * IDX_DIM, q_lo, jnp.zeros_like(q_lo)))
    _attn_queries(q_ref[0], qa_scr)
    w_t = wi_ref[0].T * (IDX_HEADS ** -0.5 * IDX_DIM ** -0.5)

    def score_chunk(c, carry):
        kx = kx_ref[0, pl.ds(pl.multiple_of(c * kb, kb), kb), :]
        s = jnp.zeros((kb, tq), F32)
        for h in range(IDX_HEADS):
            s = s + jnp.maximum(_dot_nt(kx, qs_scr[h]), 0.0) * w_t[h:h + 1, :]
        s = jnp.where(c * kb + kofs <= qpos, s, -jnp.inf)
        key_scr[c] = _sort_key(s)
        return carry

    lax.fori_loop(0, nkc, score_chunk, 0)

    def counter(cmp):
        def count(cand):
            def body(c, acc):
                kc = key_scr[c].reshape(4, kb // 32, 8, tq)
                return acc + jnp.sum(jnp.where(cmp(kc, cand), 1.0, 0.0), axis=1)
            acc = lax.fori_loop(0, nkc, body, jnp.zeros((4, 8, tq), F32))
            return jnp.sum(jnp.sum(acc, axis=0), axis=0, keepdims=True)
        return count

    thr = _kth_largest_key(counter(lambda a, b: a >= b), (1, tq), float(topk))
    rem = float(topk) - counter(lambda a, b: a > b)(thr)
    lower = (lax.broadcasted_iota(I32, (kb, kb), 1) <= lax.broadcasted_iota(I32, (kb, kb), 0)).astype(BF16)

    def attn_chunk(c, carry):
        stats, acc, run = carry
        kc = key_scr[c]
        eq = kc == thr
        eqf = jnp.where(eq, 1.0, 0.0)
        pre = _dot(lower, eqf.astype(BF16)) + run
        sel = (kc > thr) | (eq & (pre <= rem))
        msk = sel & (c * kb + kofs <= qpos)
        off = pl.multiple_of(c * kb, kb)
        stats, acc = _flash_update_t(qa_scr, kb_ref[0, pl.ds(off, kb), :], vt_ref[0, c], msk, stats, acc)
        return stats, acc, run + jnp.sum(eqf, axis=0, keepdims=True)

    stats, acc = _flash_init_t(tq)
    stats, acc, _ = lax.fori_loop(0, nkc, attn_chunk, (stats, acc, jnp.zeros((1, tq), F32)))
    o_ref[0] = _flash_finish_t(stats, acc)


def _dsa_prompt(q, qi, wi, kb16, vt16, kx, tq=256, kb=256):
    b, t, _ = q.shape
    topk = min(DSA_TOPK_MAX, t // 4)
    assert tq >= topk and tq == kb
    tile = lambda w: pl.BlockSpec((1, tq, w), lambda i, j: (i, j, 0))
    full = pl.BlockSpec((1, t, BR_W), lambda i, j: (i, 0, 0))
    return pl.pallas_call(
        functools.partial(_dsa_prompt_kernel, tq=tq, kb=kb, topk=topk),
        grid=(b, t // tq),
        in_specs=[tile(BR_W), tile(BR_W), tile(128), full,
                  pl.BlockSpec((1, t // kb, BR_W, kb), lambda i, j: (i, 0, 0, 0)), full],
        out_specs=tile(BR_W),
        out_shape=jax.ShapeDtypeStruct((b, t, BR_W), F32),
        scratch_shapes=[pltpu.VMEM((t // kb, kb, tq), I32),
                        pltpu.VMEM((IDX_HEADS, tq, BR_W), BF16), pltpu.VMEM((N_HEADS, tq, BR_W), BF16)],
        compiler_params=_cparams(("arbitrary", "arbitrary")),
        name="dsa_prompt",
    )(q, qi, wi, kb16, vt16, kx)


def _top_blocks_t(g, row, limit, nsel):
    g = jnp.where(row < limit, g, -jnp.inf)
    selm = jnp.zeros(g.shape, jnp.bool_)
    for _ in range(nsel):
        mx = jnp.max(g, axis=0, keepdims=True)
        idx = jnp.min(jnp.where(g == mx, row, jnp.int32(1 << 20)), axis=0, keepdims=True)
        pick = row == idx
        selm = selm | (pick & (row < limit))
        g = jnp.where(pick, -jnp.inf, g)
    return selm


def _moba_prompt_kernel(q_ref, kbar_ref, kb_ref, vt_ref, o_ref, qa_scr, sel_scr, *, nsel):
    qt = pl.program_id(1)
    tq = MOBA_BLOCK
    qf = q_ref[0]
    kbar = kbar_ref[0]
    row = lax.broadcasted_iota(I32, (kbar.shape[0], tq), 0)
    for h in range(N_HEADS):
        g = _mm3_nt(kbar, jnp.where(_head_mask(h, (1, BR_W)), qf, 0.0))
        sel_scr[h] = jnp.where(_top_blocks_t(g, row, qt, nsel), 1.0, 0.0)
    _attn_queries(qf, qa_scr)

    tri = lax.broadcasted_iota(I32, (tq, tq), 0) <= lax.broadcasted_iota(I32, (tq, tq), 1)
    off = pl.multiple_of(qt * tq, tq)
    stats, acc = _flash_init_t(tq)
    stats, acc = _flash_update_t(qa_scr, kb_ref[0, pl.ds(off, tq), :], vt_ref[0, qt], tri, stats, acc)

    def past(n, carry):
        stats, acc = carry
        o = pl.multiple_of(n * tq, tq)
        allow = [sel_scr[h, pl.ds(n, 1), :] > 0.0 for h in range(N_HEADS)]
        return _flash_update_t(qa_scr, kb_ref[0, pl.ds(o, tq), :], vt_ref[0, n], allow, stats, acc)

    stats, acc = lax.fori_loop(0, qt, past, (stats, acc))
    o_ref[0] = _flash_finish_t(stats, acc)


def _moba_prompt(q, kbar, kb16, vt16):
    b, t, _ = q.shape
    nb = t // MOBA_BLOCK
    nsel = min(MOBA_TOPK, nb - 1)
    nbp = -(-nb // 16) * 16
    kbar = jnp.pad(kbar, ((0, 0), (0, nbp - nb), (0, 0)))
    tile = pl.BlockSpec((1, MOBA_BLOCK, BR_W), lambda i, j: (i, j, 0))
    full = pl.BlockSpec((1, t, BR_W), lambda i, j: (i, 0, 0))
    return pl.pallas_call(
        functools.partial(_moba_prompt_kernel, nsel=nsel),
        grid=(b, nb),
        in_specs=[tile, pl.BlockSpec((1, nbp, BR_W), lambda i, j: (i, 0, 0)), full,
                  pl.BlockSpec((1, nb, BR_W, MOBA_BLOCK), lambda i, j: (i, 0, 0, 0))],
        out_specs=tile,
        out_shape=jax.ShapeDtypeStruct((b, t, BR_W), F32),
        scratch_shapes=[pltpu.VMEM((N_HEADS, MOBA_BLOCK, BR_W), BF16),
                        pltpu.VMEM((N_HEADS, nbp, MOBA_BLOCK), F32)],
        compiler_params=_cparams(("arbitrary", "arbitrary")),
        name="moba_prompt",
    )(q, kbar, kb16, vt16)


def _layer_out_kernel(x_ref, mod_ref, gpre_ref, gpost_ref, ya_ref, bo_ref, yb_ref, yc_ref, yd_ref, gt_ref,
                      lnw_ref, lnb_ref, wb_ref, wm_ref, wo_ref, o_ref):
    x = x_ref[0]
    y = x * lax.rsqrt(jnp.mean(x * x, axis=-1, keepdims=True) + RMS_EPS) * gpre_ref[...]
    shift = mod_ref[0, :, 0:D_MODEL]
    scale = mod_ref[0, :, D_MODEL:2 * D_MODEL]
    gate = mod_ref[0, :, 2 * D_MODEL:3 * D_MODEL]
    hb = (y * (1.0 + scale) + shift).astype(BF16)
    bd = _head_ones()
    ya = ya_ref[0]
    mu = _headsum(ya, bd) * (1.0 / HEAD_DIM)
    d = ya - mu
    var = _headsum(d * d, bd) * (1.0 / HEAD_DIM)
    ya = d * lax.rsqrt(var + RWKV_GN_EPS) * lnw_ref[...] + lnb_ref[...] + bo_ref[0]
    outs = (ya, yb_ref[0], yc_ref[0], yd_ref[0])
    merged = jnp.zeros(x.shape, F32)
    for n in range(4):
        o = outs[n] * jax.nn.silu(gt_ref[0, :, n * BR_W:(n + 1) * BR_W])
        merged = merged + jax.nn.sigmoid(_dot(hb, wm_ref[n])) * _dot(o.astype(BF16), wb_ref[n])
    z = _dot(merged.astype(BF16), wo_ref[...])
    z = z * lax.rsqrt(jnp.mean(z * z, axis=-1, keepdims=True) + RMS_EPS) * gpost_ref[...]
    o_ref[0] = x + gate * z


def _layer_out(x, mod, p, ya, bonus, yb, yc, yd, gates, tm):
    b, t, _ = x.shape
    r = mod.shape[1]
    if r == 1:
        mod_spec = pl.BlockSpec((1, 1, 3 * D_MODEL), lambda i, j: (i, 0, 0))
    else:
        mod_spec = pl.BlockSpec((1, tm, 3 * D_MODEL), lambda i, j: (i, j, 0))
    tile = lambda w: pl.BlockSpec((1, tm, w), lambda i, j: (i, j, 0))
    row = lambda v: v.reshape(1, -1)
    const = lambda shape: pl.BlockSpec(shape, lambda i, j: (0,) * len(shape), pipeline_mode=pl.Buffered(1))
    vec = lambda n: pl.BlockSpec((1, n), lambda i, j: (0, 0))
    return pl.pallas_call(
        _layer_out_kernel,
        grid=(b, t // tm),
        in_specs=[tile(D_MODEL), mod_spec, vec(D_MODEL), vec(D_MODEL),
                  tile(BR_W), tile(BR_W), tile(BR_W), tile(BR_W), tile(BR_W), tile(D_MODEL),
                  vec(BR_W), vec(BR_W),
                  const((4, BR_W, D_MODEL)), const((4, D_MODEL, D_MODEL)), const((D_MODEL, D_MODEL))],
        out_specs=tile(D_MODEL),
        out_shape=jax.ShapeDtypeStruct((b, t, D_MODEL), F32),
        compiler_params=_cparams(("arbitrary", "arbitrary")),
        name="layer_out",
    )(x, mod, row(p['g_pre']), row(p['g_post']), ya, bonus, yb, yc, yd, gates,
      row(p['a_ln_w']), row(p['a_ln_b']),
      p['w_branch'].astype(BF16), p['w_merge'].astype(BF16), p['w_out'].astype(BF16))


PAGES_PER_STEP = 32


def _page_specs(block, g_count):
    tail = (0,) * (len(block) - 1)

    def spec(g):
        return pl.BlockSpec(block, lambda i, j, pt: (pt[i, j * g_count + g],) + tail)
    return [spec(g) for g in range(g_count)]


def _head_rows(x):
    hr = lax.broadcasted_iota(I32, (N_HEADS, BR_W), 0)
    hl = lax.broadcasted_iota(I32, (N_HEADS, BR_W), 1) // HEAD_DIM
    return jnp.where(hr == hl, x, 0.0), hr == hl


def _dsa_scores_kernel(pt_ref, qm_ref, wi_ref, kn_ref, *refs, g_count):
    page_refs = refs[:g_count]
    o_ref, on_ref = refs[g_count], refs[g_count + 1]
    j = pl.program_id(1)
    qm = qm_ref[0]
    q_hi, q_lo = _split(qm)
    w = wi_ref[0] * (IDX_HEADS ** -0.5)

    def combine(d):
        return jnp.sum(jnp.maximum(d * (IDX_DIM ** -0.5), 0.0) * w, axis=0, keepdims=True)

    for g in range(g_count):
        k_hi, k_lo = _split(page_refs[g][0])
        d = _dot_nt(q_hi, k_hi) + (_dot_nt(q_lo, k_hi) + _dot_nt(q_hi, k_lo))
        o_ref[0, g] = combine(d)

    @pl.when(j == 0)
    def _():
        dn = jnp.sum(qm * kn_ref[0], axis=-1, keepdims=True)
        on_ref[0] = jnp.broadcast_to(combine(dn), (1, 128))


def _dsa_scores(qi, wi, ki_new, pool, page_table):
    b, n_pages = page_table.shape
    g_count = min(PAGES_PER_STEP, n_pages)
    qm = qi.reshape(b, IDX_HEADS, IDX_DIM)
    wcol = wi.reshape(b, IDX_HEADS, 1)
    kn = ki_new.reshape(b, 1, IDX_DIM)
    grid_spec = pltpu.PrefetchScalarGridSpec(
        num_scalar_prefetch=1,
        grid=(b, n_pages // g_count),
        in_specs=[pl.BlockSpec((1, IDX_HEADS, IDX_DIM), lambda i, j, pt: (i, 0, 0)),
                  pl.BlockSpec((1, IDX_HEADS, 1), lambda i, j, pt: (i, 0, 0)),
                  pl.BlockSpec((1, 1, IDX_DIM), lambda i, j, pt: (i, 0, 0))]
                 + _page_specs((1, PAGE_SIZE, IDX_DIM), g_count),
        out_specs=(pl.BlockSpec((1, g_count, 1, PAGE_SIZE), lambda i, j, pt: (i, j, 0, 0)),
                   pl.BlockSpec((1, 1, 128), lambda i, j, pt: (i, 0, 0))))
    sc, sc_new = pl.pallas_call(
        functools.partial(_dsa_scores_kernel, g_count=g_count),
        grid_spec=grid_spec,
        out_shape=(jax.ShapeDtypeStruct((b, n_pages, 1, PAGE_SIZE), F32),
                   jax.ShapeDtypeStruct((b, 1, 128), F32)),
        compiler_params=_cparams(("arbitrary", "arbitrary")),
        name="dsa_scores",
    )(page_table, qm, wcol, kn, *([pool] * g_count))
    return sc.reshape(b, n_pages * PAGE_SIZE), sc_new[:, 0, 0:1]


def _topk_rows_kernel(s_ref, o_ref, key_scr, *, n_valid, topk):
    nblk, rows, _ = s_ref.shape
    col = lax.broadcasted_iota(I32, (rows, 128), 1)

    def to_key(j, carry):
        s = jnp.where(j * 128 + col < n_valid, s_ref[j], -jnp.inf)
        key_scr[j] = _sort_key(s)
        return carry

    lax.fori_loop(0, nblk, to_key, 0)

    def counter(cmp):
        def count(cand):
            body = lambda j, acc: acc + jnp.where(cmp(key_scr[j], cand), 1.0, 0.0)
            acc = lax.fori_loop(0, nblk, body, jnp.zeros((rows, 128), F32))
            return jnp.sum(acc, axis=-1, keepdims=True)
        return count

    tb = _kth_largest_key(counter(lambda a, b: a >= b), (rows, 128), float(topk))
    rem = float(topk) - counter(lambda a, b: a > b)(tb)
    upper = (lax.broadcasted_iota(I32, (128, 128), 0) <= lax.broadcasted_iota(I32, (128, 128), 1)).astype(BF16)

    def select(j, run):
        kc = key_scr[j]
        eq = kc == tb
        eqf = jnp.where(eq, 1.0, 0.0)
        pre = _dot(eqf.astype(BF16), upper) + run
        sel = ((kc > tb) | (eq & (pre <= rem))) & (j * 128 + col < n_valid)
        o_ref[j] = jnp.where(sel, 1.0, 0.0)
        return run + jnp.sum(eqf, axis=-1, keepdims=True)

    lax.fori_loop(0, nblk, select, jnp.zeros((rows, 1), F32))


def _topk_rows(scores, topk):
    rows, n = scores.shape
    nblk = -(-n // 128)
    s = jnp.pad(scores, ((0, 0), (0, nblk * 128 - n)))
    s = jnp.transpose(s.reshape(rows, nblk, 128), (1, 0, 2))
    m = pl.pallas_call(
        functools.partial(_topk_rows_kernel, n_valid=n, topk=topk),
        in_specs=[pl.BlockSpec((nblk, rows, 128), lambda: (0, 0, 0))],
        out_specs=pl.BlockSpec((nblk, rows, 128), lambda: (0, 0, 0)),
        out_shape=jax.ShapeDtypeStruct((nblk, rows, 128), F32),
        scratch_shapes=[pltpu.VMEM((nblk, rows, 128), I32)],
        compiler_params=pltpu.CompilerParams(vmem_limit_bytes=VMEM_LIMIT_BYTES),
        name="topk_rows",
    )(s)
    return jnp.transpose(m, (1, 0, 2)).reshape(rows, nblk * 128)[:, :n]


def _dsa_attn_kernel(pt_ref, q_ref, kn_ref, vn_ref, mn_ref, msk_ref, *refs, g_count):
    k_refs = refs[:g_count]
    v_refs = refs[g_count:2 * g_count]
    o_ref = refs[2 * g_count]
    m_scr, l_scr, acc_scr = refs[2 * g_count + 1:]
    j = pl.program_id(1)
    qbd, hsel = _head_rows(q_ref[0])

    @pl.when(j == 0)
    def _():
        sn = jnp.sum(qbd * kn_ref[0], axis=-1, keepdims=True) * (HEAD_DIM ** -0.5)
        ok = mn_ref[0][:, 0:1] > 0.0
        m_scr[...] = jnp.broadcast_to(jnp.where(ok, sn, NEG), (N_HEADS, 128))
        l_scr[...] = jnp.broadcast_to(jnp.where(ok, 1.0, 0.0), (N_HEADS, 128))
        acc_scr[...] = jnp.where(ok, jnp.broadcast_to(vn_ref[0], (N_HEADS, BR_W)), 0.0)

    kcat = jnp.concatenate([r[0] for r in k_refs], axis=0).astype(BF16)
    vcat = jnp.concatenate([r[0] for r in v_refs], axis=0).astype(BF16)
    mk = msk_ref[0] > 0.0
    s = jnp.where(mk, _dot_nt(qbd.astype(BF16), kcat) * (HEAD_DIM ** -0.5), NEG)
    m_old = m_scr[:, 0:1]
    m_new = jnp.maximum(m_old, jnp.max(s, axis=-1, keepdims=True))
    alpha = jnp.exp(m_old - m_new)
    p = jnp.where(mk, jnp.exp(s - m_new), 0.0)
    l_new = alpha * l_scr[:, 0:1] + jnp.sum(p, axis=-1, keepdims=True)
    acc = alpha * acc_scr[...] + _dot(p.astype(BF16), vcat)
    m_scr[...] = jnp.broadcast_to(m_new, (N_HEADS, 128))
    l_scr[...] = jnp.broadcast_to(l_new, (N_HEADS, 128))
    acc_scr[...] = acc

    @pl.when(j == pl.num_programs(1) - 1)
    def _():
        o_ref[0] = jnp.sum(jnp.where(hsel, acc / l_new, 0.0), axis=0, keepdims=True)


def _dsa_attn(q, k_new, v_new, mask, k_pool, v_pool, page_table):
    b, n_pages = page_table.shape
    g_count = min(PAGES_PER_STEP, n_pages)
    past = n_pages * PAGE_SIZE
    row = lambda a: a.reshape(b, 1, BR_W)
    m_new = jnp.broadcast_to(mask[:, past:past + 1], (b, 128)).reshape(b, 1, 128)
    m_past = mask[:, :past].reshape(b, 1, past)
    rspec = pl.BlockSpec((1, 1, BR_W), lambda i, j, pt: (i, 0, 0))
    grid_spec = pltpu.PrefetchScalarGridSpec(
        num_scalar_prefetch=1,
        grid=(b, n_pages // g_count),
        in_specs=[rspec, rspec, rspec,
                  pl.BlockSpec((1, 1, 128), lambda i, j, pt: (i, 0, 0)),
                  pl.BlockSpec((1, 1, g_count * PAGE_SIZE), lambda i, j, pt: (i, 0, j))]
                 + _page_specs((1, PAGE_SIZE, BR_W), g_count) * 2,
        out_specs=rspec,
        scratch_shapes=[pltpu.VMEM((N_HEADS, 128), F32), pltpu.VMEM((N_HEADS, 128), F32),
                        pltpu.VMEM((N_HEADS, BR_W), F32)])
    out = pl.pallas_call(
        functools.partial(_dsa_attn_kernel, g_count=g_count),
        grid_spec=grid_spec,
        out_shape=jax.ShapeDtypeStruct((b, 1, BR_W), F32),
        compiler_params=_cparams(("arbitrary", "arbitrary")),
        name="dsa_attn",
    )(page_table, row(q), row(k_new), row(v_new), m_new, m_past, *([k_pool] * g_count), *([v_pool] * g_count))
    return out.reshape(b, BR_W)


def _kbar_kernel(pt_ref, *refs, g_count):
    page_refs = refs[:g_count]
    o_ref = refs[g_count]
    for g2 in range(g_count // 2):
        s = (jnp.sum(page_refs[2 * g2][0].astype(F32), axis=0, keepdims=True)
             + jnp.sum(page_refs[2 * g2 + 1][0].astype(F32), axis=0, keepdims=True))
        o_ref[0, g2] = s * (1.0 / MOBA_BLOCK)


def _moba_kbar(k_pool, page_table):
    b, n_pages = page_table.shape
    g_count = min(PAGES_PER_STEP, n_pages)
    grid_spec = pltpu.PrefetchScalarGridSpec(
        num_scalar_prefetch=1,
        grid=(b, n_pages // g_count),
        in_specs=_page_specs((1, PAGE_SIZE, BR_W), g_count),
        out_specs=pl.BlockSpec((1, g_count // 2, 1, BR_W), lambda i, j, pt: (i, j, 0, 0)))
    out = pl.pallas_call(
        functools.partial(_kbar_kernel, g_count=g_count),
        grid_spec=grid_spec,
        out_shape=jax.ShapeDtypeStruct((b, n_pages // 2, 1, BR_W), F32),
        compiler_params=_cparams(("arbitrary", "arbitrary")),
        name="moba_kbar",
    )(page_table, *([k_pool] * g_count))
    return out.reshape(b, n_pages // 2, BR_W)


def _moba_gate_kernel(q_ref, kbar_ref, o_ref, *, n_past, nsel):
    qbd, _ = _head_rows(q_ref[0])
    g = _mm3_nt(qbd, kbar_ref[0])
    col = lax.broadcasted_iota(I32, (N_HEADS, 128), 1)
    g = jnp.where(col < n_past, g, -jnp.inf)
    out = jnp.full((N_HEADS, 128), -1, I32)
    for i in range(nsel):
        mx = jnp.max(g, axis=-1, keepdims=True)
        idx = jnp.min(jnp.where(g == mx, col, jnp.int32(1 << 20)), axis=-1, keepdims=True)
        ok = idx < n_past
        out = jnp.where(col == i, jnp.where(ok, idx, -1), out)
        g = jnp.where(col == idx, -jnp.inf, g)
    o_ref[0] = out


def _moba_gate(q, kbar, nsel):
    b, n_past, _ = kbar.shape
    kb = jnp.pad(kbar, ((0, 0), (0, 128 - n_past), (0, 0)))
    out = pl.pallas_call(
        functools.partial(_moba_gate_kernel, n_past=n_past, nsel=nsel),
        grid=(b,),
        in_specs=[pl.BlockSpec((1, 1, BR_W), lambda i: (i, 0, 0)),
                  pl.BlockSpec((1, 128, BR_W), lambda i: (i, 0, 0))],
        out_specs=pl.BlockSpec((1, N_HEADS, 128), lambda i: (i, 0, 0)),
        out_shape=jax.ShapeDtypeStruct((b, N_HEADS, 128), I32),
        compiler_params=_cparams(("arbitrary",)),
        name="moba_gate",
    )(q.reshape(b, 1, BR_W), kb)
    return out[:, :, :nsel]


def _moba_attn_kernel(sel_ref, pt_ref, q_ref, kn_ref, vn_ref, *refs, nsel):
    n_pg = 2 * nsel
    k_refs = refs[:n_pg]
    v_refs = refs[n_pg:2 * n_pg]
    o_ref = refs[2 * n_pg]
    bi = pl.program_id(0)
    h = pl.program_id(1)
    hm = lax.broadcasted_iota(I32, (1, BR_W), 1) // HEAD_DIM == h
    qh = jnp.where(hm, q_ref[0], 0.0)
    sn = jnp.sum(qh * kn_ref[0], axis=-1, keepdims=True) * (HEAD_DIM ** -0.5)
    kcat = jnp.concatenate([r[0] for r in k_refs], axis=0).astype(BF16)
    vcat = jnp.concatenate([r[0] for r in v_refs], axis=0).astype(BF16)
    s = _dot_nt(qh.astype(BF16), kcat) * (HEAD_DIM ** -0.5)
    blk = lax.broadcasted_iota(I32, s.shape, 1) // MOBA_BLOCK
    mk = jnp.zeros(s.shape, jnp.bool_)
    for i in range(nsel):
        mk = mk | (blk == jnp.where(sel_ref[(bi * N_HEADS + h) * nsel + i] >= 0, i, -1))
    s = jnp.where(mk, s, NEG)
    m = jnp.maximum(jnp.max(s, axis=-1, keepdims=True), sn)
    p = jnp.where(mk, jnp.exp(s - m), 0.0)
    pn = jnp.exp(sn - m)
    out = (_dot(p.astype(BF16), vcat) + pn * vn_ref[0]) / (jnp.sum(p, axis=-1, keepdims=True) + pn)

    @pl.when(h == 0)
    def _():
        o_ref[0] = jnp.zeros((1, BR_W), F32)

    o_ref[0] = o_ref[0] + jnp.where(hm, out, 0.0)


def _moba_attn(q, k_new, v_new, sel, k_pool, v_pool, page_table):
    b, n_pages = page_table.shape
    nsel = sel.shape[-1]
    row = lambda a: a.reshape(b, 1, BR_W)

    def pspec(i, half):
        def imap(bi, h, sel_ref, pt):
            blk = jnp.maximum(sel_ref[(bi * N_HEADS + h) * nsel + i], 0)
            return (pt[bi, 2 * blk + half], 0, 0)
        return pl.BlockSpec((1, PAGE_SIZE, BR_W), imap)

    pages = [pspec(i, half) for i in range(nsel) for half in range(2)]
    rspec = pl.BlockSpec((1, 1, BR_W), lambda bi, h, s, pt: (bi, 0, 0))
    grid_spec = pltpu.PrefetchScalarGridSpec(
        num_scalar_prefetch=2,
        grid=(b, N_HEADS),
        in_specs=[rspec, rspec, rspec] + pages + pages,
        out_specs=rspec)
    out = pl.pallas_call(
        functools.partial(_moba_attn_kernel, nsel=nsel),
        grid_spec=grid_spec,
        out_shape=jax.ShapeDtypeStruct((b, 1, BR_W), F32),
        compiler_params=_cparams(("arbitrary", "arbitrary")),
        name="moba_attn",
    )(sel.reshape(-1), page_table, row(q), row(k_new), row(v_new),
      *([k_pool] * (2 * nsel)), *([v_pool] * (2 * nsel)))
    return out.reshape(b, BR_W)


def _sample_layer(x, c, p, a_shift, a_wkv, b_ret, ck_pool, cv_pool, cki_pool, dk_pool, dv_pool, page_table):
    b = x.shape[0]
    n_pages = page_table.shape[1]
    past = n_pages * PAGE_SIZE
    assert x.shape[1] == 1 and past % MOBA_BLOCK == 0
    pos = jnp.full((b,), past, I32)
    tab_ret = _rope_tables(pos, HEAD_DIM, RET_THETA, BR_W)
    tab_std = _rope_tables(pos, ROPE_DIMS, ROPE_THETA, BR_W)
    w_hi, w_lo = _regroup_w_in(p['w_in'])
    mod = _ada(c, p['w_ada'], p['b_ada']).reshape(1, b, 3 * D_MODEL)
    xr = x.reshape(1, b, D_MODEL)
    outs = _layer_in(xr, mod, p['g_pre'], w_hi, w_lo, tab_ret, tab_std, b)
    (ua, gates, qb, kb, vb, qc, kc, vc, qi, ki4, wi, qd, kd, vd) = [o[0] for o in outs]
    r, w, k2, v, kk, bb, bonus = _rwkv_pre_rows(ua, a_shift, p)
    ya, wkv = _rwkv_step(r, w, k2, v, kk, bb, a_wkv)
    yb, ret = _retention_step(qb, kb, vb, b_ret, p['b_gn_w'], p['b_gn_b'])
    ki = ki4[:, :IDX_DIM]
    sc_past, sc_new = _dsa_scores(qi, wi[:, :IDX_HEADS], ki, cki_pool, page_table)
    total = past + 1
    mask = _topk_rows(jnp.concatenate([sc_past, sc_new], axis=1), min(DSA_TOPK_MAX, total // 4))
    yc = _dsa_attn(qc, kc, vc, mask, ck_pool, cv_pool, page_table)
    n_past_blocks = past // MOBA_BLOCK
    nsel = min(MOBA_TOPK, n_past_blocks)
    if nsel > 0:
        sel = _moba_gate(qd, _moba_kbar(dk_pool, page_table), nsel)
        yd = _moba_attn(qd, kd, vd, sel, dk_pool, dv_pool, page_table)
    else:
        yd = vd
    row = lambda a: a.reshape(1, b, -1)
    x_new = _layer_out(xr, mod, p, row(ya), row(bonus), row(yb), row(yc), row(yd), gates.reshape(1, b, -1), b)
    heads = lambda a: a.reshape(b, 1, N_HEADS, HEAD_DIM)
    new_state = (ua, wkv, ret, heads(kc), heads(vc), ki.reshape(b, 1, IDX_DIM), heads(kd), heads(vd))
    return x_new.reshape(b, 1, D_MODEL), new_state


def _prompt_layer(x, c, p):
    b, t, _ = x.shape
    pos = jnp.arange(t)
    tab_ret = _rope_tables(pos, HEAD_DIM, RET_THETA, BR_W)
    tab_std = _rope_tables(pos, ROPE_DIMS, ROPE_THETA, BR_W)
    w_hi, w_lo = _regroup_w_in(p['w_in'])
    mod = _ada(c, p['w_ada'], p['b_ada']).reshape(b, 1, 3 * D_MODEL)
    tm = MOBA_BLOCK
    assert t % tm == 0
    (ua, gates, qb, kb, vb, qc, kc, vc, qi, ki4, wi, qd, kd, vd, kcb, vct, kix, kdb, vdt, kbar) = _layer_in(
        x, mod, p['g_pre'], w_hi, w_lo, tab_ret, tab_std, tm, attention_operands=True)
    r, w, k2, v, kk, bb, bonus = _rwkv_pre_prompt(ua, jnp.zeros((b, A_SHIFT_W), F32), p, tm)
    ya, wkv = _rwkv_scan(r, w, k2, v, kk, bb, jnp.zeros((b, HEAD_DIM, BR_W), F32), min(64, t))
    yb, ret = _retention_prompt(qb, kb, vb, jnp.zeros((b, N_HEADS, HEAD_DIM, HEAD_DIM), F32),
                                p['b_gn_w'], p['b_gn_b'], min(256, t))
    yc = _dsa_prompt(qc, qi, wi, kcb, vct, kix)
    yd = _moba_prompt(qd, kbar.reshape(b, t // tm, BR_W), kdb, vdt)
    x_new = _layer_out(x, mod, p, ya, bonus, yb, yc, yd, gates, tm)
    heads = lambda a: a.reshape(b, t, N_HEADS, HEAD_DIM)
    new_state = (ua[:, -1], _wkv_from_scan_layout(wkv), ret, heads(kc), heads(vc), ki4[..., :IDX_DIM],
                 heads(kd), heads(vd))
    return x_new, new_state


_PARAM_NAMES = ('w_ada', 'b_ada', 'g_pre', 'g_post', 'w_in', 'a_mu', 'a_w0', 'a_w2', 'a_a0', 'a_a2', 'a_kk', 'a_ka',
                'a_rk', 'a_ln_w', 'a_ln_b', 'b_gn_w', 'b_gn_b', 'w_branch', 'w_merge', 'w_out')


def kernel(x_prompt, x_sample, c_prompt, c_sample, state_a_shift, state_a_wkv, state_b_ret, cache_c_k, cache_c_v,
           cache_c_kidx, cache_d_k, cache_d_v, page_table, w_ada, b_ada, g_pre, g_post, w_in, a_mu, a_w0, a_w2,
           a_a0, a_a2, a_kk, a_ka, a_rk, a_ln_w, a_ln_b, b_gn_w, b_gn_b, w_branch, w_merge, w_out):
    stacked = dict(zip(_PARAM_NAMES, (w_ada, b_ada, g_pre, g_post, w_in, a_mu, a_w0, a_w2, a_a0, a_a2, a_kk, a_ka,
                                      a_rk, a_ln_w, a_ln_b, b_gn_w, b_gn_b, w_branch, w_merge, w_out)))
    depth = w_in.shape[0]
    n_pool = cache_c_k.shape[1]
    fold = lambda a: a.reshape(depth * n_pool, PAGE_SIZE, -1)
    ck, cv, cki, dk, dv = (fold(a) for a in (cache_c_k, cache_c_v, cache_c_kidx, cache_d_k, cache_d_v))
    xp, xs = x_prompt, x_sample
    p_new, s_new = [], []
    for l in range(depth):
        p = {name: val[l] for name, val in stacked.items()}
        xp, st_p = _prompt_layer(xp, c_prompt, p)
        xs, st_s = _sample_layer(xs, c_sample, p, state_a_shift[l], state_a_wkv[l], state_b_ret[l],
                                 ck, cv, cki, dk, dv, page_table + l * n_pool)
        p_new.append(st_p)
        s_new.append(st_s)
    stack = lambda states, i: jnp.stack([s[i] for s in states])
    return ((xp, xs) + tuple(stack(p_new, i) for i in range(8)) + tuple(stack(s_new, i) for i in range(8)))
```

```python
import functools

import jax
import jax.numpy as jnp
import numpy as np
from jax import lax
from jax.experimental import pallas as pl
from jax.experimental.pallas import tpu as pltpu

F32 = jnp.float32
BF16 = jnp.bfloat16
I32 = jnp.int32

D_MODEL = 1024
PAGE_SIZE = 128
BR_W = 256
HEAD_DIM = 64
N_HEADS = 4
LORA_W = 64
LORA_A = 64
A_SHIFT_W = 3 * BR_W + LORA_W + LORA_A
ROPE_THETA = 500000.0
ROPE_DIMS = HEAD_DIM // 4
RET_THETA = 10000.0
IDX_HEADS = 4
IDX_DIM = 64
DSA_TOPK_MAX = 256
MOBA_BLOCK = 256
MOBA_TOPK = 3
RMS_EPS = 1e-6
RWKV_GN_EPS = 64e-5
RET_GN_EPS = 1e-5
NEG = -1e30
VMEM_LIMIT_BYTES = 56 * 1024 * 1024
KEY_GROUP = 4

_C = {}
_off = 0
for _name, _n in (('a_r', 256), ('a_k', 256), ('a_v', 256), ('a_wl', 64), ('a_al', 64), ('a_g', 256),
                  ('b_q', 256), ('b_k', 256), ('b_v', 256), ('b_g', 256),
                  ('c_q', 256), ('c_k', 256), ('c_v', 256), ('c_qi', 256), ('c_ki', 64),
                  ('c_wi', 4), ('c_g', 256),
                  ('d_q', 256), ('d_k', 256), ('d_v', 256), ('d_g', 256)):
    _C[_name] = (_off, _off + _n)
    _off += _n

W_A, W_G, W_B, W_C, W_I, W_D = 896, 1024, 768, 768, 640, 768
OFF_A = 0
OFF_G = OFF_A + W_A
OFF_B = OFF_G + W_G
OFF_C = OFF_B + W_B
OFF_I = OFF_C + W_C
OFF_D = OFF_I + W_I
W_ALL = OFF_D + W_D


def _cparams(sem):
    return pltpu.CompilerParams(dimension_semantics=sem, vmem_limit_bytes=VMEM_LIMIT_BYTES)


def _split(x):
    hi = x.astype(BF16)
    lo = (x - hi.astype(F32)).astype(BF16)
    return hi, lo


def _dot(a, b):
    return jnp.dot(a, b, preferred_element_type=F32)


def _dot_nt(a, b):
    return lax.dot_general(a, b, (((1,), (1,)), ((), ())), preferred_element_type=F32)


def _mm1(a, b):
    return _dot(a.astype(BF16), b.astype(BF16))


def _mm3(a, b_hi, b_lo):
    a_hi, a_lo = _split(a)
    return _dot(a_hi, b_hi) + (_dot(a_lo, b_hi) + _dot(a_hi, b_lo))


def _mm3_nt(a, b):
    a_hi, a_lo = _split(a)
    b_hi, b_lo = _split(b)
    return _dot_nt(a_hi, b_hi) + (_dot_nt(a_lo, b_hi) + _dot_nt(a_hi, b_lo))


def _head_ones():
    r = lax.broadcasted_iota(I32, (BR_W, BR_W), 0) // HEAD_DIM
    c = lax.broadcasted_iota(I32, (BR_W, BR_W), 1) // HEAD_DIM
    return jnp.where(r == c, 1.0, 0.0).astype(BF16)


def _headsum(x, bd):
    hi, lo = _split(x)
    return _dot(hi, bd) + _dot(lo, bd)


def _head_mask(h, shape):
    c = lax.broadcasted_iota(I32, shape, len(shape) - 1) // HEAD_DIM
    return c == h


def _ada_kernel(c_ref, w_ref, b_ref, o_ref):
    w = w_ref[...]
    w_hi, w_lo = _split(w)
    o_ref[...] = _mm3(c_ref[...], w_hi, w_lo) + b_ref[...]


def _ada(c, w_ada, b_ada):
    bc = c.shape[0]
    n = w_ada.shape[1]
    tn = 1024
    return pl.pallas_call(
        _ada_kernel,
        grid=(n // tn,),
        in_specs=[pl.BlockSpec((bc, D_MODEL), lambda j: (0, 0)),
                  pl.BlockSpec((D_MODEL, tn), lambda j: (0, j)),
                  pl.BlockSpec((1, tn), lambda j: (0, j))],
        out_specs=pl.BlockSpec((bc, tn), lambda j: (0, j)),
        out_shape=jax.ShapeDtypeStruct((bc, n), F32),
        compiler_params=_cparams(("arbitrary",)),
        name="ada",
    )(c, w_ada, b_ada.reshape(1, n))


def _rope_tables(pos, rot_dims, theta, width):
    half = rot_dims // 2
    inv = jnp.power(jnp.float32(theta), -jnp.arange(half, dtype=jnp.float32) / half)
    ang = pos.astype(jnp.float32)[:, None] * inv[None, :]
    cos = jnp.cos(ang)
    sin = jnp.sin(ang)
    t = pos.shape[0]
    one = jnp.ones((t, HEAD_DIM - rot_dims), F32)
    zero = jnp.zeros((t, HEAD_DIM - rot_dims), F32)
    zh = jnp.zeros((t, half), F32)
    cos_h = jnp.concatenate([cos, cos, one], axis=1)
    up_h = jnp.concatenate([-sin, zh, zero], axis=1)
    dn_h = jnp.concatenate([zh, sin, zero], axis=1)
    reps = width // HEAD_DIM
    return jnp.stack([jnp.tile(cos_h, (1, reps)), jnp.tile(up_h, (1, reps)), jnp.tile(dn_h, (1, reps))])


def _rope_apply(x, tab_ref, lo, hi, half):
    n = hi - lo
    cos = tab_ref[0, :, lo:hi]
    up = tab_ref[1, :, lo:hi]
    dn = tab_ref[2, :, lo:hi]
    return x * cos + pltpu.roll(x, n - half, 1) * up + pltpu.roll(x, half, 1) * dn


def _indexer_key_operand(ki4):
    k_hi, k_lo = _split(ki4)
    lane = lax.broadcasted_iota(I32, ki4.shape, 1)
    return jnp.where(lane < 2 * IDX_DIM, k_hi, jnp.where(lane < 3 * IDX_DIM, k_lo, jnp.zeros_like(k_lo)))


def _layer_in_kernel(x_ref, mod_ref, g_ref, wh_ref, wl_ref, tr_ref, ts_ref,
                     ua_ref, gt_ref, qb_ref, kb_ref, vb_ref, qc_ref, kc_ref, vc_ref,
                     qi_ref, ki_ref, wi_ref, qd_ref, kd_ref, vd_ref, *extra_refs):
    x = x_ref[0]
    y = x * lax.rsqrt(jnp.mean(x * x, axis=-1, keepdims=True) + RMS_EPS) * g_ref[...]
    shift = mod_ref[0, :, 0:D_MODEL]
    scale = mod_ref[0, :, D_MODEL:2 * D_MODEL]
    h = y * (1.0 + scale) + shift
    h_hi, h_lo = _split(h)

    def proj3(lo, hi):
        b_hi = wh_ref[:, lo:hi]
        b_lo = wl_ref[:, lo:hi]
        return _dot(h_hi, b_hi) + (_dot(h_lo, b_hi) + _dot(h_hi, b_lo))

    def proj1(lo, hi):
        return _dot(h_hi, wh_ref[:, lo:hi])

    ua_ref[0] = proj1(OFF_A, OFF_A + W_A)
    gt_ref[0] = proj1(OFF_G, OFF_G + W_G)
    ub = proj1(OFF_B, OFF_B + W_B)
    qb_ref[0] = _rope_apply(ub[:, 0:256], tr_ref, 0, 256, HEAD_DIM // 2)
    kb_ref[0] = _rope_apply(ub[:, 256:512], tr_ref, 0, 256, HEAD_DIM // 2)
    vb_ref[0] = ub[:, 512:768]
    uc = proj1(OFF_C, OFF_C + W_C)
    qc_ref[0] = _rope_apply(uc[:, 0:256], ts_ref, 0, 256, ROPE_DIMS // 2)
    kc = _rope_apply(uc[:, 256:512], ts_ref, 0, 256, ROPE_DIMS // 2)
    kc_ref[0] = kc
    vc = uc[:, 512:768]
    vc_ref[0] = vc
    ui = proj3(OFF_I, OFF_I + W_I)
    qi_ref[0] = _rope_apply(ui[:, 0:256], ts_ref, 0, 256, ROPE_DIMS // 2)
    ki4 = _rope_apply(ui[:, 256:512], ts_ref, 0, 256, ROPE_DIMS // 2)
    ki_ref[0] = ki4
    wi_ref[0] = ui[:, 512:640]
    uqk = proj3(OFF_D, OFF_D + 512)
    qd_ref[0] = _rope_apply(uqk[:, 0:256], ts_ref, 0, 256, ROPE_DIMS // 2)
    kd = _rope_apply(uqk[:, 256:512], ts_ref, 0, 256, ROPE_DIMS // 2)
    kd_ref[0] = kd
    vd = proj1(OFF_D + 512, OFF_D + W_D)
    vd_ref[0] = vd
    if extra_refs:
        kcb_ref, vct_ref, kix_ref, kdb_ref, vdt_ref, kbar_ref = extra_refs
        kcb_ref[0] = kc.astype(BF16)
        vct_ref[0, 0] = vc.T.astype(BF16)
        kix_ref[0] = _indexer_key_operand(ki4)
        kdb_ref[0] = kd.astype(BF16)
        vdt_ref[0, 0] = vd.T.astype(BF16)
        kbar_ref[0, 0] = jnp.sum(kd, axis=0, keepdims=True) * (1.0 / MOBA_BLOCK)


def _regroup_w_in(w_in):
    def cols(name):
        lo, hi = _C[name]
        return w_in[:, lo:hi]
    ki4 = jnp.tile(cols('c_ki'), (1, 4))
    wi_pad = jnp.pad(cols('c_wi'), ((0, 0), (0, 124)))
    w = jnp.concatenate([
        cols('a_r'), cols('a_k'), cols('a_v'), cols('a_wl'), cols('a_al'),
        cols('a_g'), cols('b_g'), cols('c_g'), cols('d_g'),
        cols('b_q'), cols('b_k'), cols('b_v'),
        cols('c_q'), cols('c_k'), cols('c_v'),
        cols('c_qi'), ki4, wi_pad,
        cols('d_q'), cols('d_k'), cols('d_v')], axis=1)
    return _split(w)


def _layer_in(x, mod, g_pre, w_hi, w_lo, tab_ret, tab_std, tm, attention_operands=False):
    b, t, _ = x.shape
    r = mod.shape[1]
    if r == 1:
        mod_spec = pl.BlockSpec((1, 1, 3 * D_MODEL), lambda i, j: (i, 0, 0))
    else:
        mod_spec = pl.BlockSpec((1, tm, 3 * D_MODEL), lambda i, j: (i, j, 0))
    widths = (W_A, W_G, 256, 256, 256, 256, 256, 256, 256, 256, 128, 256, 256, 256)
    seq = lambda w: pl.BlockSpec((1, tm, w), lambda i, j: (i, j, 0))
    out_shape = tuple(jax.ShapeDtypeStruct((b, t, w), F32) for w in widths)
    out_specs = tuple(seq(w) for w in widths)
    if attention_operands:
        rows = jax.ShapeDtypeStruct((b, t, BR_W), BF16)
        cols = jax.ShapeDtypeStruct((b, t // tm, BR_W, tm), BF16)
        col_spec = pl.BlockSpec((1, 1, BR_W, tm), lambda i, j: (i, j, 0, 0))
        out_shape += (rows, cols, rows, rows, cols, jax.ShapeDtypeStruct((b, t // tm, 1, BR_W), F32))
        out_specs += (seq(BR_W), col_spec, seq(BR_W), seq(BR_W), col_spec,
                      pl.BlockSpec((1, 1, 1, BR_W), lambda i, j: (i, j, 0, 0)))
    return pl.pallas_call(
        _layer_in_kernel,
        grid=(b, t // tm),
        in_specs=[pl.BlockSpec((1, tm, D_MODEL), lambda i, j: (i, j, 0)),
                  mod_spec,
                  pl.BlockSpec((1, D_MODEL), lambda i, j: (0, 0)),
                  pl.BlockSpec((D_MODEL, W_ALL), lambda i, j: (0, 0), pipeline_mode=pl.Buffered(1)),
                  pl.BlockSpec((D_MODEL, W_ALL), lambda i, j: (0, 0), pipeline_mode=pl.Buffered(1)),
                  pl.BlockSpec((3, tm, 256), lambda i, j: (0, j, 0)),
                  pl.BlockSpec((3, tm, 256), lambda i, j: (0, j, 0))],
        out_specs=out_specs,
        out_shape=out_shape,
        compiler_params=_cparams(("arbitrary", "arbitrary")),
        name="layer_in",
    )(x, mod, g_pre.reshape(1, D_MODEL), w_hi, w_lo, tab_ret, tab_std)


def _rwkv_pre_math(ua, prev, mu, w0, w2h, w2l, a0, a2h, a2l, kkp, ka, rk, bd):
    xs = ua + (prev - ua) * mu
    r = xs[:, 0:256]
    k = xs[:, 256:512]
    v = xs[:, 512:768]
    wl = xs[:, 768:832]
    al = xs[:, 832:896]
    zw = w0 + _mm3(jnp.tanh(wl), w2h, w2l)
    w_log = -jax.nn.softplus(-zw) - 0.5
    decay = jnp.exp(-jnp.exp(w_log))
    a = jax.nn.sigmoid(a0 + _mm3(al, a2h, a2l))
    kq = k * kkp
    kk = kq * lax.rsqrt(_headsum(kq * kq, bd) + 1e-12)
    k2 = k * (1.0 + (a - 1.0) * ka)
    bonus = _headsum(r * k2 * rk, bd) * v
    return r, decay, k2, v, kk, kk * a, bonus


def _rwkv_pre_shift_kernel(ua_ref, up_ref, p0_ref, mu_ref, w0_ref, w2h_ref, w2l_ref, a0_ref, a2h_ref, a2l_ref,
                           kkp_ref, ka_ref, rk_ref, r_ref, w_ref, k_ref, v_ref, kk_ref, b_ref, bo_ref):
    j = pl.program_id(1)
    ua = ua_ref[0]
    tm = ua.shape[0]
    first = jnp.where(j == 0, p0_ref[0], up_ref[0, 7:8, :])
    row = lax.broadcasted_iota(I32, ua.shape, 0)
    prev = jnp.where(row == 0, first, pltpu.roll(ua, 1, 0))
    outs = _rwkv_pre_math(ua, prev, mu_ref[...], w0_ref[...], w2h_ref[...], w2l_ref[...], a0_ref[...],
                          a2h_ref[...], a2l_ref[...], kkp_ref[...], ka_ref[...], rk_ref[...], _head_ones())
    for o_ref, o in zip((r_ref, w_ref, k_ref, v_ref, kk_ref, b_ref, bo_ref), outs):
        o_ref[0] = o


def _rwkv_pre_rows_kernel(ua_ref, pv_ref, mu_ref, w0_ref, w2h_ref, w2l_ref, a0_ref, a2h_ref, a2l_ref,
                          kkp_ref, ka_ref, rk_ref, r_ref, w_ref, k_ref, v_ref, kk_ref, b_ref, bo_ref):
    outs = _rwkv_pre_math(ua_ref[...], pv_ref[...], mu_ref[...], w0_ref[...], w2h_ref[...], w2l_ref[...],
                          a0_ref[...], a2h_ref[...], a2l_ref[...], kkp_ref[...], ka_ref[...], rk_ref[...],
                          _head_ones())
    for o_ref, o in zip((r_ref, w_ref, k_ref, v_ref, kk_ref, b_ref, bo_ref), outs):
        o_ref[...] = o


def _rwkv_params(p):
    w2h, w2l = _split(p['a_w2'])
    a2h, a2l = _split(p['a_a2'])
    row = lambda v: v.reshape(1, -1)
    return (row(p['a_mu']), row(p['a_w0']), w2h, w2l, row(p['a_a0']), a2h, a2l,
            row(p['a_kk']), row(p['a_ka']), row(p['a_rk']))


def _rwkv_pre_prompt(ua, prev0, p, tm):
    b, t, _ = ua.shape
    prm = _rwkv_params(p)
    full = lambda a: pl.BlockSpec(a.shape, lambda i, j: (0,) * a.ndim)
    blk8 = tm // 8
    return pl.pallas_call(
        _rwkv_pre_shift_kernel,
        grid=(b, t // tm),
        in_specs=[pl.BlockSpec((1, tm, A_SHIFT_W), lambda i, j: (i, j, 0)),
                  pl.BlockSpec((1, 8, A_SHIFT_W), lambda i, j: (i, jnp.maximum(j * blk8 - 1, 0), 0)),
                  pl.BlockSpec((1, 1, A_SHIFT_W), lambda i, j: (i, 0, 0))] + [full(a) for a in prm],
        out_specs=tuple(pl.BlockSpec((1, tm, BR_W), lambda i, j: (i, j, 0)) for _ in range(7)),
        out_shape=tuple(jax.ShapeDtypeStruct((b, t, BR_W), F32) for _ in range(7)),
        compiler_params=_cparams(("arbitrary", "arbitrary")),
        name="rwkv_pre",
    )(ua, ua, prev0.reshape(b, 1, A_SHIFT_W), *prm)


def _rwkv_pre_rows(ua, prev, p):
    n = ua.shape[0]
    prm = _rwkv_params(p)
    full = lambda a: pl.BlockSpec(a.shape, lambda: (0,) * a.ndim)
    return pl.pallas_call(
        _rwkv_pre_rows_kernel,
        in_specs=[full(ua), full(prev)] + [full(a) for a in prm],
        out_specs=tuple(pl.BlockSpec((n, BR_W), lambda: (0, 0)) for _ in range(7)),
        out_shape=tuple(jax.ShapeDtypeStruct((n, BR_W), F32) for _ in range(7)),
        name="rwkv_pre_rows",
    )(ua, prev, *prm)


def _rwkv_scan_kernel(r_ref, w_ref, k_ref, v_ref, kk_ref, b_ref, s0_ref, y_ref, sout_ref,
                      s_scr, z_scr, be_scr, ga_scr, *, tc):
    c = pl.program_id(0)

    @pl.when(c == 0)
    def _():
        s_scr[...] = s0_ref[...]

    nb = s_scr.shape[0]
    bd = _head_ones()
    vi = lax.broadcasted_iota(I32, (HEAD_DIM, BR_W), 0)
    ci = lax.broadcasted_iota(I32, (HEAD_DIM, BR_W), 1)
    diag = ((ci % HEAD_DIM) == vi)[None]

    for bi in range(nb):
        kk_next = pltpu.roll(kk_ref[bi], tc - 1, 0)
        z_scr[bi] = w_ref[bi] * kk_next
        be_scr[bi] = _headsum(b_ref[bi] * kk_next, bd)
        ga_scr[bi] = _headsum(k_ref[bi] * kk_next, bd)

    def pair(i, carry):
        t = 2 * i
        row = lambda ref, tt: ref[:, pl.ds(tt, 1), :]
        s = s_scr[...]
        lhs = jnp.concatenate([
            (s * row(kk_ref, t)).astype(BF16),
            (s * row(z_scr, t)).astype(BF16),
            jnp.where(diag, row(v_ref, t), 0.0).astype(BF16),
            jnp.where(diag, row(v_ref, t + 1), 0.0).astype(BF16)], axis=1)
        res = _dot(lhs.reshape(nb * 4 * HEAD_DIM, BR_W), bd).reshape(nb, 4 * HEAD_DIM, BR_W)
        sk1 = res[:, 0:HEAD_DIM]
        vc1 = res[:, 2 * HEAD_DIM:3 * HEAD_DIM]
        vc2 = res[:, 3 * HEAD_DIM:4 * HEAD_DIM]
        sk2 = res[:, HEAD_DIM:2 * HEAD_DIM] - sk1 * row(be_scr, t) + vc1 * row(ga_scr, t)
        s1 = s * row(w_ref, t) - sk1 * row(b_ref, t) + vc1 * row(k_ref, t)
        s2 = s1 * row(w_ref, t + 1) - sk2 * row(b_ref, t + 1) + vc2 * row(k_ref, t + 1)
        s_scr[...] = s2
        q = jnp.concatenate([(s1 * row(r_ref, t)).astype(BF16), (s2 * row(r_ref, t + 1)).astype(BF16)], axis=1)
        yb = _dot(q.reshape(nb * 2 * HEAD_DIM, BR_W), bd).reshape(nb, 2 * HEAD_DIM, BR_W)
        y_ref[:, pl.ds(t, 1), :] = jnp.sum(jnp.where(diag, yb[:, :HEAD_DIM], 0.0), axis=1, keepdims=True)
        y_ref[:, pl.ds(t + 1, 1), :] = jnp.sum(jnp.where(diag, yb[:, HEAD_DIM:], 0.0), axis=1, keepdims=True)
        return carry

    lax.fori_loop(0, tc // 2, pair, 0, unroll=8)

    @pl.when(c == pl.num_programs(0) - 1)
    def _():
        sout_ref[...] = s_scr[...]


def _rwkv_scan(r, w, k, v, kk, bb, s0, tc):
    b, t, _ = r.shape
    assert tc % 2 == 0 and t % tc == 0
    seq = pl.BlockSpec((b, tc, BR_W), lambda c: (0, c, 0))
    st = pl.BlockSpec((b, HEAD_DIM, BR_W), lambda c: (0, 0, 0))
    return pl.pallas_call(
        functools.partial(_rwkv_scan_kernel, tc=tc),
        grid=(t // tc,),
        in_specs=[seq] * 6 + [st],
        out_specs=(seq, st),
        out_shape=(jax.ShapeDtypeStruct((b, t, BR_W), F32), jax.ShapeDtypeStruct((b, HEAD_DIM, BR_W), F32)),
        scratch_shapes=[pltpu.VMEM((b, HEAD_DIM, BR_W), F32)] + [pltpu.VMEM((b, tc, BR_W), F32)] * 3,
        compiler_params=_cparams(("arbitrary",)),
        name="rwkv_scan",
    )(r, w, k, v, kk, bb, s0)


def _wkv_to_scan_layout(s):
    b = s.shape[0]
    return jnp.transpose(s, (0, 2, 1, 3)).reshape(b, HEAD_DIM, BR_W)


def _wkv_from_scan_layout(s):
    b = s.shape[0]
    return jnp.transpose(s.reshape(b, HEAD_DIM, N_HEADS, HEAD_DIM), (0, 2, 1, 3))


def _rwkv_step_kernel(r_ref, w_ref, k_ref, vc_ref, kk_ref, b_ref, s_ref, y_ref, so_ref):
    s = s_ref[...]
    sk = jnp.sum(s * kk_ref[...], axis=-1, keepdims=True)
    s = s * w_ref[...] - sk * b_ref[...] + vc_ref[...] * k_ref[...]
    so_ref[...] = s
    y_ref[...] = jnp.sum(s * r_ref[...], axis=-1, keepdims=True)


def _rwkv_step(r, w, k, v, kk, bb, s0):
    b = r.shape[0]
    n = b * N_HEADS
    rowf = lambda a: a.reshape(n, 1, HEAD_DIM)
    full = lambda shape: pl.BlockSpec(shape, lambda: (0,) * len(shape))
    y, s = pl.pallas_call(
        _rwkv_step_kernel,
        in_specs=[full((n, 1, HEAD_DIM))] * 3 + [full((n, HEAD_DIM, 1))] + [full((n, 1, HEAD_DIM))] * 2
                 + [full((n, HEAD_DIM, HEAD_DIM))],
        out_specs=(full((n, HEAD_DIM, 1)), full((n, HEAD_DIM, HEAD_DIM))),
        out_shape=(jax.ShapeDtypeStruct((n, HEAD_DIM, 1), F32), jax.ShapeDtypeStruct((n, HEAD_DIM, HEAD_DIM), F32)),
        compiler_params=pltpu.CompilerParams(vmem_limit_bytes=VMEM_LIMIT_BYTES),
        name="rwkv_step",
    )(rowf(r), rowf(w), rowf(k), v.reshape(n, HEAD_DIM, 1), rowf(kk), rowf(bb), s0.reshape(n, HEAD_DIM, HEAD_DIM))
    return y.reshape(b, BR_W), s.reshape(b, N_HEADS, HEAD_DIM, HEAD_DIM)


def _ret_tables(c):
    log_g = jnp.log(1.0 - jnp.power(2.0, -5.0 - jnp.arange(N_HEADS, dtype=jnp.float32)))
    i = jnp.arange(c, dtype=jnp.float32)
    diff = i[:, None] - i[None, :]
    dmat = jnp.where(diff[None] >= 0, jnp.exp(jnp.maximum(diff, 0.0)[None] * log_g[:, None, None]), 0.0)
    dq = jnp.exp((i[:, None] + 1.0) * log_g[None, :])
    dk = jnp.exp((c - 1.0 - i)[:, None] * log_g[None, :])
    ds = jnp.exp(c * log_g)
    lanes = lambda a: jnp.repeat(a, HEAD_DIM, axis=-1)
    return dmat, lanes(dq), lanes(dk), lanes(ds[None, :])


def _ret_kernel(q_ref, k_ref, v_ref, s0_ref, dm_ref, dq_ref, dk_ref, ds_ref, gw_ref, gb_ref,
                y_ref, so_ref, s_scr):
    j = pl.program_id(1)

    @pl.when(j == 0)
    def _():
        s_scr[...] = s0_ref[0]

    q = q_ref[0]
    k = k_ref[0] * (HEAD_DIM ** -0.5)
    v = v_ref[0]
    s = s_scr[...]
    bd = _head_ones()
    qb = q.astype(BF16)
    kb = k.astype(BF16)
    vb = v.astype(BF16)
    y = _dot(qb, s.astype(BF16)) * dq_ref[...]
    for h in range(N_HEADS):
        hm = _head_mask(h, (1, BR_W))
        att = _dot_nt(jnp.where(hm, q, 0.0).astype(BF16), kb) * dm_ref[h]
        y = y + jnp.where(hm, _dot(att.astype(BF16), vb), 0.0)
    kd = (k * dk_ref[...]).T.astype(BF16)
    s_scr[...] = s * ds_ref[...] + bd.astype(F32) * _dot(kd, vb)
    mu = _headsum(y, bd) * (1.0 / HEAD_DIM)
    d = y - mu
    var = _headsum(d * d, bd) * (1.0 / HEAD_DIM)
    y_ref[0] = d * lax.rsqrt(var + RET_GN_EPS) * gw_ref[...] + gb_ref[...]

    @pl.when(j == pl.num_programs(1) - 1)
    def _():
        so_ref[0] = s_scr[...]


def _ret_state_embed(s):
    b = s.shape[0]
    eye = jnp.eye(N_HEADS, dtype=s.dtype)
    return jnp.einsum('bhde,hg->bhdge', s, eye).reshape(b, BR_W, BR_W)


def _ret_state_extract(s):
    b = s.shape[0]
    s4 = s.reshape(b, N_HEADS, HEAD_DIM, N_HEADS, HEAD_DIM)
    return jnp.stack([s4[:, h, :, h, :] for h in range(N_HEADS)], axis=1)


def _retention_prompt(q, k, v, s0, gn_w, gn_b, c):
    b, t, _ = q.shape
    dmat, dq, dk, ds = _ret_tables(c)
    seq = pl.BlockSpec((1, c, BR_W), lambda i, j: (i, j, 0))
    st = pl.BlockSpec((1, BR_W, BR_W), lambda i, j: (i, 0, 0))
    const = lambda a: pl.BlockSpec(a.shape, lambda i, j: (0,) * a.ndim)
    gw = gn_w.reshape(1, BR_W)
    gb = gn_b.reshape(1, BR_W)
    y, s = pl.pallas_call(
        _ret_kernel,
        grid=(b, t // c),
        in_specs=[seq, seq, seq, st, const(dmat), const(dq), const(dk), const(ds), const(gw), const(gb)],
        out_specs=(seq, st),
        out_shape=(jax.ShapeDtypeStruct((b, t, BR_W), F32), jax.ShapeDtypeStruct((b, BR_W, BR_W), F32)),
        scratch_shapes=[pltpu.VMEM((BR_W, BR_W), F32)],
        compiler_params=_cparams(("arbitrary", "arbitrary")),
        name="retention",
    )(q, k, v, _ret_state_embed(s0), dmat, dq, dk, ds, gw, gb)
    return y, _ret_state_extract(s)


def _ret_step_kernel(qc_ref, kc_ref, qr_ref, kr_ref, v_ref, s_ref, g_ref, gw_ref, gb_ref, y_ref, so_ref):
    s = s_ref[...]
    g = g_ref[...]
    v = v_ref[...]
    qk = jnp.sum(qr_ref[...] * kr_ref[...], axis=-1, keepdims=True)
    y = qk * v + jnp.sum(qc_ref[...] * s, axis=1, keepdims=True) * g
    so_ref[...] = s * g + kc_ref[...] * v
    mu = jnp.mean(y, axis=-1, keepdims=True)
    d = y - mu
    var = jnp.mean(d * d, axis=-1, keepdims=True)
    y_ref[...] = d * lax.rsqrt(var + RET_GN_EPS) * gw_ref[...] + gb_ref[...]


def _retention_step(q, k, v, s0, gn_w, gn_b):
    b = q.shape[0]
    n = b * N_HEADS
    ks = k * (HEAD_DIM ** -0.5)
    g = 1.0 - jnp.power(2.0, -5.0 - jnp.arange(N_HEADS, dtype=jnp.float32))
    g = jnp.tile(g, (b,)).reshape(n, 1, 1)
    gw = jnp.tile(gn_w.reshape(N_HEADS, HEAD_DIM), (b, 1)).reshape(n, 1, HEAD_DIM)
    gb = jnp.tile(gn_b.reshape(N_HEADS, HEAD_DIM), (b, 1)).reshape(n, 1, HEAD_DIM)
    col = lambda a: a.reshape(n, HEAD_DIM, 1)
    row = lambda a: a.reshape(n, 1, HEAD_DIM)
    args = (col(q), col(ks), row(q), row(ks), row(v), s0.reshape(n, HEAD_DIM, HEAD_DIM), g, gw, gb)
    full = lambda a: pl.BlockSpec(a.shape, lambda: (0,) * a.ndim)
    y, s = pl.pallas_call(
        _ret_step_kernel,
        in_specs=[full(a) for a in args],
        out_specs=(pl.BlockSpec((n, 1, HEAD_DIM), lambda: (0, 0, 0)),
                   pl.BlockSpec((n, HEAD_DIM, HEAD_DIM), lambda: (0, 0, 0))),
        out_shape=(jax.ShapeDtypeStruct((n, 1, HEAD_DIM), F32), jax.ShapeDtypeStruct((n, HEAD_DIM, HEAD_DIM), F32)),
        compiler_params=pltpu.CompilerParams(vmem_limit_bytes=VMEM_LIMIT_BYTES),
        name="retention_step",
    )(*args)
    return y.reshape(b, BR_W), s.reshape(b, N_HEADS, HEAD_DIM, HEAD_DIM)


INT_MIN = -2 ** 31


def _sort_key(s):
    s = jnp.where(s == 0.0, 0.0, s)
    bits = lax.bitcast_convert_type(s, I32)
    return jnp.where(bits < 0, bits ^ jnp.int32(0x7FFFFFFF), bits)


def _kth_largest_key(count_ge, shape, k):
    def bit_step(i, tb):
        cand = tb + jnp.left_shift(jnp.int32(1), 31 - i)
        return jnp.where(count_ge(cand) >= k, cand, tb)

    return lax.fori_loop(0, 32, bit_step, jnp.full(shape, INT_MIN, I32))


def _attn_queries(qf, qa_scr):
    qs = qf * (HEAD_DIM ** -0.5)
    for h in range(N_HEADS):
        qa_scr[h] = jnp.where(_head_mask(h, (1, BR_W)), qs, 0.0).astype(BF16)


def _flash_update_t(qa_scr, kb, vt, msk, stats, acc):
    new_stats = []
    parts = []
    for h in range(N_HEADS):
        m_old, l_old = stats[h]
        mk = msk[h] if isinstance(msk, (list, tuple)) else msk
        s = jnp.where(mk, _dot_nt(kb, qa_scr[h]), NEG)
        m_new = jnp.maximum(m_old, jnp.max(s, axis=0, keepdims=True))
        alpha = jnp.exp(m_old - m_new)
        p = jnp.exp(s - m_new)
        l_new = alpha * l_old + jnp.sum(p, axis=0, keepdims=True)
        rows = slice(h * HEAD_DIM, (h + 1) * HEAD_DIM)
        parts.append(acc[rows] * alpha + _dot(vt[rows], p.astype(BF16)))
        new_stats.append((m_new, l_new))
    return tuple(new_stats), jnp.concatenate(parts, axis=0)


def _flash_init_t(tq):
    stats = tuple((jnp.full((1, tq), NEG, F32), jnp.zeros((1, tq), F32)) for _ in range(N_HEADS))
    return stats, jnp.zeros((BR_W, tq), F32)


def _flash_finish_t(stats, acc):
    out = jnp.concatenate([acc[h * HEAD_DIM:(h + 1) * HEAD_DIM] / stats[h][1] for h in range(N_HEADS)], axis=0)
    return out.T


def _dsa_prompt_kernel(q_ref, qi_ref, wi_ref, kb_ref, vt_ref, kx_ref, o_ref,
                       key_scr, qs_scr, qa_scr, *, tq, kb, topk):
    qt = pl.program_id(1)
    nkc = (qt + 1) * (tq // kb)
    qpos = qt * tq + lax.broadcasted_iota(I32, (kb, tq), 1)
    kofs = lax.broadcasted_iota(I32, (kb, tq), 0)
    lane = lax.broadcasted_iota(I32, (tq, BR_W), 1)
    qi = qi_ref[0]
    for h in range(IDX_HEADS):
        qm = jnp.where(lane // IDX_DIM == h, qi, 0.0)
        rep = (qm + pltpu.roll(qm, IDX_DIM, 1)) + (pltpu.roll(qm, 2 * IDX_DIM, 1) + pltpu.roll(qm, 3 * IDX_DIM, 1))
        q_hi, q_lo = _split(rep)
        qs_scr[h] = jnp.where((lane < IDX_DIM) | ((lane >= 2 * IDX_DIM) & (lane < 3 * IDX_DIM)), q_hi,
                              jnp.where(lane < 2 * IDX_DIM, q_lo, jnp.zeros_like(q_lo)))
    _attn_queries(q_ref[0], qa_scr)
    w_t = wi_ref[0].T * (IDX_HEADS ** -0.5 * IDX_DIM ** -0.5)

    def score_chunk(c, carry):
        kx = kx_ref[0, pl.ds(pl.multiple_of(c * kb, kb), kb), :]
        s = jnp.zeros((kb, tq), F32)
        for h in range(IDX_HEADS):
            s = s + jnp.maximum(_dot_nt(kx, qs_scr[h]), 0.0) * w_t[h:h + 1, :]
        s = jnp.where(c * kb + kofs <= qpos, s, -jnp.inf)
        key_scr[c] = _sort_key(s)
        return carry

    lax.fori_loop(0, nkc, score_chunk, 0)

    def counter(cmp):
        def count(cand):
            def body(c, acc):
                kc = key_scr[c].reshape(4, kb // 32, 8, tq)
                return acc + jnp.sum(jnp.where(cmp(kc, cand), 1.0, 0.0), axis=1)
            acc = lax.fori_loop(0, nkc, body, jnp.zeros((4, 8, tq), F32))
            return jnp.sum(jnp.sum(acc, axis=0), axis=0, keepdims=True)
        return count

    thr = _kth_largest_key(counter(lambda a, b: a >= b), (1, tq), float(topk))
    rem = float(topk) - counter(lambda a, b: a > b)(thr)
    lower = (lax.broadcasted_iota(I32, (kb, kb), 1) <= lax.broadcasted_iota(I32, (kb, kb), 0)).astype(BF16)

    def attn_chunk(c, carry):
        stats, acc, run = carry
        kc = key_scr[c]
        eq = kc == thr
        eqf = jnp.where(eq, 1.0, 0.0)
        pre = _dot(lower, eqf.astype(BF16)) + run
        sel = (kc > thr) | (eq & (pre <= rem))
        msk = sel & (c * kb + kofs <= qpos)
        off = pl.multiple_of(c * kb, kb)
        stats, acc = _flash_update_t(qa_scr, kb_ref[0, pl.ds(off, kb), :], vt_ref[0, c], msk, stats, acc)
        return stats, acc, run + jnp.sum(eqf, axis=0, keepdims=True)

    def attn_group(i, carry):
        stats, acc, run = carry
        c0 = KEY_GROUP * i
        msks = []
        for g in range(KEY_GROUP):
            kc = key_scr[c0 + g]
            eq = kc == thr
            eqf = jnp.where(eq, 1.0, 0.0)
            pre = _dot(lower, eqf.astype(BF16)) + run
            sel = (kc > thr) | (eq & (pre <= rem))
            msks.append(sel & ((c0 + g) * kb + kofs <= qpos))
            run = run + jnp.sum(eqf, axis=0, keepdims=True)
        off = pl.multiple_of(c0 * kb, KEY_GROUP * kb)
        vtg = jnp.concatenate([vt_ref[0, c0 + g] for g in range(KEY_GROUP)], axis=1)
        stats, acc = _flash_update_t(qa_scr, kb_ref[0, pl.ds(off, KEY_GROUP * kb), :], vtg,
                                     jnp.concatenate(msks, axis=0), stats, acc)
        return stats, acc, run

    stats, acc = _flash_init_t(tq)
    ng = nkc // KEY_GROUP
    carry = lax.fori_loop(0, ng, attn_group, (stats, acc, jnp.zeros((1, tq), F32)))
    stats, acc, _ = lax.fori_loop(ng * KEY_GROUP, nkc, attn_chunk, carry)
    o_ref[0] = _flash_finish_t(stats, acc)


def _dsa_prompt(q, qi, wi, kb16, vt16, kx, tq=256, kb=256):
    b, t, _ = q.shape
    topk = min(DSA_TOPK_MAX, t // 4)
    assert tq >= topk and tq == kb
    tile = lambda w: pl.BlockSpec((1, tq, w), lambda i, j: (i, j, 0))
    full = pl.BlockSpec((1, t, BR_W), lambda i, j: (i, 0, 0))
    return pl.pallas_call(
        functools.partial(_dsa_prompt_kernel, tq=tq, kb=kb, topk=topk),
        grid=(b, t // tq),
        in_specs=[tile(BR_W), tile(BR_W), tile(128), full,
                  pl.BlockSpec((1, t // kb, BR_W, kb), lambda i, j: (i, 0, 0, 0)), full],
        out_specs=tile(BR_W),
        out_shape=jax.ShapeDtypeStruct((b, t, BR_W), F32),
        scratch_shapes=[pltpu.VMEM((t // kb, kb, tq), I32),
                        pltpu.VMEM((IDX_HEADS, tq, BR_W), BF16), pltpu.VMEM((N_HEADS, tq, BR_W), BF16)],
        compiler_params=_cparams(("arbitrary", "arbitrary")),
        name="dsa_prompt",
    )(q, qi, wi, kb16, vt16, kx)


def _top_blocks_t(g, row, limit, nsel):
    g = jnp.where(row < limit, g, -jnp.inf)
    selm = jnp.zeros(g.shape, jnp.bool_)
    for _ in range(nsel):
        mx = jnp.max(g, axis=0, keepdims=True)
        idx = jnp.min(jnp.where(g == mx, row, jnp.int32(1 << 20)), axis=0, keepdims=True)
        pick = row == idx
        selm = selm | (pick & (row < limit))
        g = jnp.where(pick, -jnp.inf, g)
    return selm


def _moba_prompt_kernel(q_ref, kbar_ref, kb_ref, vt_ref, o_ref, qa_scr, sel_scr, *, nsel):
    qt = pl.program_id(1)
    tq = MOBA_BLOCK
    qf = q_ref[0]
    kbar = kbar_ref[0]
    row = lax.broadcasted_iota(I32, (kbar.shape[0], tq), 0)
    for h in range(N_HEADS):
        g = _mm3_nt(kbar, jnp.where(_head_mask(h, (1, BR_W)), qf, 0.0))
        sel_scr[h] = jnp.where(_top_blocks_t(g, row, qt, nsel), 1.0, 0.0)
    _attn_queries(qf, qa_scr)

    tri = lax.broadcasted_iota(I32, (tq, tq), 0) <= lax.broadcasted_iota(I32, (tq, tq), 1)
    off = pl.multiple_of(qt * tq, tq)
    stats, acc = _flash_init_t(tq)
    stats, acc = _flash_update_t(qa_scr, kb_ref[0, pl.ds(off, tq), :], vt_ref[0, qt], tri, stats, acc)

    def past(n, carry):
        stats, acc = carry
        o = pl.multiple_of(n * tq, tq)
        allow = [sel_scr[h, pl.ds(n, 1), :] > 0.0 for h in range(N_HEADS)]
        return _flash_update_t(qa_scr, kb_ref[0, pl.ds(o, tq), :], vt_ref[0, n], allow, stats, acc)

    def past_group(i, carry):
        stats, acc = carry
        n0 = KEY_GROUP * i
        o = pl.multiple_of(n0 * tq, KEY_GROUP * tq)
        vtg = jnp.concatenate([vt_ref[0, n0 + g] for g in range(KEY_GROUP)], axis=1)
        allow = [jnp.concatenate([jnp.broadcast_to(sel_scr[h, pl.ds(n0 + g, 1), :] > 0.0, (tq, tq))
                                  for g in range(KEY_GROUP)], axis=0) for h in range(N_HEADS)]
        return _flash_update_t(qa_scr, kb_ref[0, pl.ds(o, KEY_GROUP * tq), :], vtg, allow, stats, acc)

    ng = qt // KEY_GROUP
    stats, acc = lax.fori_loop(0, ng, past_group, (stats, acc))
    stats, acc = lax.fori_loop(ng * KEY_GROUP, qt, past, (stats, acc))
    o_ref[0] = _flash_finish_t(stats, acc)


def _moba_prompt(q, kbar, kb16, vt16):
    b, t, _ = q.shape
    nb = t // MOBA_BLOCK
    nsel = min(MOBA_TOPK, nb - 1)
    nbp = -(-nb // 16) * 16
    kbar = jnp.pad(kbar, ((0, 0), (0, nbp - nb), (0, 0)))
    tile = pl.BlockSpec((1, MOBA_BLOCK, BR_W), lambda i, j: (i, j, 0))
    full = pl.BlockSpec((1, t, BR_W), lambda i, j: (i, 0, 0))
    return pl.pallas_call(
        functools.partial(_moba_prompt_kernel, nsel=nsel),
        grid=(b, nb),
        in_specs=[tile, pl.BlockSpec((1, nbp, BR_W), lambda i, j: (i, 0, 0)), full,
                  pl.BlockSpec((1, nb, BR_W, MOBA_BLOCK), lambda i, j: (i, 0, 0, 0))],
        out_specs=tile,
        out_shape=jax.ShapeDtypeStruct((b, t, BR_W), F32),
        scratch_shapes=[pltpu.VMEM((N_HEADS, MOBA_BLOCK, BR_W), BF16),
                        pltpu.VMEM((N_HEADS, nbp, MOBA_BLOCK), F32)],
        compiler_params=_cparams(("arbitrary", "arbitrary")),
        name="moba_prompt",
    )(q, kbar, kb16, vt16)


def _layer_out_kernel(x_ref, mod_ref, gpre_ref, gpost_ref, ya_ref, bo_ref, yb_ref, yc_ref, yd_ref, gt_ref,
                      lnw_ref, lnb_ref, wb_ref, wm_ref, wo_ref, o_ref):
    x = x_ref[0]
    y = x * lax.rsqrt(jnp.mean(x * x, axis=-1, keepdims=True) + RMS_EPS) * gpre_ref[...]
    shift = mod_ref[0, :, 0:D_MODEL]
    scale = mod_ref[0, :, D_MODEL:2 * D_MODEL]
    gate = mod_ref[0, :, 2 * D_MODEL:3 * D_MODEL]
    hb = (y * (1.0 + scale) + shift).astype(BF16)
    bd = _head_ones()
    ya = ya_ref[0]
    mu = _headsum(ya, bd) * (1.0 / HEAD_DIM)
    d = ya - mu
    var = _headsum(d * d, bd) * (1.0 / HEAD_DIM)
    ya = d * lax.rsqrt(var + RWKV_GN_EPS) * lnw_ref[...] + lnb_ref[...] + bo_ref[0]
    outs = (ya, yb_ref[0], yc_ref[0], yd_ref[0])
    merged = jnp.zeros(x.shape, F32)
    for n in range(4):
        o = outs[n] * jax.nn.silu(gt_ref[0, :, n * BR_W:(n + 1) * BR_W])
        merged = merged + jax.nn.sigmoid(_dot(hb, wm_ref[n])) * _dot(o.astype(BF16), wb_ref[n])
    z = _dot(merged.astype(BF16), wo_ref[...])
    z = z * lax.rsqrt(jnp.mean(z * z, axis=-1, keepdims=True) + RMS_EPS) * gpost_ref[...]
    o_ref[0] = x + gate * z


def _layer_out(x, mod, p, ya, bonus, yb, yc, yd, gates, tm):
    b, t, _ = x.shape
    r = mod.shape[1]
    if r == 1:
        mod_spec = pl.BlockSpec((1, 1, 3 * D_MODEL), lambda i, j: (i, 0, 0))
    else:
        mod_spec = pl.BlockSpec((1, tm, 3 * D_MODEL), lambda i, j: (i, j, 0))
    tile = lambda w: pl.BlockSpec((1, tm, w), lambda i, j: (i, j, 0))
    row = lambda v: v.reshape(1, -1)
    const = lambda shape: pl.BlockSpec(shape, lambda i, j: (0,) * len(shape), pipeline_mode=pl.Buffered(1))
    vec = lambda n: pl.BlockSpec((1, n), lambda i, j: (0, 0))
    return pl.pallas_call(
        _layer_out_kernel,
        grid=(b, t // tm),
        in_specs=[tile(D_MODEL), mod_spec, vec(D_MODEL), vec(D_MODEL),
                  tile(BR_W), tile(BR_W), tile(BR_W), tile(BR_W), tile(BR_W), tile(D_MODEL),
                  vec(BR_W), vec(BR_W),
                  const((4, BR_W, D_MODEL)), const((4, D_MODEL, D_MODEL)), const((D_MODEL, D_MODEL))],
        out_specs=tile(D_MODEL),
        out_shape=jax.ShapeDtypeStruct((b, t, D_MODEL), F32),
        compiler_params=_cparams(("arbitrary", "arbitrary")),
        name="layer_out",
    )(x, mod, row(p['g_pre']), row(p['g_post']), ya, bonus, yb, yc, yd, gates,
      row(p['a_ln_w']), row(p['a_ln_b']),
      p['w_branch'].astype(BF16), p['w_merge'].astype(BF16), p['w_out'].astype(BF16))


PAGES_PER_STEP = 32


def _page_specs(block, g_count):
    tail = (0,) * (len(block) - 1)

    def spec(g):
        return pl.BlockSpec(block, lambda i, j, pt: (pt[i, j * g_count + g],) + tail)
    return [spec(g) for g in range(g_count)]


def _head_rows(x):
    hr = lax.broadcasted_iota(I32, (N_HEADS, BR_W), 0)
    hl = lax.broadcasted_iota(I32, (N_HEADS, BR_W), 1) // HEAD_DIM
    return jnp.where(hr == hl, x, 0.0), hr == hl


def _dsa_scores_kernel(pt_ref, qm_ref, wi_ref, kn_ref, *refs, g_count):
    page_refs = refs[:g_count]
    o_ref, on_ref = refs[g_count], refs[g_count + 1]
    j = pl.program_id(1)
    qm = qm_ref[0]
    q_hi, q_lo = _split(qm)
    w = wi_ref[0] * (IDX_HEADS ** -0.5)

    def combine(d):
        return jnp.sum(jnp.maximum(d * (IDX_DIM ** -0.5), 0.0) * w, axis=0, keepdims=True)

    for g in range(g_count):
        k_hi, k_lo = _split(page_refs[g][0])
        d = _dot_nt(q_hi, k_hi) + (_dot_nt(q_lo, k_hi) + _dot_nt(q_hi, k_lo))
        o_ref[0, g] = combine(d)

    @pl.when(j == 0)
    def _():
        dn = jnp.sum(qm * kn_ref[0], axis=-1, keepdims=True)
        on_ref[0] = jnp.broadcast_to(combine(dn), (1, 128))


def _dsa_scores(qi, wi, ki_new, pool, page_table):
    b, n_pages = page_table.shape
    g_count = min(PAGES_PER_STEP, n_pages)
    qm = qi.reshape(b, IDX_HEADS, IDX_DIM)
    wcol = wi.reshape(b, IDX_HEADS, 1)
    kn = ki_new.reshape(b, 1, IDX_DIM)
    grid_spec = pltpu.PrefetchScalarGridSpec(
        num_scalar_prefetch=1,
        grid=(b, n_pages // g_count),
        in_specs=[pl.BlockSpec((1, IDX_HEADS, IDX_DIM), lambda i, j, pt: (i, 0, 0)),
                  pl.BlockSpec((1, IDX_HEADS, 1), lambda i, j, pt: (i, 0, 0)),
                  pl.BlockSpec((1, 1, IDX_DIM), lambda i, j, pt: (i, 0, 0))]
                 + _page_specs((1, PAGE_SIZE, IDX_DIM), g_count),
        out_specs=(pl.BlockSpec((1, g_count, 1, PAGE_SIZE), lambda i, j, pt: (i, j, 0, 0)),
                   pl.BlockSpec((1, 1, 128), lambda i, j, pt: (i, 0, 0))))
    sc, sc_new = pl.pallas_call(
        functools.partial(_dsa_scores_kernel, g_count=g_count),
        grid_spec=grid_spec,
        out_shape=(jax.ShapeDtypeStruct((b, n_pages, 1, PAGE_SIZE), F32),
                   jax.ShapeDtypeStruct((b, 1, 128), F32)),
        compiler_params=_cparams(("arbitrary", "arbitrary")),
        name="dsa_scores",
    )(page_table, qm, wcol, kn, *([pool] * g_count))
    return sc.reshape(b, n_pages * PAGE_SIZE), sc_new[:, 0, 0:1]


def _topk_rows_kernel(s_ref, o_ref, key_scr, *, n_valid, topk):
    nblk, rows, _ = s_ref.shape
    col = lax.broadcasted_iota(I32, (rows, 128), 1)

    def to_key(j, carry):
        s = jnp.where(j * 128 + col < n_valid, s_ref[j], -jnp.inf)
        key_scr[j] = _sort_key(s)
        return carry

    lax.fori_loop(0, nblk, to_key, 0)

    def counter(cmp):
        def count(cand):
            body = lambda j, acc: acc + jnp.where(cmp(key_scr[j], cand), 1.0, 0.0)
            acc = lax.fori_loop(0, nblk, body, jnp.zeros((rows, 128), F32))
            return jnp.sum(acc, axis=-1, keepdims=True)
        return count

    tb = _kth_largest_key(counter(lambda a, b: a >= b), (rows, 128), float(topk))
    rem = float(topk) - counter(lambda a, b: a > b)(tb)
    upper = (lax.broadcasted_iota(I32, (128, 128), 0) <= lax.broadcasted_iota(I32, (128, 128), 1)).astype(BF16)

    def select(j, run):
        kc = key_scr[j]
        eq = kc == tb
        eqf = jnp.where(eq, 1.0, 0.0)
        pre = _dot(eqf.astype(BF16), upper) + run
        sel = ((kc > tb) | (eq & (pre <= rem))) & (j * 128 + col < n_valid)
        o_ref[j] = jnp.where(sel, 1.0, 0.0)
        return run + jnp.sum(eqf, axis=-1, keepdims=True)

    lax.fori_loop(0, nblk, select, jnp.zeros((rows, 1), F32))


def _topk_rows(scores, topk):
    rows, n = scores.shape
    nblk = -(-n // 128)
    s = jnp.pad(scores, ((0, 0), (0, nblk * 128 - n)))
    s = jnp.transpose(s.reshape(rows, nblk, 128), (1, 0, 2))
    m = pl.pallas_call(
        functools.partial(_topk_rows_kernel, n_valid=n, topk=topk),
        in_specs=[pl.BlockSpec((nblk, rows, 128), lambda: (0, 0, 0))],
        out_specs=pl.BlockSpec((nblk, rows, 128), lambda: (0, 0, 0)),
        out_shape=jax.ShapeDtypeStruct((nblk, rows, 128), F32),
        scratch_shapes=[pltpu.VMEM((nblk, rows, 128), I32)],
        compiler_params=pltpu.CompilerParams(vmem_limit_bytes=VMEM_LIMIT_BYTES),
        name="topk_rows",
    )(s)
    return jnp.transpose(m, (1, 0, 2)).reshape(rows, nblk * 128)[:, :n]


def _dsa_attn_kernel(pt_ref, q_ref, kn_ref, vn_ref, mn_ref, msk_ref, *refs, g_count):
    k_refs = refs[:g_count]
    v_refs = refs[g_count:2 * g_count]
    o_ref = refs[2 * g_count]
    m_scr, l_scr, acc_scr = refs[2 * g_count + 1:]
    j = pl.program_id(1)
    qbd, hsel = _head_rows(q_ref[0])

    @pl.when(j == 0)
    def _():
        sn = jnp.sum(qbd * kn_ref[0], axis=-1, keepdims=True) * (HEAD_DIM ** -0.5)
        ok = mn_ref[0][:, 0:1] > 0.0
        m_scr[...] = jnp.broadcast_to(jnp.where(ok, sn, NEG), (N_HEADS, 128))
        l_scr[...] = jnp.broadcast_to(jnp.where(ok, 1.0, 0.0), (N_HEADS, 128))
        acc_scr[...] = jnp.where(ok, jnp.broadcast_to(vn_ref[0], (N_HEADS, BR_W)), 0.0)

    kcat = jnp.concatenate([r[0] for r in k_refs], axis=0).astype(BF16)
    vcat = jnp.concatenate([r[0] for r in v_refs], axis=0).astype(BF16)
    mk = msk_ref[0] > 0.0
    s = jnp.where(mk, _dot_nt(qbd.astype(BF16), kcat) * (HEAD_DIM ** -0.5), NEG)
    m_old = m_scr[:, 0:1]
    m_new = jnp.maximum(m_old, jnp.max(s, axis=-1, keepdims=True))
    alpha = jnp.exp(m_old - m_new)
    p = jnp.where(mk, jnp.exp(s - m_new), 0.0)
    l_new = alpha * l_scr[:, 0:1] + jnp.sum(p, axis=-1, keepdims=True)
    acc = alpha * acc_scr[...] + _dot(p.astype(BF16), vcat)
    m_scr[...] = jnp.broadcast_to(m_new, (N_HEADS, 128))
    l_scr[...] = jnp.broadcast_to(l_new, (N_HEADS, 128))
    acc_scr[...] = acc

    @pl.when(j == pl.num_programs(1) - 1)
    def _():
        o_ref[0] = jnp.sum(jnp.where(hsel, acc / l_new, 0.0), axis=0, keepdims=True)


def _dsa_attn(q, k_new, v_new, mask, k_pool, v_pool, page_table):
    b, n_pages = page_table.shape
    g_count = min(PAGES_PER_STEP, n_pages)
    past = n_pages * PAGE_SIZE
    row = lambda a: a.reshape(b, 1, BR_W)
    m_new = jnp.broadcast_to(mask[:, past:past + 1], (b, 128)).reshape(b, 1, 128)
    m_past = mask[:, :past].reshape(b, 1, past)
    rspec = pl.BlockSpec((1, 1, BR_W), lambda i, j, pt: (i, 0, 0))
    grid_spec = pltpu.PrefetchScalarGridSpec(
        num_scalar_prefetch=1,
        grid=(b, n_pages // g_count),
        in_specs=[rspec, rspec, rspec,
                  pl.BlockSpec((1, 1, 128), lambda i, j, pt: (i, 0, 0)),
                  pl.BlockSpec((1, 1, g_count * PAGE_SIZE), lambda i, j, pt: (i, 0, j))]
                 + _page_specs((1, PAGE_SIZE, BR_W), g_count) * 2,
        out_specs=rspec,
        scratch_shapes=[pltpu.VMEM((N_HEADS, 128), F32), pltpu.VMEM((N_HEADS, 128), F32),
                        pltpu.VMEM((N_HEADS, BR_W), F32)])
    out = pl.pallas_call(
        functools.partial(_dsa_attn_kernel, g_count=g_count),
        grid_spec=grid_spec,
        out_shape=jax.ShapeDtypeStruct((b, 1, BR_W), F32),
        compiler_params=_cparams(("arbitrary", "arbitrary")),
        name="dsa_attn",
    )(page_table, row(q), row(k_new), row(v_new), m_new, m_past, *([k_pool] * g_count), *([v_pool] * g_count))
    return out.reshape(b, BR_W)


def _kbar_kernel(pt_ref, *refs, g_count):
    page_refs = refs[:g_count]
    o_ref = refs[g_count]
    for g2 in range(g_count // 2):
        s = (jnp.sum(page_refs[2 * g2][0].astype(F32), axis=0, keepdims=True)
             + jnp.sum(page_refs[2 * g2 + 1][0].astype(F32), axis=0, keepdims=True))
        o_ref[0, g2] = s * (1.0 / MOBA_BLOCK)


def _moba_kbar(k_pool, page_table):
    b, n_pages = page_table.shape
    g_count = min(PAGES_PER_STEP, n_pages)
    grid_spec = pltpu.PrefetchScalarGridSpec(
        num_scalar_prefetch=1,
        grid=(b, n_pages // g_count),
        in_specs=_page_specs((1, PAGE_SIZE, BR_W), g_count),
        out_specs=pl.BlockSpec((1, g_count // 2, 1, BR_W), lambda i, j, pt: (i, j, 0, 0)))
    out = pl.pallas_call(
        functools.partial(_kbar_kernel, g_count=g_count),
        grid_spec=grid_spec,
        out_shape=jax.ShapeDtypeStruct((b, n_pages // 2, 1, BR_W), F32),
        compiler_params=_cparams(("arbitrary", "arbitrary")),
        name="moba_kbar",
    )(page_table, *([k_pool] * g_count))
    return out.reshape(b, n_pages // 2, BR_W)


def _moba_gate_kernel(q_ref, kbar_ref, o_ref, *, n_past, nsel):
    qbd, _ = _head_rows(q_ref[0])
    g = _mm3_nt(qbd, kbar_ref[0])
    col = lax.broadcasted_iota(I32, (N_HEADS, 128), 1)
    g = jnp.where(col < n_past, g, -jnp.inf)
    out = jnp.full((N_HEADS, 128), -1, I32)
    for i in range(nsel):
        mx = jnp.max(g, axis=-1, keepdims=True)
        idx = jnp.min(jnp.where(g == mx, col, jnp.int32(1 << 20)), axis=-1, keepdims=True)
        ok = idx < n_past
        out = jnp.where(col == i, jnp.where(ok, idx, -1), out)
        g = jnp.where(col == idx, -jnp.inf, g)
    o_ref[0] = out


def _moba_gate(q, kbar, nsel):
    b, n_past, _ = kbar.shape
    kb = jnp.pad(kbar, ((0, 0), (0, 128 - n_past), (0, 0)))
    out = pl.pallas_call(
        functools.partial(_moba_gate_kernel, n_past=n_past, nsel=nsel),
        grid=(b,),
        in_specs=[pl.BlockSpec((1, 1, BR_W), lambda i: (i, 0, 0)),
                  pl.BlockSpec((1, 128, BR_W), lambda i: (i, 0, 0))],
        out_specs=pl.BlockSpec((1, N_HEADS, 128), lambda i: (i, 0, 0)),
        out_shape=jax.ShapeDtypeStruct((b, N_HEADS, 128), I32),
        compiler_params=_cparams(("arbitrary",)),
        name="moba_gate",
    )(q.reshape(b, 1, BR_W), kb)
    return out[:, :, :nsel]


def _moba_attn_kernel(sel_ref, pt_ref, q_ref, kn_ref, vn_ref, *refs, nsel):
    n_pg = 2 * nsel
    k_refs = refs[:n_pg]
    v_refs = refs[n_pg:2 * n_pg]
    o_ref = refs[2 * n_pg]
    bi = pl.program_id(0)
    h = pl.program_id(1)
    hm = lax.broadcasted_iota(I32, (1, BR_W), 1) // HEAD_DIM == h
    qh = jnp.where(hm, q_ref[0], 0.0)
    sn = jnp.sum(qh * kn_ref[0], axis=-1, keepdims=True) * (HEAD_DIM ** -0.5)
    kcat = jnp.concatenate([r[0] for r in k_refs], axis=0).astype(BF16)
    vcat = jnp.concatenate([r[0] for r in v_refs], axis=0).astype(BF16)
    s = _dot_nt(qh.astype(BF16), kcat) * (HEAD_DIM ** -0.5)
    blk = lax.broadcasted_iota(I32, s.shape, 1) // MOBA_BLOCK
    mk = jnp.zeros(s.shape, jnp.bool_)
    for i in range(nsel):
        mk = mk | (blk == jnp.where(sel_ref[(bi * N_HEADS + h) * nsel + i] >= 0, i, -1))
    s = jnp.where(mk, s, NEG)
    m = jnp.maximum(jnp.max(s, axis=-1, keepdims=True), sn)
    p = jnp.where(mk, jnp.exp(s - m), 0.0)
    pn = jnp.exp(sn - m)
    out = (_dot(p.astype(BF16), vcat) + pn * vn_ref[0]) / (jnp.sum(p, axis=-1, keepdims=True) + pn)

    @pl.when(h == 0)
    def _():
        o_ref[0] = jnp.zeros((1, BR_W), F32)

    o_ref[0] = o_ref[0] + jnp.where(hm, out, 0.0)


def _moba_attn(q, k_new, v_new, sel, k_pool, v_pool, page_table):
    b, n_pages = page_table.shape
    nsel = sel.shape[-1]
    row = lambda a: a.reshape(b, 1, BR_W)

    def pspec(i, half):
        def imap(bi, h, sel_ref, pt):
            blk = jnp.maximum(sel_ref[(bi * N_HEADS + h) * nsel + i], 0)
            return (pt[bi, 2 * blk + half], 0, 0)
        return pl.BlockSpec((1, PAGE_SIZE, BR_W), imap)

    pages = [pspec(i, half) for i in range(nsel) for half in range(2)]
    rspec = pl.BlockSpec((1, 1, BR_W), lambda bi, h, s, pt: (bi, 0, 0))
    grid_spec = pltpu.PrefetchScalarGridSpec(
        num_scalar_prefetch=2,
        grid=(b, N_HEADS),
        in_specs=[rspec, rspec, rspec] + pages + pages,
        out_specs=rspec)
    out = pl.pallas_call(
        functools.partial(_moba_attn_kernel, nsel=nsel),
        grid_spec=grid_spec,
        out_shape=jax.ShapeDtypeStruct((b, 1, BR_W), F32),
        compiler_params=_cparams(("arbitrary", "arbitrary")),
        name="moba_attn",
    )(sel.reshape(-1), page_table, row(q), row(k_new), row(v_new),
      *([k_pool] * (2 * nsel)), *([v_pool] * (2 * nsel)))
    return out.reshape(b, BR_W)


def _sample_layer(x, c, p, a_shift, a_wkv, b_ret, ck_pool, cv_pool, cki_pool, dk_pool, dv_pool, page_table):
    b = x.shape[0]
    n_pages = page_table.shape[1]
    past = n_pages * PAGE_SIZE
    assert x.shape[1] == 1 and past % MOBA_BLOCK == 0
    pos = jnp.full((b,), past, I32)
    tab_ret = _rope_tables(pos, HEAD_DIM, RET_THETA, BR_W)
    tab_std = _rope_tables(pos, ROPE_DIMS, ROPE_THETA, BR_W)
    w_hi, w_lo = _regroup_w_in(p['w_in'])
    mod = _ada(c, p['w_ada'], p['b_ada']).reshape(1, b, 3 * D_MODEL)
    xr = x.reshape(1, b, D_MODEL)
    outs = _layer_in(xr, mod, p['g_pre'], w_hi, w_lo, tab_ret, tab_std, b)
    (ua, gates, qb, kb, vb, qc, kc, vc, qi, ki4, wi, qd, kd, vd) = [o[0] for o in outs]
    r, w, k2, v, kk, bb, bonus = _rwkv_pre_rows(ua, a_shift, p)
    ya, wkv = _rwkv_step(r, w, k2, v, kk, bb, a_wkv)
    yb, ret = _retention_step(qb, kb, vb, b_ret, p['b_gn_w'], p['b_gn_b'])
    ki = ki4[:, :IDX_DIM]
    sc_past, sc_new = _dsa_scores(qi, wi[:, :IDX_HEADS], ki, cki_pool, page_table)
    total = past + 1
    mask = _topk_rows(jnp.concatenate([sc_past, sc_new], axis=1), min(DSA_TOPK_MAX, total // 4))
    yc = _dsa_attn(qc, kc, vc, mask, ck_pool, cv_pool, page_table)
    n_past_blocks = past // MOBA_BLOCK
    nsel = min(MOBA_TOPK, n_past_blocks)
    if nsel > 0:
        sel = _moba_gate(qd, _moba_kbar(dk_pool, page_table), nsel)
        yd = _moba_attn(qd, kd, vd, sel, dk_pool, dv_pool, page_table)
    else:
        yd = vd
    row = lambda a: a.reshape(1, b, -1)
    x_new = _layer_out(xr, mod, p, row(ya), row(bonus), row(yb), row(yc), row(yd), gates.reshape(1, b, -1), b)
    heads = lambda a: a.reshape(b, 1, N_HEADS, HEAD_DIM)
    new_state = (ua, wkv, ret, heads(kc), heads(vc), ki.reshape(b, 1, IDX_DIM), heads(kd), heads(vd))
    return x_new.reshape(b, 1, D_MODEL), new_state


def _prompt_layer(x, c, p):
    b, t, _ = x.shape
    pos = jnp.arange(t)
    tab_ret = _rope_tables(pos, HEAD_DIM, RET_THETA, BR_W)
    tab_std = _rope_tables(pos, ROPE_DIMS, ROPE_THETA, BR_W)
    w_hi, w_lo = _regroup_w_in(p['w_in'])
    mod = _ada(c, p['w_ada'], p['b_ada']).reshape(b, 1, 3 * D_MODEL)
    tm = MOBA_BLOCK
    assert t % tm == 0
    (ua, gates, qb, kb, vb, qc, kc, vc, qi, ki4, wi, qd, kd, vd, kcb, vct, kix, kdb, vdt, kbar) = _layer_in(
        x, mod, p['g_pre'], w_hi, w_lo, tab_ret, tab_std, tm, attention_operands=True)
    r, w, k2, v, kk, bb, bonus = _rwkv_pre_prompt(ua, jnp.zeros((b, A_SHIFT_W), F32), p, tm)
    ya, wkv = _rwkv_scan(r, w, k2, v, kk, bb, jnp.zeros((b, HEAD_DIM, BR_W), F32), min(64, t))
    yb, ret = _retention_prompt(qb, kb, vb, jnp.zeros((b, N_HEADS, HEAD_DIM, HEAD_DIM), F32),
                                p['b_gn_w'], p['b_gn_b'], min(256, t))
    yc = _dsa_prompt(qc, qi, wi, kcb, vct, kix)
    yd = _moba_prompt(qd, kbar.reshape(b, t // tm, BR_W), kdb, vdt)
    x_new = _layer_out(x, mod, p, ya, bonus, yb, yc, yd, gates, tm)
    heads = lambda a: a.reshape(b, t, N_HEADS, HEAD_DIM)
    new_state = (ua[:, -1], _wkv_from_scan_layout(wkv), ret, heads(kc), heads(vc), ki4[..., :IDX_DIM],
                 heads(kd), heads(vd))
    return x_new, new_state


_PARAM_NAMES = ('w_ada', 'b_ada', 'g_pre', 'g_post', 'w_in', 'a_mu', 'a_w0', 'a_w2', 'a_a0', 'a_a2', 'a_kk', 'a_ka',
                'a_rk', 'a_ln_w', 'a_ln_b', 'b_gn_w', 'b_gn_b', 'w_branch', 'w_merge', 'w_out')


def kernel(x_prompt, x_sample, c_prompt, c_sample, state_a_shift, state_a_wkv, state_b_ret, cache_c_k, cache_c_v,
           cache_c_kidx, cache_d_k, cache_d_v, page_table, w_ada, b_ada, g_pre, g_post, w_in, a_mu, a_w0, a_w2,
           a_a0, a_a2, a_kk, a_ka, a_rk, a_ln_w, a_ln_b, b_gn_w, b_gn_b, w_branch, w_merge, w_out):
    stacked = dict(zip(_PARAM_NAMES, (w_ada, b_ada, g_pre, g_post, w_in, a_mu, a_w0, a_w2, a_a0, a_a2, a_kk, a_ka,
                                      a_rk, a_ln_w, a_ln_b, b_gn_w, b_gn_b, w_branch, w_merge, w_out)))
    depth = w_in.shape[0]
    n_pool = cache_c_k.shape[1]
    fold = lambda a: a.reshape(depth * n_pool, PAGE_SIZE, -1)
    ck, cv, cki, dk, dv = (fold(a) for a in (cache_c_k, cache_c_v, cache_c_kidx, cache_d_k, cache_d_v))
    xp, xs = x_prompt, x_sample
    p_new, s_new = [], []
    for l in range(depth):
        p = {name: val[l] for name, val in stacked.items()}
        xp, st_p = _prompt_layer(xp, c_prompt, p)
        xs, st_s = _sample_layer(xs, c_sample, p, state_a_shift[l], state_a_wkv[l], state_b_ret[l],
                                 ck, cv, cki, dk, dv, page_table + l * n_pool)
        p_new.append(st_p)
        s_new.append(st_s)
    stack = lambda states, i: jnp.stack([s[i] for s in states])
    return ((xp, xs) + tuple(stack(p_new, i) for i in range(8)) + tuple(stack(s_new, i) for i in range(8)))
```
